```python
import math
import jax
import jax.numpy as jnp
from jax import lax
import numpy as np

D_MODEL = 1024
BATCH = 4
SEQ = 4096
DEPTH = 2

GRID_W = 64
CTX_LEN = 256
N_BRANCH = 4
MIX_W = D_MODEL // 2
GDN_HEAD_DIM = 128
GDN_HEADS = MIX_W // GDN_HEAD_DIM
HGRN_HEAD_DIM = 128
HGRN_HEADS = MIX_W // HGRN_HEAD_DIM
LRU_BLOCKS = 8
LRU_BLOCK_W = MIX_W // LRU_BLOCKS
LRU_C = 8.0
RET_HEAD_DIM = 128
RET_HEADS = MIX_W // RET_HEAD_DIM
CONV_W = 4
CHUNK = 64
ROPE_BASE = 10000.0
N_EXPERTS = 32
TOP_K = 4
EXPERT_FF = D_MODEL
SWIGLU_ALPHA = 1.702
SWIGLU_LIMIT = 7.0
EXPERT_BLOCK = 128
N_MOD = 6
EPS = 1e-6
LB_MIN = 1e-30
LB_MAX = 1.0 - 1e-4
SQRT_MIN = 1e-12
IN_SIZES = (3 * MIX_W, MIX_W, 2 * GDN_HEADS, 2 * GDN_HEADS,
            MIX_W, MIX_W, 2 * MIX_W, MIX_W,
            MIX_W, MIX_W,
            3 * MIX_W, MIX_W,
            N_BRANCH * D_MODEL)
N_IN = sum(IN_SIZES)

kernel_name = "hybrid_flow_backbone_gdn_hgrn2_rglru_retention_moe"


def rms_norm(x, g):
    x32 = x.astype(jnp.float32)
    y = x32 * lax.rsqrt(jnp.mean(x32 * x32, axis=-1, keepdims=True) + EPS)
    return (y * g.astype(jnp.float32)).astype(x.dtype)


def l2_normalize(x):
    return x * lax.rsqrt(jnp.sum(x * x, axis=-1, keepdims=True) + EPS)


def depthwise_conv(x, w):
    left = CONV_W // 2
    return lax.conv_general_dilated(
        x, w[:, None, :].astype(x.dtype), window_strides=(1,),
        padding=[(left, CONV_W - 1 - left)],
        dimension_numbers=("NWC", "WIO", "NWC"),
        feature_group_count=x.shape[-1])


def to_heads(a, n_heads):
    b, t, _ = a.shape
    return a.reshape(b, t, n_heads, -1).transpose(0, 2, 1, 3)


def from_heads(a):
    b, h, t, d = a.shape
    return a.transpose(0, 2, 1, 3).reshape(b, t, h * d)


def to_chunks(a):
    b, h, t = a.shape[:3]
    a = a.reshape((b, h, t // CHUNK, CHUNK) + a.shape[3:])
    return jnp.moveaxis(a, 2, 0)


def from_chunks(a):
    a = jnp.moveaxis(a, 0, 2)
    return a.reshape(a.shape[:2] + (a.shape[2] * a.shape[3],) + a.shape[4:])


def masked_decay(diff, mask):
    return jnp.where(mask, jnp.exp(jnp.where(mask, diff, 0.0)), 0.0)


def gdn_scan(inputs, s0):
    causal = jnp.tril(jnp.ones((CHUNK, CHUNK), dtype=bool))
    strict = jnp.tril(jnp.ones((CHUNK, CHUNK), dtype=bool), k=-1)

    def step(s, blk):
        qb, kb, vb, gb, betab = blk
        dv = vb.shape[-1]
        bcum = jnp.cumsum(gb, axis=-1)
        decay = masked_decay(bcum[..., :, None] - bcum[..., None, :], causal)
        kbeta = kb * betab[..., None]
        m = jnp.where(strict, jnp.einsum("bhtd,bhsd->bhts", kbeta, kb) * decay, 0.0)
        rhs = jnp.concatenate([vb * betab[..., None], kbeta * jnp.exp(bcum)[..., None]], axis=-1)
        sol = lax.linalg.triangular_solve(m, rhs, left_side=True, lower=True, unit_diagonal=True)
        u, w = sol[..., :dv], sol[..., dv:]
        v_new = u - jnp.einsum("bhtd,bhde->bhte", w, s)
        attn = jnp.einsum("bhtd,bhsd->bhts", qb, kb) * decay
        o = (jnp.einsum("bhtd,bhde->bhte", qb * jnp.exp(bcum)[..., None], s)
             + jnp.einsum("bhts,bhse->bhte", attn, v_new))
        blast = bcum[..., -1:]
        s = s * jnp.exp(blast)[..., None] + jnp.einsum(
            "bhsd,bhse->bhde", kb * jnp.exp(blast - bcum)[..., None], v_new)
        return s, o

    s_fin, o = lax.scan(step, s0, tuple(to_chunks(a) for a in inputs))
    return from_chunks(o), s_fin


def gla_scan(inputs, s0):
    causal = jnp.tril(jnp.ones((CHUNK, CHUNK), dtype=bool))[:, :, None]

    def step(s, blk):
        qb, kb, vb, fb = blk
        bcum = jnp.cumsum(fb, axis=2)
        decay = masked_decay(bcum[:, :, :, None, :] - bcum[:, :, None, :, :], causal)
        scores = jnp.einsum("bhtd,bhtsd,bhsd->bhts", qb, decay, kb)
        o = (jnp.einsum("bhts,bhse->bhte", scores, vb)
             + jnp.einsum("bhtd,bhde->bhte", qb * jnp.exp(bcum), s))
        blast = bcum[:, :, -1:, :]
        s = s * jnp.exp(blast[:, :, 0, :, None]) + jnp.einsum(
            "bhsd,bhse->bhde", kb * jnp.exp(blast - bcum), vb)
        return s, o

    s_fin, o = lax.scan(step, s0, tuple(to_chunks(a) for a in inputs))
    return from_chunks(o), s_fin


def retention_scan(inputs, s0):
    causal = jnp.tril(jnp.ones((CHUNK, CHUNK), dtype=bool))

    def step(s, blk):
        qb, kb, vb, gb = blk
        bcum = jnp.cumsum(gb, axis=-1)
        decay = masked_decay(bcum[..., :, None] - bcum[..., None, :], causal)
        scores = jnp.einsum("bhtd,bhsd->bhts", qb, kb) * decay
        o = (jnp.einsum("bhts,bhse->bhte", scores, vb)
             + jnp.einsum("bhtd,bhde->bhte", qb * jnp.exp(bcum)[..., None], s))
        blast = bcum[..., -1:]
        s = s * jnp.exp(blast)[..., None] + jnp.einsum(
            "bhsd,bhse->bhde", kb * jnp.exp(blast - bcum)[..., None], vb)
        return s, o

    s_fin, o = lax.scan(step, s0, tuple(to_chunks(a) for a in inputs))
    return from_chunks(o), s_fin


def lru_scan(inputs, h0):
    a, u = inputs
    u = u.at[:, 0].add(a[:, 0] * h0)
    _, h = lax.associative_scan(lambda l, r: (l[0] * r[0], r[0] * l[1] + r[1]), (a, u), axis=1)
    return h, h[:, -1]


def two_stream_bidir(scan_fn, ctx_dirs, lat_dirs, s0, axis):
    rev = lambda a: jnp.flip(a, axis=axis)
    o_cf, s_f = scan_fn(ctx_dirs[0], s0)
    o_cb, s_b = scan_fn(tuple(rev(a) for a in ctx_dirs[1]), s0)
    o_lf, _ = scan_fn(lat_dirs[0], s_f)
    o_lb, _ = scan_fn(tuple(rev(a) for a in lat_dirs[1]), s_b)
    return o_cf + rev(o_cb), o_lf + rev(o_lb)


def axial_rope_tables(n_rows):
    row = jnp.repeat(jnp.arange(n_rows, dtype=jnp.float32), GRID_W)
    col = jnp.tile(jnp.arange(GRID_W, dtype=jnp.float32), n_rows)
    n_freq = RET_HEAD_DIM // 4
    inv_freq = ROPE_BASE ** (-jnp.arange(n_freq, dtype=jnp.float32) / n_freq)
    ang = jnp.concatenate([row[:, None] * inv_freq, col[:, None] * inv_freq], axis=-1)
    return jnp.cos(ang), jnp.sin(ang)


def apply_rope(x, cos, sin):
    half = x.shape[-1] // 2
    x1, x2 = x[..., :half], x[..., half:]
    return jnp.concatenate([x1 * cos - x2 * sin, x1 * sin + x2 * cos], axis=-1)


def prepare_stream(h, lp, lb, rope):
    f32 = jnp.float32
    b, t, _ = h.shape
    p = h @ lp["w_in"]
    cols, off = [], 0
    for size in IN_SIZES:
        cols.append(p[..., off:off + size])
        off += size
    (gdn_qkv, gdn_z, gdn_a, gdn_beta, hg_q, hg_i, hg_f, hg_g,
     lru_x, lru_y, ret_qkv, ret_g, merge) = cols

    qkv = jax.nn.silu(depthwise_conv(gdn_qkv, lp["gdn_conv_w"])).astype(f32)
    q, k, v = jnp.split(qkv, 3, axis=-1)
    q = l2_normalize(to_heads(q, GDN_HEADS)) * GDN_HEAD_DIM ** -0.5
    k = l2_normalize(to_heads(k, GDN_HEADS))
    v = to_heads(v, GDN_HEADS)
    a = gdn_a.astype(f32).reshape(b, t, 2, GDN_HEADS).transpose(2, 0, 3, 1)
    g = -jnp.exp(lp["gdn_a_log"].astype(f32))[:, None, :, None] * jax.nn.softplus(
        a + lp["gdn_dt_bias"].astype(f32)[:, None, :, None])
    beta = jax.nn.sigmoid(gdn_beta.astype(f32).reshape(b, t, 2, GDN_HEADS).transpose(2, 0, 3, 1))
    gdn = [(q, k, v, g[d], beta[d]) for d in range(2)]

    hq = to_heads(jax.nn.silu(hg_q.astype(f32)), HGRN_HEADS) * HGRN_HEAD_DIM ** -0.5
    hv = to_heads(hg_i.astype(f32), HGRN_HEADS)
    zf = hg_f.astype(f32).reshape(b, t, 2, MIX_W)
    log_lb = jnp.log(jnp.maximum(lb, LB_MIN))
    log_f = jnp.logaddexp(log_lb, jnp.log1p(-lb) + jax.nn.log_sigmoid(zf))
    key = (1.0 - lb) * jax.nn.sigmoid(-zf)
    hgrn = [(hq, to_heads(key[:, :, d], HGRN_HEADS), hv, to_heads(log_f[:, :, d], HGRN_HEADS))
            for d in range(2)]

    xb = (depthwise_conv(lru_x, lp["lru_conv_w"]) + lp["lru_conv_b"]).astype(f32)
    xr = xb.reshape(b, t, LRU_BLOCKS, LRU_BLOCK_W)
    gate_r = jax.nn.sigmoid(jnp.einsum("btni,dnij->dbtnj", xr, lp["lru_wa"].astype(f32)).reshape(2, b, t, MIX_W)
                            + lp["lru_ba"].astype(f32)[:, None, None, :])
    gate_i = jax.nn.sigmoid(jnp.einsum("btni,dnij->dbtnj", xr, lp["lru_wx"].astype(f32)).reshape(2, b, t, MIX_W)
                            + lp["lru_bx"].astype(f32)[:, None, None, :])
    log_a = LRU_C * gate_r * jax.nn.log_sigmoid(lp["lru_lambda"].astype(f32))[:, None, None, :]
    u = xb[None] * gate_i * jnp.sqrt(jnp.maximum(-jnp.expm1(2.0 * log_a), SQRT_MIN))
    lru = [(jnp.exp(log_a[d]), u[d]) for d in range(2)]

    rq, rk, rv = (to_heads(a_, RET_HEADS) for a_ in jnp.split(ret_qkv.astype(f32), 3, axis=-1))
    if rope is not None:
        rq = apply_rope(rq, rope[0], rope[1])
        rk = apply_rope(rk, rope[0], rope[1])
    rk = rk * RET_HEAD_DIM ** -0.5
    log_gamma = jnp.log1p(-jnp.exp2(-5.0 - jnp.arange(RET_HEADS, dtype=f32)))
    ret = (rq, rk, rv, jnp.broadcast_to(log_gamma[None, :, None], (b, RET_HEADS, t)))

    gates = {"gdn_z": gdn_z, "hgrn_g": hg_g, "lru_y": lru_y, "ret_g": ret_g, "merge": merge}
    return gdn, hgrn, lru, ret, gates


def merge_branches(o_gdn, o_hgrn, h_lru, o_ret, gates, lp, dtype):
    f32 = jnp.float32
    b, t, _ = h_lru.shape

    def gated_head_norm(o, z, g, n_heads):
        return from_heads(rms_norm(o, g) * jax.nn.silu(to_heads(z.astype(f32), n_heads)))

    y_gdn = gated_head_norm(o_gdn, gates["gdn_z"], lp["gdn_norm_g"], GDN_HEADS)
    y_hgrn = gated_head_norm(o_hgrn, gates["hgrn_g"], lp["hgrn_norm_g"], HGRN_HEADS)
    y_lru = h_lru * jax.nn.gelu(gates["lru_y"].astype(f32))
    y_ret = gated_head_norm(o_ret, gates["ret_g"], lp["ret_norm_g"], RET_HEADS)
    branches = jnp.stack([y_gdn, y_hgrn, y_lru, y_ret], axis=2).astype(dtype)
    up = jnp.einsum("btnw,nwd->btnd", branches, lp["w_branch"])
    gate = jax.nn.sigmoid(gates["merge"].reshape(b, t, N_BRANCH, D_MODEL))
    return jnp.sum(gate * up, axis=2) @ lp["w_out"]


def token_mixer(h_lat, h_ctx, lp, lb, rope, with_ctx_output):
    f32 = jnp.float32
    b = h_lat.shape[0]
    gdn_l, hgrn_l, lru_l, ret_l, gates_l = prepare_stream(h_lat, lp, lb, rope)
    gdn_c, hgrn_c, lru_c, ret_c, gates_c = prepare_stream(h_ctx, lp, lb, None)
    s_gdn = jnp.zeros((b, GDN_HEADS, GDN_HEAD_DIM, GDN_HEAD_DIM), f32)
    s_hgrn = jnp.zeros((b, HGRN_HEADS, HGRN_HEAD_DIM, HGRN_HEAD_DIM), f32)
    s_lru = jnp.zeros((b, MIX_W), f32)
    s_ret = jnp.zeros((b, RET_HEADS, RET_HEAD_DIM, RET_HEAD_DIM), f32)
    oc_gdn, ol_gdn = two_stream_bidir(gdn_scan, gdn_c, gdn_l, s_gdn, 2)
    oc_hgrn, ol_hgrn = two_stream_bidir(gla_scan, hgrn_c, hgrn_l, s_hgrn, 2)
    oc_lru, ol_lru = two_stream_bidir(lru_scan, lru_c, lru_l, s_lru, 1)
    oc_ret, ol_ret = two_stream_bidir(retention_scan, [ret_c, ret_c], [ret_l, ret_l], s_ret, 2)
    y_lat = merge_branches(ol_gdn, ol_hgrn, ol_lru, ol_ret, gates_l, lp, h_lat.dtype)
    if not with_ctx_output:
        return y_lat, None
    y_ctx = merge_branches(oc_gdn, oc_hgrn, oc_lru, oc_ret, gates_c, lp, h_ctx.dtype)
    return y_lat, y_ctx


def moe_ffn(h, router_w, router_b, w1, b1, w2, b2):
    m, d = h.shape
    logits = (h @ router_w + router_b).astype(jnp.float32)
    top_logit, top_idx = lax.top_k(logits, TOP_K)
    top_w = jax.nn.softmax(top_logit, axis=-1)
    n_assign = m * TOP_K
    flat_e = top_idx.reshape(-1).astype(jnp.int32)
    order = jnp.argsort(flat_e).astype(jnp.int32)
    sorted_e = flat_e[order]
    counts = jnp.zeros((N_EXPERTS,), jnp.int32).at[flat_e].add(1)
    padded = (counts + EXPERT_BLOCK - 1) // EXPERT_BLOCK * EXPERT_BLOCK
    pad_end = jnp.cumsum(padded)
    pad_start = pad_end - padded
    start = jnp.cumsum(counts) - counts
    dest_sorted = pad_start[sorted_e] + jnp.arange(n_assign, dtype=jnp.int32) - start[sorted_e]
    n_blocks = -(-n_assign // EXPERT_BLOCK) + N_EXPERTS
    cap = n_blocks * EXPERT_BLOCK
    src_tok = jnp.full((cap,), m, jnp.int32).at[dest_sorted].set(order // TOP_K)
    block_expert = jnp.minimum(
        jnp.searchsorted(pad_end, jnp.arange(n_blocks, dtype=jnp.int32) * EXPERT_BLOCK, side="right"),
        N_EXPERTS - 1).astype(jnp.int32)
    h_pad = jnp.concatenate([h, jnp.zeros((1, d), h.dtype)], axis=0)[src_tok].reshape(n_blocks, EXPERT_BLOCK, d)

    def expert_block(args):
        hb, e = args
        z = hb @ w1[e] + b1[e]
        glu = jnp.minimum(z[:, :EXPERT_FF], SWIGLU_LIMIT)
        lin = jnp.clip(z[:, EXPERT_FF:], -SWIGLU_LIMIT, SWIGLU_LIMIT)
        act = glu * jax.nn.sigmoid(SWIGLU_ALPHA * glu) * (lin + 1.0)
        return act @ w2[e] + b2[e]

    y_pad = lax.map(expert_block, (h_pad, block_expert)).reshape(cap, d)
    dest = jnp.zeros((n_assign,), jnp.int32).at[order].set(dest_sorted)
    y = y_pad[dest].reshape(m, TOP_K, d)
    return jnp.einsum("mkd,mk->md", y, top_w.astype(y.dtype))


def setup_inputs(seed: int = 0) -> dict:
    key = jax.random.key(seed)
    ks = iter(jax.random.split(key, 48))
    nrm = lambda shape, scale: scale * jax.random.normal(next(ks), shape, jnp.float32)
    L, D, W = DEPTH, D_MODEL, MIX_W
    a_init = jax.random.uniform(next(ks), (L, 2, GDN_HEADS), minval=1.0, maxval=16.0)
    dt = jnp.exp(jax.random.uniform(next(ks), (L, 2, GDN_HEADS),
                                    minval=math.log(1e-3), maxval=math.log(1e-1)))
    dt_bias = dt + jnp.log(-jnp.expm1(-dt))
    a8 = jax.random.uniform(next(ks), (L, 2, W), minval=0.9, maxval=0.999)
    root = a8 ** (1.0 / LRU_C)
    lru_lambda = jnp.log(root) - jnp.log1p(-root)
    return {
        "x": nrm((BATCH, SEQ, D), 1.0),
        "c": nrm((BATCH, D), 1.0),
        "ctx": nrm((BATCH, CTX_LEN, D), 1.0),
        "c_ctx": nrm((D,), 1.0),
        "w_mod": nrm((L, D, N_MOD * D), D ** -0.5),
        "b_mod": nrm((L, N_MOD * D), 0.02),
        "norm1_g": 1.0 + nrm((L, D), 0.02),
        "norm2_g": 1.0 + nrm((L, D), 0.02),
        "w_in": nrm((L, D, N_IN), D ** -0.5),
        "gdn_conv_w": nrm((L, CONV_W, 3 * W), CONV_W ** -0.5),
        "gdn_a_log": jnp.log(a_init),
        "gdn_dt_bias": dt_bias,
        "gdn_norm_g": 1.0 + nrm((L, GDN_HEAD_DIM), 0.02),
        "hgrn_lb": nrm((L, 2, W), 0.1),
        "hgrn_norm_g": 1.0 + nrm((L, HGRN_HEAD_DIM), 0.02),
        "lru_conv_w": nrm((L, CONV_W, W), CONV_W ** -0.5),
        "lru_conv_b": nrm((L, W), 0.02),
        "lru_wa": nrm((L, 2, LRU_BLOCKS, LRU_BLOCK_W, LRU_BLOCK_W), LRU_BLOCK_W ** -0.5),
        "lru_ba": nrm((L, 2, W), 0.02),
        "lru_wx": nrm((L, 2, LRU_BLOCKS, LRU_BLOCK_W, LRU_BLOCK_W), LRU_BLOCK_W ** -0.5),
        "lru_bx": nrm((L, 2, W), 0.02),
        "lru_lambda": lru_lambda,
        "ret_norm_g": 1.0 + nrm((L, RET_HEAD_DIM), 0.02),
        "w_branch": nrm((L, N_BRANCH, W, D), W ** -0.5),
        "w_out": nrm((L, D, D), D ** -0.5),
        "router_w": nrm((L, D, N_EXPERTS), D ** -0.5),
        "router_b": nrm((L, N_EXPERTS), 0.01),
        "moe_w1": nrm((L, N_EXPERTS, D, 2 * EXPERT_FF), D ** -0.5),
        "moe_b1": nrm((L, N_EXPERTS, 2 * EXPERT_FF), 0.02),
        "moe_w2": nrm((L, N_EXPERTS, EXPERT_FF, D), EXPERT_FF ** -0.5),
        "moe_b2": nrm((L, N_EXPERTS, D), 0.02),
        "final_norm_g": 1.0 + nrm((D,), 0.02),
    }


def reference(x, c, ctx, c_ctx, w_mod, b_mod, norm1_g, norm2_g, w_in, gdn_conv_w, gdn_a_log,
              gdn_dt_bias, gdn_norm_g, hgrn_lb, hgrn_norm_g, lru_conv_w, lru_conv_b, lru_wa, lru_ba,
              lru_wx, lru_bx, lru_lambda, ret_norm_g, w_branch, w_out, router_w, router_b,
              moe_w1, moe_b1, moe_w2, moe_b2, final_norm_g):
    b, n_lat, d = x.shape
    n_ctx = ctx.shape[1]
    ROWS = n_lat // GRID_W
    rope = axial_rope_tables(ROWS)
    lb_soft = jax.nn.softmax(hgrn_lb.astype(jnp.float32), axis=0)
    lower_bounds = jnp.clip(jnp.cumsum(lb_soft, axis=0) - lb_soft[0], 0.0, LB_MAX)
    xl, xc = x, ctx
    for l in range(DEPTH):
        last = l == DEPTH - 1
        lp = {"w_in": w_in[l], "gdn_conv_w": gdn_conv_w[l], "gdn_a_log": gdn_a_log[l],
              "gdn_dt_bias": gdn_dt_bias[l], "gdn_norm_g": gdn_norm_g[l], "hgrn_norm_g": hgrn_norm_g[l],
              "lru_conv_w": lru_conv_w[l], "lru_conv_b": lru_conv_b[l], "lru_wa": lru_wa[l],
              "lru_ba": lru_ba[l], "lru_wx": lru_wx[l], "lru_bx": lru_bx[l], "lru_lambda": lru_lambda[l],
              "ret_norm_g": ret_norm_g[l], "w_branch": w_branch[l], "w_out": w_out[l]}
        sh1, sc1, gt1, sh2, sc2, gt2 = [m_[:, None, :] for m_ in jnp.split(
            jax.nn.silu(c) @ w_mod[l] + b_mod[l], N_MOD, axis=-1)]
        csh1, csc1, cgt1, csh2, csc2, cgt2 = jnp.split(
            jax.nn.silu(c_ctx) @ w_mod[l] + b_mod[l], N_MOD, axis=-1)
        h_lat = rms_norm(xl, norm1_g[l]) * (1.0 + sc1) + sh1
        h_ctx = rms_norm(xc, norm1_g[l]) * (1.0 + csc1) + csh1
        y_lat, y_ctx = token_mixer(h_lat, h_ctx, lp, lower_bounds[l], rope, not last)
        xl = xl + gt1 * y_lat
        f_lat = (rms_norm(xl, norm2_g[l]) * (1.0 + sc2) + sh2).reshape(-1, d)
        moe_args = (router_w[l], router_b[l], moe_w1[l], moe_b1[l], moe_w2[l], moe_b2[l])
        if last:
            xl = xl + gt2 * moe_ffn(f_lat, *moe_args).reshape(b, n_lat, d)
        else:
            xc = xc + cgt1 * y_ctx
            f_ctx = (rms_norm(xc, norm2_g[l]) * (1.0 + csc2) + csh2).reshape(-1, d)
            out = moe_ffn(jnp.concatenate([f_lat, f_ctx], axis=0), *moe_args)
            xl = xl + gt2 * out[: b * n_lat].reshape(b, n_lat, d)
            xc = xc + cgt2 * out[b * n_lat:].reshape(b, n_ctx, d)
    return rms_norm(xl, final_norm_g)
```

```python
import functools
import math

import numpy as np
import jax
import jax.numpy as jnp
from jax import lax
from jax.experimental import pallas as pl
from jax.experimental.pallas import tpu as pltpu

F32 = jnp.float32
BF16 = jnp.bfloat16
HIGHEST = lax.Precision.HIGHEST

GRID_W = 64
N_BRANCH = 4
HEAD_DIM = 128
LRU_BLOCKS = 8
LRU_C = 8.0
CONV_W = 4
CHUNK = 64
ROPE_BASE = 10000.0
TOP_K = 4
SWIGLU_ALPHA = 1.702
SWIGLU_LIMIT = 7.0
N_MOD = 6
EPS = 1e-6
LB_MIN = 1e-30
LB_MAX = 1.0 - 1e-4
SQRT_MIN = 1e-12

LANES = 128
SUBLANES = 8
VMEM_LIMIT_BYTES = 56 * 1024 * 1024
ROUTE_BLOCK = 256
NEG_BIG = -1e30


def _cparams(semantics):
    return pltpu.CompilerParams(dimension_semantics=semantics, vmem_limit_bytes=VMEM_LIMIT_BYTES)


def _largest_tile(n, cap, quantum=SUBLANES):
    best = None
    for t in range(quantum, min(n, cap) + 1, quantum):
        if n % t == 0:
            best = t
    assert best is not None, (n, cap)
    return best


def _dot(a, b, precision=None):
    return jnp.dot(a, b, preferred_element_type=F32, precision=precision)


def _dot_nt(a, b):
    return lax.dot_general(a, b, (((1,), (1,)), ((), ())), preferred_element_type=F32)


def _dot_tn(a, b):
    return lax.dot_general(a, b, (((0,), (0,)), ((), ())), preferred_element_type=F32)


def _bdot(a, b):
    return _dot(a.astype(BF16), b.astype(BF16))


def _bdot_nt(a, b):
    return _dot_nt(a.astype(BF16), b.astype(BF16))


def _bdot_tn(a, b):
    return _dot_tn(a.astype(BF16), b.astype(BF16))


def _sigmoid(x):
    return 1.0 / (1.0 + jnp.exp(-x))


def _silu(x):
    return x * _sigmoid(x)


def _softplus(x):
    return jnp.maximum(x, 0.0) + jnp.log1p(jnp.exp(-jnp.abs(x)))


def _log_sigmoid(x):
    return -_softplus(-x)


def _mod_kernel(c_ref, w_ref, b_ref, o_ref):
    o_ref[0] = _dot(_silu(c_ref[...]), w_ref[0], precision=HIGHEST) + b_ref[0]


def _modulation(cc, w_mod, b_mod):
    n_layers, d, n = w_mod.shape
    rows = cc.shape[0]
    tn = _largest_tile(n, 1024, LANES)
    return pl.pallas_call(
        _mod_kernel,
        grid=(n_layers, n // tn),
        in_specs=[pl.BlockSpec((rows, d), lambda l, j: (0, 0)),
                  pl.BlockSpec((1, d, tn), lambda l, j: (l, 0, j)),
                  pl.BlockSpec((1, 1, tn), lambda l, j: (l, 0, j))],
        out_specs=pl.BlockSpec((1, rows, tn), lambda l, j: (l, 0, j)),
        out_shape=jax.ShapeDtypeStruct((n_layers, rows, n), F32),
        compiler_params=_cparams(("parallel", "parallel")),
        name="modulation",
    )(cc, w_mod, b_mod.reshape(n_layers, 1, n))


D_MODEL = 1024
MIX_W = 512
N_HEADS = MIX_W // HEAD_DIM
COL_MERGE = 0
COL_GDN_Z = 4096
COL_GDN_QKV = 4608
COL_HG_Q = 6144
COL_HG_I = 6656
COL_HG_F = 7168
COL_HG_G = 8192
COL_LRU_X = 8704
COL_LRU_Y = 9216
COL_RET_Q = 9728
COL_RET_K = 10240
COL_RET_V = 10752
COL_RET_G = 11264
COL_AB = 11776
N_PROJ = 12288


class _Geo:
    def __init__(self, b, n_lat, n_ctx):
        self.b, self.n_lat, self.n_ctx = b, n_lat, n_ctx
        self.m_lat, self.m_ctx = b * n_lat, b * n_ctx
        self.m = self.m_lat + self.m_ctx
        assert n_lat % CHUNK == 0 and n_ctx % CHUNK == 0 and n_lat % GRID_W == 0

    def mod_row(self, i, tm):
        r0 = i * tm
        return jnp.where(r0 < self.m_lat, r0 // self.n_lat, self.b)

    def seq_edges(self, i, tr):
        r0 = i * tr
        is_lat = r0 < self.m_lat
        pos = jnp.where(is_lat, r0 % self.n_lat, (r0 - self.m_lat) % self.n_ctx)
        n = jnp.where(is_lat, self.n_lat, self.n_ctx)
        return pos == 0, pos + tr == n


def _reorder_w_in(w_in):
    w = w_in.astype(BF16)
    d = w.shape[0]
    pad = lambda n: jnp.zeros((d, n), BF16)
    ab = [jnp.concatenate([w[:, 2048 + 4 * dr:2052 + 4 * dr], w[:, 2056 + 4 * dr:2060 + 4 * dr],
                           pad(LANES - 8)], axis=1) for dr in range(2)]
    out = jnp.concatenate([w[:, 7696:11792], w[:, 1536:2048], w[:, 0:1536], w[:, 2064:7696],
                           ab[0], ab[1], pad(N_PROJ - COL_AB - 2 * LANES)], axis=1)
    assert out.shape[1] == N_PROJ
    return out


def _rms_mod(x, g, sc, sh):
    y = x * lax.rsqrt(jnp.mean(x * x, axis=-1, keepdims=True) + EPS) * g
    return y * (1.0 + sc) + sh


def _norm_proj_kernel(x_ref, g_ref, sc_ref, sh_ref, w_ref, o_ref, h_ref):
    @pl.when(pl.program_id(1) == 0)
    def _():
        h_ref[...] = _rms_mod(x_ref[...], g_ref[...], sc_ref[0], sh_ref[0]).astype(BF16)

    o_ref[...] = _dot(h_ref[...], w_ref[...])


def _norm_proj(geo, x_rows, g, mods3, w_p):
    m, d = x_rows.shape
    n = w_p.shape[1]
    tm = _largest_tile(math.gcd(geo.n_lat, geo.m_ctx), 1024)
    tn = 1024
    return pl.pallas_call(
        _norm_proj_kernel,
        grid=(m // tm, n // tn),
        in_specs=[pl.BlockSpec((tm, d), lambda i, j: (i, 0)),
                  pl.BlockSpec((1, d), lambda i, j: (0, 0)),
                  pl.BlockSpec((1, 1, d), lambda i, j: (geo.mod_row(i, tm), 0, 1)),
                  pl.BlockSpec((1, 1, d), lambda i, j: (geo.mod_row(i, tm), 0, 0)),
                  pl.BlockSpec((d, tn), lambda i, j: (0, j))],
        out_specs=pl.BlockSpec((tm, tn), lambda i, j: (i, j)),
        out_shape=jax.ShapeDtypeStruct((m, n), F32),
        scratch_shapes=[pltpu.VMEM((tm, d), BF16)],
        compiler_params=_cparams(("parallel", "arbitrary")),
        name="norm_proj",
    )(x_rows, g.reshape(1, d), mods3, mods3, w_p)


def _halo_specs(geo, tr, width, col_block):
    nb8 = geo.m // SUBLANES
    r8 = tr // SUBLANES
    return [pl.BlockSpec((SUBLANES, width), lambda i: (jnp.maximum(i * r8 - 1, 0), col_block)),
            pl.BlockSpec((tr, width), lambda i: (i, col_block)),
            pl.BlockSpec((SUBLANES, width), lambda i: (jnp.minimum((i + 1) * r8, nb8 - 1), col_block))]


def _token_conv(prev_ref, x_ref, next_ref, w_ref, first, last):
    x = x_ref[...]
    tr = x.shape[0]
    prev = jnp.where(first, 0.0, prev_ref[...])
    nxt = jnp.where(last, 0.0, next_ref[...])
    ext = jnp.concatenate([prev, x, nxt], axis=0)
    w = w_ref[...]
    y = None
    for j in range(CONV_W):
        off = SUBLANES - CONV_W // 2 + j
        term = ext[off:off + tr] * w[j:j + 1]
        y = term if y is None else y + term
    return y


def _gdn_prep_kernel(geo, tr, prev_ref, x_ref, next_ref, ab0_ref, ab1_ref, cw_ref, alog_ref, dtb_ref,
                     q_ref, k_ref, v_ref, gb_ref):
    first, last = geo.seq_edges(pl.program_id(0), tr)
    s = _silu(_token_conv(prev_ref, x_ref, next_ref, cw_ref, first, last))
    for h in range(N_HEADS):
        sl = slice(h * HEAD_DIM, (h + 1) * HEAD_DIM)
        qh = s[:, sl]
        kh = s[:, MIX_W + h * HEAD_DIM:MIX_W + (h + 1) * HEAD_DIM]
        q_ref[:, sl] = qh * (lax.rsqrt(jnp.sum(qh * qh, axis=-1, keepdims=True) + EPS) * HEAD_DIM ** -0.5)
        k_ref[:, sl] = kh * lax.rsqrt(jnp.sum(kh * kh, axis=-1, keepdims=True) + EPS)
    v_ref[...] = s[:, 2 * MIX_W:]
    lane = lax.broadcasted_iota(jnp.int32, (tr, LANES), 1)
    for dr, ab_ref in enumerate((ab0_ref, ab1_ref)):
        a = ab_ref[...]
        g = -jnp.exp(alog_ref[dr]) * _softplus(a + dtb_ref[dr])
        gb_ref[dr] = jnp.where(lane < N_HEADS, g, _sigmoid(a))


def _gdn_prep(geo, p, conv_w, a_log, dt_bias):
    tr = _largest_tile(math.gcd(geo.n_lat, geo.n_ctx), 256)
    m = geo.m
    lane_pad = lambda v: jnp.pad(v.astype(F32), ((0, 0), (0, LANES - N_HEADS))).reshape(2, 1, LANES)
    ab_blk = COL_AB // LANES
    rows = lambda width: pl.BlockSpec((tr, width), lambda i: (i, 0))
    full3 = pl.BlockSpec((2, 1, LANES), lambda i: (0, 0, 0))
    return pl.pallas_call(
        functools.partial(_gdn_prep_kernel, geo, tr),
        grid=(m // tr,),
        in_specs=_halo_specs(geo, tr, 3 * MIX_W, COL_GDN_QKV // (3 * MIX_W)) + [
            pl.BlockSpec((tr, LANES), lambda i: (i, ab_blk)),
            pl.BlockSpec((tr, LANES), lambda i: (i, ab_blk + 1)),
            pl.BlockSpec((CONV_W, 3 * MIX_W), lambda i: (0, 0)), full3, full3],
        out_specs=[rows(MIX_W), rows(MIX_W), rows(MIX_W),
                   pl.BlockSpec((2, tr, LANES), lambda i: (0, i, 0))],
        out_shape=[jax.ShapeDtypeStruct((m, MIX_W), F32)] * 3 + [jax.ShapeDtypeStruct((2, m, LANES), F32)],
        compiler_params=_cparams(("parallel",)),
        name="gdn_prep",
    )(p, p, p, p, p, conv_w.astype(F32), lane_pad(a_log), lane_pad(dt_bias))


def _chunk_block(geo, b, d, j):
    nc_ctx, nc_lat = geo.n_ctx // CHUNK, geo.n_lat // CHUNK
    jc = jnp.where(d == 0, j, nc_ctx - 1 - j)
    jl = jnp.where(d == 0, j - nc_ctx, nc_lat - 1 - (j - nc_ctx))
    return jnp.where(j < nc_ctx, (geo.m_lat + b * geo.n_ctx) // CHUNK + jc, (b * geo.n_lat) // CHUNK + jl)


def _scan_grid(geo):
    return (geo.b, 2, (geo.n_ctx + geo.n_lat) // CHUNK)


N_LEVELS = int(math.log2(CHUNK))


def _chunk_consts():
    t = np.arange(CHUNK)
    tt, uu = t[:, None], t[None, :]
    tri = uu <= tt
    pair = []
    for lv in range(N_LEVELS):
        hi = ((t >> lv) & 1) == 1
        pair.append(((uu >> (lv + 1)) == (tt >> (lv + 1))) & hi[:, None] & (~hi)[None, :])
    pair = np.stack(pair)
    both = lambda a: np.stack([a, a[..., ::-1, ::-1]]).astype(np.float32)
    return jnp.asarray(both(tri)), jnp.asarray(both(pair))


def _gdn_scan_kernel(q_ref, k_ref, v_ref, gb_ref, tri_ref, pair_ref, o_ref, s_ref):
    @pl.when(pl.program_id(2) == 0)
    def _():
        s_ref[...] = jnp.zeros_like(s_ref)

    tri = tri_ref[0]
    row = lax.broadcasted_iota(jnp.int32, (CHUNK, CHUNK), 0)
    col = lax.broadcasted_iota(jnp.int32, (CHUNK, CHUNK), 1)
    incl = tri > 0.5
    eye = (row == col).astype(F32)
    gb = gb_ref[0]
    gcum = _dot(tri, gb, precision=HIGHEST)
    gcum_t = gcum.T
    gtot = jnp.sum(gb, axis=0, keepdims=True)
    q, k, v = q_ref[...], k_ref[...], v_ref[...]
    for h in range(N_HEADS):
        sl = slice(h * HEAD_DIM, (h + 1) * HEAD_DIM)
        bc = gcum[:, h:h + 1]
        br = gcum_t[h:h + 1, :]
        beta = gb[:, N_HEADS + h:N_HEADS + h + 1]
        decay = jnp.where(incl, jnp.exp(jnp.where(incl, bc - br, 0.0)), 0.0)
        qh, kh, vh = q[:, sl], k[:, sl], v[:, sl]
        kbeta = kh * beta
        k16 = kh.astype(BF16)
        mm = _dot_nt(kbeta.astype(BF16), k16) * decay
        dinv = eye - pair_ref[0, 0] * mm
        for lv in range(1, N_LEVELS):
            dinv = dinv - _bdot(dinv, _bdot(pair_ref[0, lv] * mm, dinv))
        eb = jnp.exp(bc)
        rhs = jnp.concatenate([vh * beta, kbeta * eb], axis=-1)
        sol = rhs + _bdot(dinv - eye, rhs)
        u, w = sol[:, :HEAD_DIM], sol[:, HEAD_DIM:]
        s = s_ref[h]
        v_new = u - _bdot(w, s)
        attn = _dot_nt(qh.astype(BF16), k16) * decay
        o_ref[0, :, sl] = _bdot(qh * eb, s) + _bdot(attn, v_new)
        bl = gtot[:, h:h + 1]
        s_ref[h] = s * jnp.exp(bl) + _bdot_tn(kh * jnp.exp(bl - bc), v_new)


def _gdn_scan(geo, q, k, v, gb):
    blk = lambda width: pl.BlockSpec((CHUNK, width), lambda b, d, j: (_chunk_block(geo, b, d, j), 0))
    tri, pair = _chunk_consts()
    return pl.pallas_call(
        _gdn_scan_kernel,
        grid=_scan_grid(geo),
        in_specs=[blk(MIX_W), blk(MIX_W), blk(MIX_W),
                  pl.BlockSpec((1, CHUNK, LANES), lambda b, d, j: (d, _chunk_block(geo, b, d, j), 0)),
                  pl.BlockSpec((1, CHUNK, CHUNK), lambda b, d, j: (d, 0, 0)),
                  pl.BlockSpec((1, N_LEVELS, CHUNK, CHUNK), lambda b, d, j: (d, 0, 0, 0))],
        out_specs=pl.BlockSpec((1, CHUNK, MIX_W), lambda b, d, j: (d, _chunk_block(geo, b, d, j), 0)),
        out_shape=jax.ShapeDtypeStruct((2, geo.m, MIX_W), F32),
        scratch_shapes=[pltpu.VMEM((N_HEADS, HEAD_DIM, HEAD_DIM), F32)],
        compiler_params=_cparams(("parallel", "parallel", "arbitrary")),
        name="gdn_scan",
    )(q, k, v, gb, tri, pair)


def _hgrn_consts():
    t = np.arange(CHUNK)
    tt, uu = t[:, None], t[None, :]
    a_all, rowhi, pair = [(uu <= tt)], [], [(uu == tt)]
    for lv in range(N_LEVELS):
        hi = ((t >> lv) & 1) == 1
        same_half = (uu >> lv) == (tt >> lv)
        a_all.append(same_half & np.where(hi[:, None], uu <= tt, uu > tt))
        rowhi.append(np.broadcast_to(hi[:, None], (CHUNK, HEAD_DIM)))
        pair.append(((uu >> (lv + 1)) == (tt >> (lv + 1))) & hi[:, None] & (~hi)[None, :])
    mirror = lambda a: a[::-1, ::-1] if a.shape[1] == CHUNK else a[::-1]
    stack = lambda xs: np.stack([np.stack(xs), np.stack([mirror(x) for x in xs])]).astype(np.float32)
    a = stack(a_all)
    return (jnp.asarray(a.reshape(2, (N_LEVELS + 1) * CHUNK, CHUNK)), jnp.asarray(stack(rowhi)),
            jnp.asarray(stack(pair)))


def _hgrn_scan_kernel(q_ref, i_ref, z_ref, lb_ref, a_ref, rowhi_ref, pair_ref, o_ref, st_ref):
    @pl.when(pl.program_id(2) == 0)
    def _():
        st_ref[...] = jnp.zeros_like(st_ref)

    lb = lb_ref[0]
    z = z_ref[...]
    log_lb = jnp.log(jnp.maximum(lb, LB_MIN))
    other = jnp.log1p(-lb) + _log_sigmoid(z)
    log_f = jnp.maximum(log_lb, other) + jnp.log1p(jnp.exp(-jnp.abs(log_lb - other)))
    key = (1.0 - lb) * _sigmoid(-z)
    q = _silu(q_ref[...]) * HEAD_DIM ** -0.5
    v = i_ref[...]
    e_all = _dot(a_ref[0], log_f, precision=HIGHEST)
    bfull = e_all[:CHUNK]
    ltot = jnp.sum(log_f, axis=0, keepdims=True)
    for h in range(N_HEADS):
        sl = slice(h * HEAD_DIM, (h + 1) * HEAD_DIM)
        qh, kh, vh = q[:, sl], key[:, sl], v[:, sl]
        scores = pair_ref[0, 0] * _bdot_nt(qh, kh)
        for lv in range(N_LEVELS):
            x = jnp.exp(e_all[(lv + 1) * CHUNK:(lv + 2) * CHUNK, sl])
            hi = rowhi_ref[0, lv]
            scores = scores + pair_ref[0, lv + 1] * _bdot_nt(qh * (x * hi), kh * (x * (1.0 - hi)))
        st = st_ref[h]
        bh = bfull[:, sl]
        o_ref[0, :, sl] = _bdot(scores, vh) + _bdot_nt(qh * jnp.exp(bh), st)
        bl = ltot[:, sl]
        st_ref[h] = st * jnp.exp(bl) + _bdot_tn(vh, kh * jnp.exp(bl - bh))


def _hgrn_scan(geo, p, lb):
    a_all, rowhi, pair = _hgrn_consts()
    col = lambda c: pl.BlockSpec((CHUNK, MIX_W), lambda b, d, j: (_chunk_block(geo, b, d, j), c // MIX_W))
    per_dir = lambda shape: pl.BlockSpec((1,) + shape, lambda b, d, j: (d,) + (0,) * len(shape))
    return pl.pallas_call(
        _hgrn_scan_kernel,
        grid=_scan_grid(geo),
        in_specs=[col(COL_HG_Q), col(COL_HG_I),
                  pl.BlockSpec((CHUNK, MIX_W),
                               lambda b, d, j: (_chunk_block(geo, b, d, j), COL_HG_F // MIX_W + d)),
                  per_dir((1, MIX_W)), per_dir(((N_LEVELS + 1) * CHUNK, CHUNK)),
                  per_dir((N_LEVELS, CHUNK, HEAD_DIM)), per_dir((N_LEVELS + 1, CHUNK, CHUNK))],
        out_specs=pl.BlockSpec((1, CHUNK, MIX_W), lambda b, d, j: (d, _chunk_block(geo, b, d, j), 0)),
        out_shape=jax.ShapeDtypeStruct((2, geo.m, MIX_W), F32),
        scratch_shapes=[pltpu.VMEM((N_HEADS, HEAD_DIM, HEAD_DIM), F32)],
        compiler_params=_cparams(("parallel", "parallel", "arbitrary")),
        name="hgrn_scan",
    )(p, p, p, lb.reshape(2, 1, MIX_W), a_all, rowhi, pair)


def _ret_consts(n_lat):
    t = np.arange(CHUNK, dtype=np.float64)
    log_gamma = np.log1p(-np.exp2(-5.0 - np.arange(N_HEADS, dtype=np.float64)))
    dmat, qs, ks = [], [], []
    for rev in (False, True):
        te = t[::-1] if rev else t
        diff = te[:, None] - te[None, :]
        dmat.append(np.stack([np.where(diff >= 0, np.exp(lg * np.maximum(diff, 0)), 0.0) for lg in log_gamma]))
        qs.append(np.repeat(np.exp(log_gamma[None, :] * (te[:, None] + 1.0)), HEAD_DIM, axis=1))
        ks.append(np.repeat(np.exp(log_gamma[None, :] * (CHUNK - 1.0 - te[:, None])), HEAD_DIM, axis=1))
    g_chunk = np.repeat(np.exp(log_gamma * CHUNK), HEAD_DIM)[None, :]
    row = np.repeat(np.arange(n_lat // GRID_W, dtype=np.float32), GRID_W)
    colp = np.tile(np.arange(GRID_W, dtype=np.float32), n_lat // GRID_W)
    n_freq = HEAD_DIM // 4
    inv_freq = (np.float32(ROPE_BASE) ** (-np.arange(n_freq, dtype=np.float32) / n_freq)).astype(np.float32)
    ang = np.concatenate([row[:, None] * inv_freq, colp[:, None] * inv_freq], axis=-1).astype(np.float32)
    cos, sin = np.cos(ang), np.sin(ang)
    f32 = lambda a: jnp.asarray(np.asarray(a, np.float32))
    return (f32(np.stack(dmat)), f32(np.stack(qs)), f32(np.stack(ks)), f32(g_chunk),
            f32(np.concatenate([cos, cos], axis=-1)), f32(np.concatenate([-sin, sin], axis=-1)))


def _ret_scan_kernel(nc_ctx, q_ref, k_ref, v_ref, cos_ref, sin_ref, dm_ref, qs_ref, ks_ref, gc_ref,
                     o_ref, s_ref):
    j = pl.program_id(2)

    @pl.when(j == 0)
    def _():
        s_ref[...] = jnp.zeros_like(s_ref)

    is_lat = j >= nc_ctx
    cos = jnp.where(is_lat, cos_ref[...], 1.0)
    sin = jnp.where(is_lat, sin_ref[...], 0.0)
    q, k, v = q_ref[...], k_ref[...], v_ref[...]
    qs, ks, gc = qs_ref[0], ks_ref[0], gc_ref[...]
    for h in range(N_HEADS):
        sl = slice(h * HEAD_DIM, (h + 1) * HEAD_DIM)
        qh, kh, vh = q[:, sl], k[:, sl], v[:, sl]
        rq = qh * cos + pltpu.roll(qh, HEAD_DIM // 2, 1) * sin
        rk = (kh * cos + pltpu.roll(kh, HEAD_DIM // 2, 1) * sin) * HEAD_DIM ** -0.5
        scores = _bdot_nt(rq, rk) * dm_ref[0, h]
        s = s_ref[h]
        o_ref[0, :, sl] = _bdot(scores, vh) + _bdot(rq * qs[:, sl], s)
        s_ref[h] = s * gc[:, sl] + _bdot_tn(rk * ks[:, sl], vh)


def _ret_scan(geo, p):
    dmat, qs, ks, gc, cos2, sin2 = _ret_consts(geo.n_lat)
    nc_ctx, nc_lat = geo.n_ctx // CHUNK, geo.n_lat // CHUNK
    col = lambda c: pl.BlockSpec((CHUNK, MIX_W), lambda b, d, j: (_chunk_block(geo, b, d, j), c // MIX_W))

    def rope_blk(b, d, j):
        jl = jnp.where(d == 0, j - nc_ctx, nc_lat - 1 - (j - nc_ctx))
        return (jnp.where(j < nc_ctx, 0, jl), 0)

    per_dir = lambda shape: pl.BlockSpec((1,) + shape, lambda b, d, j: (d,) + (0,) * len(shape))
    return pl.pallas_call(
        functools.partial(_ret_scan_kernel, nc_ctx),
        grid=_scan_grid(geo),
        in_specs=[col(COL_RET_Q), col(COL_RET_K), col(COL_RET_V),
                  pl.BlockSpec((CHUNK, HEAD_DIM), rope_blk), pl.BlockSpec((CHUNK, HEAD_DIM), rope_blk),
                  per_dir((N_HEADS, CHUNK, CHUNK)), per_dir((CHUNK, MIX_W)), per_dir((CHUNK, MIX_W)),
                  pl.BlockSpec((1, MIX_W), lambda b, d, j: (0, 0))],
        out_specs=pl.BlockSpec((1, CHUNK, MIX_W), lambda b, d, j: (d, _chunk_block(geo, b, d, j), 0)),
        out_shape=jax.ShapeDtypeStruct((2, geo.m, MIX_W), F32),
        scratch_shapes=[pltpu.VMEM((N_HEADS, HEAD_DIM, HEAD_DIM), F32)],
        compiler_params=_cparams(("parallel", "parallel", "arbitrary")),
        name="ret_scan",
    )(p, p, p, cos2, sin2, dmat, qs, ks, gc)


def _lru_kernel(geo, tr, rev, prev_ref, x_ref, next_ref, cw_ref, cb_ref, wg_ref, bg_ref, lam_ref,
                o_ref, h_ref):
    b, j = pl.program_id(0), pl.program_id(1)

    @pl.when(j == 0)
    def _():
        h_ref[...] = jnp.zeros_like(h_ref)

    first, last = geo.seq_edges(_lru_block(geo, tr, rev, b, j), tr)
    xb = _token_conv(prev_ref, x_ref, next_ref, cw_ref, first, last) + cb_ref[...]
    gates = _bdot(xb, wg_ref[...]) + bg_ref[...]
    gate_r = _sigmoid(gates[:, :MIX_W])
    gate_i = _sigmoid(gates[:, MIX_W:])
    log_a = LRU_C * gate_r * _log_sigmoid(lam_ref[...])
    a = jnp.exp(log_a)
    u = xb * gate_i * jnp.sqrt(jnp.maximum(-jnp.tanh(log_a) * (a * a + 1.0), SQRT_MIN))
    row = lax.broadcasted_iota(jnp.int32, (tr, MIX_W), 0)
    step = 1
    while step < tr:
        shift = tr - step if rev else step
        valid = (row < tr - step) if rev else (row >= step)
        u = jnp.where(valid, a * pltpu.roll(u, shift, 0) + u, u)
        a = jnp.where(valid, a * pltpu.roll(a, shift, 0), a)
        step *= 2
    h = a * h_ref[...] + u
    o_ref[...] = h
    h_ref[...] = h[0:1] if rev else h[tr - 1:tr]


def _lru_block(geo, tr, rev, b, j):
    nb_ctx, nb_lat = geo.n_ctx // tr, geo.n_lat // tr
    jc = nb_ctx - 1 - j if rev else j
    jl = nb_lat - 1 - (j - nb_ctx) if rev else j - nb_ctx
    return jnp.where(j < nb_ctx, (geo.m_lat + b * geo.n_ctx) // tr + jc, (b * geo.n_lat) // tr + jl)


def _lru_scan(geo, p, conv_w, conv_b, wa, ba, wx, bx, lam, rev):
    tr = _largest_tile(math.gcd(geo.n_lat, geo.n_ctx), 256)
    d = 1 if rev else 0
    blockdiag = lambda w: jax.scipy.linalg.block_diag(*[w[n] for n in range(LRU_BLOCKS)])
    wg = jnp.concatenate([blockdiag(wa[d]), blockdiag(wx[d])], axis=1).astype(BF16)
    bg = jnp.concatenate([ba[d], bx[d]]).astype(F32).reshape(1, 2 * MIX_W)
    nb8, r8 = geo.m // SUBLANES, tr // SUBLANES
    cblk = COL_LRU_X // MIX_W
    blk = lambda b, j: _lru_block(geo, tr, rev, b, j)
    full = lambda shape: pl.BlockSpec(shape, lambda b, j: (0,) * len(shape))
    return pl.pallas_call(
        functools.partial(_lru_kernel, geo, tr, rev),
        grid=(geo.b, (geo.n_ctx + geo.n_lat) // tr),
        in_specs=[pl.BlockSpec((SUBLANES, MIX_W), lambda b, j: (jnp.maximum(blk(b, j) * r8 - 1, 0), cblk)),
                  pl.BlockSpec((tr, MIX_W), lambda b, j: (blk(b, j), cblk)),
                  pl.BlockSpec((SUBLANES, MIX_W),
                               lambda b, j: (jnp.minimum((blk(b, j) + 1) * r8, nb8 - 1), cblk)),
                  full((CONV_W, MIX_W)), full((1, MIX_W)), full((MIX_W, 2 * MIX_W)),
                  full((1, 2 * MIX_W)), full((1, MIX_W))],
        out_specs=pl.BlockSpec((tr, MIX_W), lambda b, j: (blk(b, j), 0)),
        out_shape=jax.ShapeDtypeStruct((geo.m, MIX_W), F32),
        scratch_shapes=[pltpu.VMEM((1, MIX_W), F32)],
        compiler_params=_cparams(("parallel", "arbitrary")),
        name="lru_scan_bwd" if rev else "lru_scan_fwd",
    )(p, p, p, conv_w.astype(F32), conv_b.astype(F32).reshape(1, MIX_W), wg, bg,
      lam[d].astype(F32).reshape(1, MIX_W))


def _gelu_tanh(x):
    return 0.5 * x * (1.0 + jnp.tanh(math.sqrt(2.0 / math.pi) * (x + 0.044715 * (x * x * x))))


def _gated_head_norm(o, g, z):
    parts = []
    for h in range(N_HEADS):
        oh = o[:, h * HEAD_DIM:(h + 1) * HEAD_DIM]
        parts.append(oh * lax.rsqrt(jnp.mean(oh * oh, axis=-1, keepdims=True) + EPS) * g)
    return jnp.concatenate(parts, axis=-1) * _silu(z)


def _merge_kernel(og_ref, oh_ref, olf_ref, olb_ref, or_ref, gz_ref, hg_ref, ly_ref, rg_ref, mg_ref,
                  x_ref, gt_ref, gng_ref, hng_ref, rng_ref, wb_ref, wo_ref, out_ref):
    branches = (
        _gated_head_norm(og_ref[0] + og_ref[1], gng_ref[...], gz_ref[...]),
        _gated_head_norm(oh_ref[0] + oh_ref[1], hng_ref[...], hg_ref[...]),
        (olf_ref[...] + olb_ref[...]) * _gelu_tanh(ly_ref[...]),
        _gated_head_norm(or_ref[0] + or_ref[1], rng_ref[...], rg_ref[...]),
    )
    mix = None
    for n, y in enumerate(branches):
        term = _sigmoid(mg_ref[:, n * D_MODEL:(n + 1) * D_MODEL]) * _bdot(y, wb_ref[n])
        mix = term if mix is None else mix + term
    out_ref[...] = x_ref[...] + gt_ref[0] * _bdot(mix, wo_ref[...])


def _merge(geo, n_rows, x_rows, p, o_gdn, o_hgrn, o_lru_f, o_lru_b, o_ret, mods3, gdn_g, hgrn_g, ret_g,
           w_branch, w_out):
    tm = _largest_tile(math.gcd(geo.n_lat, geo.n_ctx), 256)
    both = pl.BlockSpec((2, tm, MIX_W), lambda i: (0, i, 0))
    rows = lambda width: pl.BlockSpec((tm, width), lambda i: (i, 0))
    col = lambda c, width: pl.BlockSpec((tm, width), lambda i: (i, c // width))
    full = lambda shape: pl.BlockSpec(shape, lambda i: (0,) * len(shape))
    hd = lambda g: g.astype(F32).reshape(1, HEAD_DIM)
    return pl.pallas_call(
        _merge_kernel,
        grid=(n_rows // tm,),
        in_specs=[both, both, rows(MIX_W), rows(MIX_W), both,
                  col(COL_GDN_Z, MIX_W), col(COL_HG_G, MIX_W), col(COL_LRU_Y, MIX_W), col(COL_RET_G, MIX_W),
                  col(COL_MERGE, N_BRANCH * D_MODEL), rows(D_MODEL),
                  pl.BlockSpec((1, 1, D_MODEL), lambda i: (geo.mod_row(i, tm), 0, 2)),
                  full((1, HEAD_DIM)), full((1, HEAD_DIM)), full((1, HEAD_DIM)),
                  full((N_BRANCH, MIX_W, D_MODEL)), full((D_MODEL, D_MODEL))],
        out_specs=rows(D_MODEL),
        out_shape=jax.ShapeDtypeStruct((n_rows, D_MODEL), F32),
        compiler_params=_cparams(("parallel",)),
        name="merge",
    )(o_gdn, o_hgrn, o_lru_f, o_lru_b, o_ret, p, p, p, p, p, x_rows, mods3, hd(gdn_g), hd(hgrn_g), hd(ret_g),
      w_branch.astype(BF16), w_out.astype(BF16))


def _router_kernel(n_experts, x_ref, g_ref, sc_ref, sh_ref, rw_ref, rb_ref, f_ref, idx_ref, wt_ref):
    f = _rms_mod(x_ref[...], g_ref[...], sc_ref[0], sh_ref[0])
    f_ref[...] = f
    logits = _dot(f, rw_ref[...], precision=HIGHEST) + rb_ref[...]
    lane = lax.broadcasted_iota(jnp.int32, logits.shape, 1)
    logits = jnp.where(lane < n_experts, logits, NEG_BIG)
    vals, idxs = [], []
    for _ in range(TOP_K):
        mx = jnp.max(logits, axis=-1, keepdims=True)
        ix = jnp.min(jnp.where(logits == mx, lane, LANES), axis=-1, keepdims=True)
        vals.append(mx)
        idxs.append(ix)
        logits = jnp.where(lane == ix, -jnp.inf, logits)
    ex = [jnp.exp(vl - vals[0]) for vl in vals]
    tot = ex[0] + ex[1] + ex[2] + ex[3]
    idx_out = jnp.zeros(lane.shape, jnp.int32)
    wt_out = jnp.zeros(lane.shape, F32)
    for kk in range(TOP_K):
        idx_out = jnp.where(lane == kk, idxs[kk], idx_out)
        wt_out = jnp.where(lane == kk, ex[kk] / tot, wt_out)
    idx_ref[...] = idx_out
    wt_ref[...] = wt_out


def _router(geo, x_rows, g, mods3, router_w, router_b):
    n_rows, d = x_rows.shape
    n_experts = router_w.shape[1]
    tm = _largest_tile(math.gcd(geo.n_lat, geo.n_ctx), 256)
    rw = jnp.pad(router_w.astype(F32), ((0, 0), (0, LANES - n_experts)))
    rb = jnp.pad(router_b.astype(F32), (0, LANES - n_experts)).reshape(1, LANES)
    rows = lambda width: pl.BlockSpec((tm, width), lambda i: (i, 0))
    full = lambda shape: pl.BlockSpec(shape, lambda i: (0,) * len(shape))
    return pl.pallas_call(
        functools.partial(_router_kernel, n_experts),
        grid=(n_rows // tm,),
        in_specs=[rows(d), full((1, d)),
                  pl.BlockSpec((1, 1, d), lambda i: (geo.mod_row(i, tm), 0, 4)),
                  pl.BlockSpec((1, 1, d), lambda i: (geo.mod_row(i, tm), 0, 3)),
                  full((d, LANES)), full((1, LANES))],
        out_specs=[rows(d), rows(LANES), rows(LANES)],
        out_shape=[jax.ShapeDtypeStruct((n_rows, d), F32), jax.ShapeDtypeStruct((n_rows, LANES), jnp.int32),
                   jax.ShapeDtypeStruct((n_rows, LANES), F32)],
        compiler_params=_cparams(("parallel",)),
        name="router",
    )(x_rows, g.astype(F32).reshape(1, d), mods3, mods3, rw, rb)


def _route_plan(top_idx, n_experts):
    m = top_idx.shape[0]
    n_assign = m * TOP_K
    flat_e = top_idx.reshape(-1)
    order = jnp.argsort(flat_e).astype(jnp.int32)
    sorted_e = flat_e[order]
    counts = jnp.zeros((n_experts,), jnp.int32).at[flat_e].add(1)
    padded = (counts + ROUTE_BLOCK - 1) // ROUTE_BLOCK * ROUTE_BLOCK
    pad_end = jnp.cumsum(padded)
    pad_start = pad_end - padded
    start = jnp.cumsum(counts) - counts
    dest_sorted = pad_start[sorted_e] + jnp.arange(n_assign, dtype=jnp.int32) - start[sorted_e]
    n_blocks = -(-n_assign // ROUTE_BLOCK) + n_experts
    cap = n_blocks * ROUTE_BLOCK
    src_tok = jnp.zeros((cap,), jnp.int32).at[dest_sorted].set(order // TOP_K)
    block_expert = jnp.minimum(
        jnp.searchsorted(pad_end, jnp.arange(n_blocks, dtype=jnp.int32) * ROUTE_BLOCK, side="right"),
        n_experts - 1).astype(jnp.int32)
    dest = jnp.zeros((n_assign,), jnp.int32).at[order].set(dest_sorted)
    return src_tok.reshape(n_blocks, 1, ROUTE_BLOCK), block_expert, dest


def _gather_rows(idx_ref, n, src_hbm, dst, sem, start):
    def body(r, carry):
        cp = pltpu.make_async_copy(src_hbm.at[pl.ds(idx_ref[0, 0, r], 1)], dst(r), sem)
        if start:
            cp.start()
        else:
            cp.wait()
        return carry

    lax.fori_loop(0, n, body, 0)


def _expert_kernel(be_ref, src_ref, nxt_ref, f_hbm, w1_ref, b1_ref, w2_ref, b2_ref, y_ref, buf, sem):
    del be_ref
    i, nb = pl.program_id(0), pl.num_programs(0)
    slot = i % 2
    ff = w2_ref.shape[1]

    @pl.when(i == 0)
    def _():
        _gather_rows(src_ref, ROUTE_BLOCK, f_hbm, lambda r: buf.at[0, pl.ds(r, 1)], sem.at[0], True)

    @pl.when(i + 1 < nb)
    def _():
        _gather_rows(nxt_ref, ROUTE_BLOCK, f_hbm, lambda r: buf.at[1 - slot, pl.ds(r, 1)], sem.at[1 - slot], True)

    _gather_rows(src_ref, ROUTE_BLOCK, f_hbm, lambda r: buf.at[slot, pl.ds(r, 1)], sem.at[slot], False)
    z = _bdot(buf[slot], w1_ref[0]) + b1_ref[0]
    glu = jnp.minimum(z[:, :ff], SWIGLU_LIMIT)
    lin = jnp.clip(z[:, ff:], -SWIGLU_LIMIT, SWIGLU_LIMIT)
    act = glu * _sigmoid(SWIGLU_ALPHA * glu) * (lin + 1.0)
    y_ref[...] = _bdot(act, w2_ref[0]) + b2_ref[0]


def _experts(f, src_tok, block_expert, w1, b1, w2, b2):
    n_blocks = src_tok.shape[0]
    d = f.shape[1]
    n_experts, _, ff2 = w1.shape
    ff = w2.shape[1]
    grid_spec = pltpu.PrefetchScalarGridSpec(
        num_scalar_prefetch=1,
        grid=(n_blocks,),
        in_specs=[pl.BlockSpec((1, 1, ROUTE_BLOCK), lambda i, be: (i, 0, 0), memory_space=pltpu.SMEM),
                  pl.BlockSpec((1, 1, ROUTE_BLOCK), lambda i, be: (jnp.minimum(i + 1, n_blocks - 1), 0, 0),
                               memory_space=pltpu.SMEM),
                  pl.BlockSpec(memory_space=pl.ANY),
                  pl.BlockSpec((1, d, ff2), lambda i, be: (be[i], 0, 0)),
                  pl.BlockSpec((1, 1, ff2), lambda i, be: (be[i], 0, 0)),
                  pl.BlockSpec((1, ff, d), lambda i, be: (be[i], 0, 0)),
                  pl.BlockSpec((1, 1, d), lambda i, be: (be[i], 0, 0))],
        out_specs=pl.BlockSpec((ROUTE_BLOCK, d), lambda i, be: (i, 0)),
        scratch_shapes=[pltpu.VMEM((2, ROUTE_BLOCK, d), F32), pltpu.SemaphoreType.DMA((2,))],
    )
    return pl.pallas_call(
        _expert_kernel,
        grid_spec=grid_spec,
        out_shape=jax.ShapeDtypeStruct((n_blocks * ROUTE_BLOCK, d), F32),
        compiler_params=_cparams(("arbitrary",)),
        name="experts",
    )(block_expert, src_tok, src_tok, f, w1.astype(BF16), b1.astype(F32).reshape(n_experts, 1, ff2),
      w2.astype(BF16), b2.astype(F32).reshape(n_experts, 1, d))


def _combine_kernel(tc, final, dst_ref, nxt_ref, y_hbm, x_ref, wt_ref, gt_ref, fg_ref, out_ref, buf, sem):
    i, nb = pl.program_id(0), pl.num_programs(0)
    slot = i % 2
    n = TOP_K * tc

    @pl.when(i == 0)
    def _():
        _gather_rows(dst_ref, n, y_hbm, lambda r: buf.at[0, pl.ds(r, 1)], sem.at[0], True)

    @pl.when(i + 1 < nb)
    def _():
        _gather_rows(nxt_ref, n, y_hbm, lambda r: buf.at[1 - slot, pl.ds(r, 1)], sem.at[1 - slot], True)

    _gather_rows(dst_ref, n, y_hbm, lambda r: buf.at[slot, pl.ds(r, 1)], sem.at[slot], False)
    wt = wt_ref[...]
    acc = None
    for kk in range(TOP_K):
        term = wt[:, kk:kk + 1] * buf[slot, kk * tc:(kk + 1) * tc]
        acc = term if acc is None else acc + term
    out = x_ref[...] + gt_ref[0] * acc
    if final:
        out = out * lax.rsqrt(jnp.mean(out * out, axis=-1, keepdims=True) + EPS) * fg_ref[...]
    out_ref[...] = out


def _combine(geo, y_pad, dest, x_rows, wt, mods3, final_g, final):
    n_rows, d = x_rows.shape
    tc = _largest_tile(math.gcd(geo.n_lat, geo.n_ctx), 128)
    nt = n_rows // tc
    dst = dest.reshape(nt, tc, TOP_K).transpose(0, 2, 1).reshape(nt, 1, TOP_K * tc)
    rows = lambda width: pl.BlockSpec((tc, width), lambda i: (i, 0))
    return pl.pallas_call(
        functools.partial(_combine_kernel, tc, final),
        grid=(nt,),
        in_specs=[pl.BlockSpec((1, 1, TOP_K * tc), lambda i: (i, 0, 0), memory_space=pltpu.SMEM),
                  pl.BlockSpec((1, 1, TOP_K * tc), lambda i: (jnp.minimum(i + 1, nt - 1), 0, 0),
                               memory_space=pltpu.SMEM),
                  pl.BlockSpec(memory_space=pl.ANY), rows(d), rows(LANES),
                  pl.BlockSpec((1, 1, d), lambda i: (geo.mod_row(i, tc), 0, 5)),
                  pl.BlockSpec((1, d), lambda i: (0, 0))],
        out_specs=rows(d),
        out_shape=jax.ShapeDtypeStruct((n_rows, d), F32),
        scratch_shapes=[pltpu.VMEM((2, TOP_K * tc, d), F32), pltpu.SemaphoreType.DMA((2,))],
        compiler_params=_cparams(("arbitrary",)),
        name="moe_combine",
    )(dst, dst, y_pad, x_rows, wt, mods3, final_g.astype(F32).reshape(1, d))


def _moe(geo, x_rows, g, mods3, router_w, router_b, w1, b1, w2, b2, final_g, final):
    f, top_idx, top_w = _router(geo, x_rows, g, mods3, router_w, router_b)
    src_tok, block_expert, dest = _route_plan(top_idx[:, :TOP_K], router_w.shape[1])
    y_pad = _experts(f, src_tok, block_expert, w1, b1, w2, b2)
    return _combine(geo, y_pad, dest, x_rows, top_w, mods3, final_g, final)


def kernel(x, c, ctx, c_ctx, w_mod, b_mod, norm1_g, norm2_g, w_in, gdn_conv_w, gdn_a_log, gdn_dt_bias, gdn_norm_g, hgrn_lb, hgrn_norm_g, lru_conv_w, lru_conv_b, lru_wa, lru_ba, lru_wx, lru_bx, lru_lambda, ret_norm_g, w_branch, w_out, router_w, router_b, moe_w1, moe_b1, moe_w2, moe_b2, final_norm_g):
    b, n_lat, d = x.shape
    n_ctx = ctx.shape[1]
    depth = w_mod.shape[0]
    assert d == D_MODEL and w_branch.shape[2] == MIX_W
    geo = _Geo(b, n_lat, n_ctx)
    lb_soft = jax.nn.softmax(hgrn_lb.astype(F32), axis=0)
    lower_bounds = jnp.clip(jnp.cumsum(lb_soft, axis=0) - lb_soft[0], 0.0, LB_MAX)
    n_cond = -(-(b + 1) // SUBLANES) * SUBLANES
    cc = jnp.concatenate([c, c_ctx[None, :], jnp.zeros((n_cond - b - 1, d), c.dtype)], axis=0)
    mods = _modulation(cc.astype(F32), w_mod, b_mod)
    rows = jnp.concatenate([x.reshape(b * n_lat, d), ctx.reshape(b * n_ctx, d)], axis=0)
    for l in range(depth):
        last = l == depth - 1
        mods3 = mods[l].reshape(n_cond, 1, N_MOD * d)
        p = _norm_proj(geo, rows, norm1_g[l], mods3, _reorder_w_in(w_in[l]))
        gq, gk, gv, gb = _gdn_prep(geo, p, gdn_conv_w[l], gdn_a_log[l], gdn_dt_bias[l])
        o_gdn = _gdn_scan(geo, gq, gk, gv, gb)
        o_hgrn = _hgrn_scan(geo, p, lower_bounds[l])
        lru_args = (geo, p, lru_conv_w[l], lru_conv_b[l], lru_wa[l], lru_ba[l], lru_wx[l], lru_bx[l],
                    lru_lambda[l])
        o_lru_f = _lru_scan(*lru_args, rev=False)
        o_lru_b = _lru_scan(*lru_args, rev=True)
        o_ret = _ret_scan(geo, p)
        n_rows = geo.m_lat if last else geo.m
        rows = _merge(geo, n_rows, rows, p, o_gdn, o_hgrn, o_lru_f, o_lru_b, o_ret, mods3, gdn_norm_g[l],
                      hgrn_norm_g[l], ret_norm_g[l], w_branch[l], w_out[l])
        rows = _moe(geo, rows, norm2_g[l], mods3, router_w[l], router_b[l], moe_w1[l], moe_b1[l],
                    moe_w2[l], moe_b2[l], final_norm_g, last)
    return rows.reshape(b, n_lat, d)
```

```python
import functools
import math

import numpy as np
import jax
import jax.numpy as jnp
from jax import lax
from jax.experimental import pallas as pl
from jax.experimental.pallas import tpu as pltpu

F32 = jnp.float32
BF16 = jnp.bfloat16
HIGHEST = lax.Precision.HIGHEST

GRID_W = 64
N_BRANCH = 4
HEAD_DIM = 128
LRU_BLOCKS = 8
LRU_C = 8.0
CONV_W = 4
CHUNK = 64
ROPE_BASE = 10000.0
TOP_K = 4
SWIGLU_ALPHA = 1.702
SWIGLU_LIMIT = 7.0
N_MOD = 6
EPS = 1e-6
LB_MIN = 1e-30
LB_MAX = 1.0 - 1e-4
SQRT_MIN = 1e-12

LANES = 128
SUBLANES = 8
VMEM_LIMIT_BYTES = 56 * 1024 * 1024
ROUTE_BLOCK = 256
NEG_BIG = -1e30


def _cparams(semantics):
    return pltpu.CompilerParams(dimension_semantics=semantics, vmem_limit_bytes=VMEM_LIMIT_BYTES)


def _largest_tile(n, cap, quantum=SUBLANES):
    best = None
    for t in range(quantum, min(n, cap) + 1, quantum):
        if n % t == 0:
            best = t
    assert best is not None, (n, cap)
    return best


def _dot(a, b, precision=None):
    return jnp.dot(a, b, preferred_element_type=F32, precision=precision)


def _dot_nt(a, b):
    return lax.dot_general(a, b, (((1,), (1,)), ((), ())), preferred_element_type=F32)


def _dot_tn(a, b):
    return lax.dot_general(a, b, (((0,), (0,)), ((), ())), preferred_element_type=F32)


def _bdot(a, b):
    return _dot(a.astype(BF16), b.astype(BF16))


def _bdot_nt(a, b):
    return _dot_nt(a.astype(BF16), b.astype(BF16))


def _bdot_tn(a, b):
    return _dot_tn(a.astype(BF16), b.astype(BF16))


def _dot_exact01(a01, x):
    hi = x.astype(BF16)
    r1 = x - hi.astype(F32)
    mid = r1.astype(BF16)
    lo = (r1 - mid.astype(F32)).astype(BF16)
    return _dot(a01, hi) + _dot(a01, mid) + _dot(a01, lo)


def _sigmoid(x):
    return 1.0 / (1.0 + jnp.exp(-x))


def _silu(x):
    return x * _sigmoid(x)


def _softplus(x):
    return jnp.maximum(x, 0.0) + jnp.log1p(jnp.exp(-jnp.abs(x)))


def _log_sigmoid(x):
    return -_softplus(-x)


def _mod_kernel(c_ref, w_ref, b_ref, o_ref):
    o_ref[0] = _dot(_silu(c_ref[...]), w_ref[0], precision=HIGHEST) + b_ref[0]


def _modulation(cc, w_mod, b_mod):
    n_layers, d, n = w_mod.shape
    rows = cc.shape[0]
    tn = _largest_tile(n, 1024, LANES)
    return pl.pallas_call(
        _mod_kernel,
        grid=(n_layers, n // tn),
        in_specs=[pl.BlockSpec((rows, d), lambda l, j: (0, 0)),
                  pl.BlockSpec((1, d, tn), lambda l, j: (l, 0, j)),
                  pl.BlockSpec((1, 1, tn), lambda l, j: (l, 0, j))],
        out_specs=pl.BlockSpec((1, rows, tn), lambda l, j: (l, 0, j)),
        out_shape=jax.ShapeDtypeStruct((n_layers, rows, n), F32),
        compiler_params=_cparams(("parallel", "parallel")),
        name="modulation",
    )(cc, w_mod, b_mod.reshape(n_layers, 1, n))


D_MODEL = 1024
MIX_W = 512
N_HEADS = MIX_W // HEAD_DIM
COL_MERGE = 0
COL_GDN_Z = 4096
COL_GDN_QKV = 4608
COL_HG_Q = 6144
COL_HG_I = 6656
COL_HG_F = 7168
COL_HG_G = 8192
COL_LRU_X = 8704
COL_LRU_Y = 9216
COL_RET_Q = 9728
COL_RET_K = 10240
COL_RET_V = 10752
COL_RET_G = 11264
COL_AB = 11776
N_PROJ = 12288


class _Geo:
    def __init__(self, b, n_lat, n_ctx):
        self.b, self.n_lat, self.n_ctx = b, n_lat, n_ctx
        self.n_tot = n_lat + n_ctx
        self.m_lat = b * n_lat
        self.m = b * self.n_tot
        assert n_lat % CHUNK == 0 and n_ctx % CHUNK == 0 and n_lat % GRID_W == 0
        self.row_tile = math.gcd(n_lat, n_ctx)

    def mod_row(self, i, tm, compact=False):
        r0 = i * tm
        if compact:
            return r0 // self.n_lat
        return jnp.where(r0 % self.n_tot < self.n_lat, r0 // self.n_tot, self.b)

    def full_tile(self, i, tm, compact=False):
        if not compact:
            return i
        per_lat, per_tot = self.n_lat // tm, self.n_tot // tm
        return (i // per_lat) * per_tot + i % per_lat

    def seq_edges(self, i, tr):
        pos = (i * tr) % self.n_tot
        first = jnp.logical_or(pos == 0, pos == self.n_lat)
        last = jnp.logical_or(pos + tr == self.n_lat, pos + tr == self.n_tot)
        return first, last


def _reorder_w_in(w_in):
    w = w_in.astype(BF16)
    d = w.shape[0]
    pad = lambda n: jnp.zeros((d, n), BF16)
    ab = [jnp.concatenate([w[:, 2048 + 4 * dr:2052 + 4 * dr], w[:, 2056 + 4 * dr:2060 + 4 * dr],
                           pad(LANES - 8)], axis=1) for dr in range(2)]
    out = jnp.concatenate([w[:, 7696:11792], w[:, 1536:2048], w[:, 0:1536], w[:, 2064:7696],
                           ab[0], ab[1], pad(N_PROJ - COL_AB - 2 * LANES)], axis=1)
    assert out.shape[1] == N_PROJ
    return out


def _rms_mod(x, g, sc, sh):
    y = x * lax.rsqrt(jnp.mean(x * x, axis=-1, keepdims=True) + EPS) * g
    return y * (1.0 + sc) + sh


def _norm_kernel(x_ref, g_ref, sc_ref, sh_ref, h_ref):
    h_ref[...] = _rms_mod(x_ref[...], g_ref[...], sc_ref[0], sh_ref[0]).astype(BF16)


def _proj_kernel(h_ref, w_ref, o_ref):
    o_ref[...] = _dot(h_ref[...], w_ref[...])


def _norm_proj(geo, x_rows, g, mods3, w_p):
    m, d = x_rows.shape
    n = w_p.shape[1]
    tr = _largest_tile(geo.row_tile, 512)
    h = pl.pallas_call(
        _norm_kernel,
        grid=(m // tr,),
        in_specs=[pl.BlockSpec((tr, d), lambda i: (i, 0)),
                  pl.BlockSpec((1, d), lambda i: (0, 0)),
                  pl.BlockSpec((1, 1, d), lambda i: (geo.mod_row(i, tr), 0, 1)),
                  pl.BlockSpec((1, 1, d), lambda i: (geo.mod_row(i, tr), 0, 0))],
        out_specs=pl.BlockSpec((tr, d), lambda i: (i, 0)),
        out_shape=jax.ShapeDtypeStruct((m, d), BF16),
        compiler_params=_cparams(("parallel",)),
        name="norm1",
    )(x_rows, g.reshape(1, d), mods3, mods3)
    tm = _largest_tile(m, 1024)
    tn = _largest_tile(n, 2048, LANES)
    return pl.pallas_call(
        _proj_kernel,
        grid=(n // tn, m // tm),
        in_specs=[pl.BlockSpec((tm, d), lambda j, i: (i, 0)),
                  pl.BlockSpec((d, tn), lambda j, i: (0, j))],
        out_specs=pl.BlockSpec((tm, tn), lambda j, i: (i, j)),
        out_shape=jax.ShapeDtypeStruct((m, n), F32),
        compiler_params=_cparams(("parallel", "parallel")),
        name="in_proj",
    )(h, w_p)


def _halo_specs(geo, tr, width, col_block):
    nb8 = geo.m // SUBLANES
    r8 = tr // SUBLANES
    return [pl.BlockSpec((SUBLANES, width), lambda i: (jnp.maximum(i * r8 - 1, 0), col_block)),
            pl.BlockSpec((tr, width), lambda i: (i, col_block)),
            pl.BlockSpec((SUBLANES, width), lambda i: (jnp.minimum((i + 1) * r8, nb8 - 1), col_block))]


def _token_conv(prev_ref, x_ref, next_ref, w_ref, first, last):
    x = x_ref[...]
    tr = x.shape[0]
    prev = jnp.where(first, 0.0, prev_ref[...])
    nxt = jnp.where(last, 0.0, next_ref[...])
    ext = jnp.concatenate([prev, x, nxt], axis=0)
    w = w_ref[...]
    y = None
    for j in range(CONV_W):
        off = SUBLANES - CONV_W // 2 + j
        term = ext[off:off + tr] * w[j:j + 1]
        y = term if y is None else y + term
    return y


def _gdn_prep_kernel(geo, tr, prev_ref, x_ref, next_ref, ab0_ref, ab1_ref, cw_ref, alog_ref, dtb_ref,
                     q_ref, k_ref, v_ref, gb_ref):
    first, last = geo.seq_edges(pl.program_id(0), tr)
    s = _silu(_token_conv(prev_ref, x_ref, next_ref, cw_ref, first, last))
    for h in range(N_HEADS):
        sl = slice(h * HEAD_DIM, (h + 1) * HEAD_DIM)
        qh = s[:, sl]
        kh = s[:, MIX_W + h * HEAD_DIM:MIX_W + (h + 1) * HEAD_DIM]
        q_ref[:, sl] = qh * (lax.rsqrt(jnp.sum(qh * qh, axis=-1, keepdims=True) + EPS) * HEAD_DIM ** -0.5)
        k_ref[:, sl] = kh * lax.rsqrt(jnp.sum(kh * kh, axis=-1, keepdims=True) + EPS)
    v_ref[...] = s[:, 2 * MIX_W:]
    lane = lax.broadcasted_iota(jnp.int32, (tr, LANES), 1)
    for dr, ab_ref in enumerate((ab0_ref, ab1_ref)):
        a = ab_ref[...]
        g = -jnp.exp(alog_ref[dr]) * _softplus(a + dtb_ref[dr])
        gb_ref[dr] = jnp.where(lane < N_HEADS, g, _sigmoid(a))


def _gdn_prep(geo, p, conv_w, a_log, dt_bias):
    tr = _largest_tile(math.gcd(geo.n_lat, geo.n_ctx), 256)
    m = geo.m
    lane_pad = lambda v: jnp.pad(v.astype(F32), ((0, 0), (0, LANES - N_HEADS))).reshape(2, 1, LANES)
    ab_blk = COL_AB // LANES
    rows = lambda width: pl.BlockSpec((tr, width), lambda i: (i, 0))
    full3 = pl.BlockSpec((2, 1, LANES), lambda i: (0, 0, 0))
    return pl.pallas_call(
        functools.partial(_gdn_prep_kernel, geo, tr),
        grid=(m // tr,),
        in_specs=_halo_specs(geo, tr, 3 * MIX_W, COL_GDN_QKV // (3 * MIX_W)) + [
            pl.BlockSpec((tr, LANES), lambda i: (i, ab_blk)),
            pl.BlockSpec((tr, LANES), lambda i: (i, ab_blk + 1)),
            pl.BlockSpec((CONV_W, 3 * MIX_W), lambda i: (0, 0)), full3, full3],
        out_specs=[rows(MIX_W), rows(MIX_W), rows(MIX_W),
                   pl.BlockSpec((2, tr, LANES), lambda i: (0, i, 0))],
        out_shape=[jax.ShapeDtypeStruct((m, MIX_W), F32)] * 3 + [jax.ShapeDtypeStruct((2, m, LANES), F32)],
        compiler_params=_cparams(("parallel",)),
        name="gdn_prep",
    )(p, p, p, p, p, conv_w.astype(F32), lane_pad(a_log), lane_pad(dt_bias))


def _chunk_index(geo, d, j):
    nc_ctx, nc_lat = geo.n_ctx // CHUNK, geo.n_lat // CHUNK
    jc = jnp.where(d == 0, j, nc_ctx - 1 - j)
    jl = jnp.where(d == 0, j - nc_ctx, nc_lat - 1 - (j - nc_ctx))
    return jnp.where(j < nc_ctx, nc_lat + jc, jl)


def _scan_grid(geo):
    return (2, geo.n_tot // CHUNK)


def _chunk_spec(geo, width, col_block=0):
    return pl.BlockSpec((geo.b, CHUNK, width), lambda d, j: (0, _chunk_index(geo, d, j), col_block))


def _dir_chunk_spec(geo, width):
    return pl.BlockSpec((1, geo.b, CHUNK, width), lambda d, j: (d, 0, _chunk_index(geo, d, j), 0))


def _per_dir_spec(shape):
    return pl.BlockSpec((1,) + shape, lambda d, j: (d,) + (0,) * len(shape))


N_LEVELS = int(math.log2(CHUNK))


def _chunk_consts():
    t = np.arange(CHUNK)
    tt, uu = t[:, None], t[None, :]
    tri = uu <= tt
    pair = []
    for lv in range(N_LEVELS):
        hi = ((t >> lv) & 1) == 1
        pair.append(((uu >> (lv + 1)) == (tt >> (lv + 1))) & hi[:, None] & (~hi)[None, :])
    pair = np.stack(pair)
    both = lambda a: np.stack([a, a[..., ::-1, ::-1]]).astype(np.float32)
    return jnp.asarray(both(tri)), jnp.asarray(both(pair))


def _gdn_scan_kernel(n_batch, q_ref, k_ref, v_ref, gb_ref, tri_ref, pair_ref, o_ref, s_ref):
    @pl.when(pl.program_id(1) == 0)
    def _():
        s_ref[...] = jnp.zeros_like(s_ref)

    tri = tri_ref[0]
    row = lax.broadcasted_iota(jnp.int32, (CHUNK, CHUNK), 0)
    col = lax.broadcasted_iota(jnp.int32, (CHUNK, CHUNK), 1)
    incl = tri > 0.5
    eye = (row == col).astype(F32)
    chains = []
    for b in range(n_batch):
        gb = gb_ref[0, b]
        gcum = _dot(tri, gb, precision=HIGHEST)
        gcum_t = gcum.T
        gtot = jnp.sum(gb, axis=0, keepdims=True)
        q, k, v = q_ref[b], k_ref[b], v_ref[b]
        for h in range(N_HEADS):
            sl = slice(h * HEAD_DIM, (h + 1) * HEAD_DIM)
            chains.append(dict(b=b, h=h, sl=sl, bc=gcum[:, h:h + 1], br=gcum_t[h:h + 1, :],
                               beta=gb[:, N_HEADS + h:N_HEADS + h + 1], bl=gtot[:, h:h + 1],
                               q=q[:, sl], k=k[:, sl], v=v[:, sl], s=s_ref[b, h]))
    for c in chains:
        c["decay"] = jnp.where(incl, jnp.exp(jnp.where(incl, c["bc"] - c["br"], 0.0)), 0.0)
        c["kbeta"] = c["k"] * c["beta"]
        c["k16"] = c["k"].astype(BF16)
        c["mm"] = _dot_nt(c["kbeta"].astype(BF16), c["k16"]) * c["decay"]
        c["dinv"] = eye - pair_ref[0, 0] * c["mm"]
    for lv in range(1, N_LEVELS):
        for c in chains:
            c["ld"] = _bdot(pair_ref[0, lv] * c["mm"], c["dinv"])
        for c in chains:
            c["dinv"] = c["dinv"] - _bdot(c["dinv"], c["ld"])
    for c in chains:
        c["eb"] = jnp.exp(c["bc"])
        rhs = jnp.concatenate([c["v"] * c["beta"], c["kbeta"] * c["eb"]], axis=-1)
        c["sol"] = rhs + _bdot(c["dinv"] - eye, rhs)
        c["attn"] = _dot_nt(c["q"].astype(BF16), c["k16"]) * c["decay"]
    for c in chains:
        c["v_new"] = c["sol"][:, :HEAD_DIM] - _bdot(c["sol"][:, HEAD_DIM:], c["s"])
    for c in chains:
        o_ref[0, c["b"], :, c["sl"]] = _bdot(c["q"] * c["eb"], c["s"]) + _bdot(c["attn"], c["v_new"])
        s_ref[c["b"], c["h"]] = (c["s"] * jnp.exp(c["bl"])
                                 + _bdot_tn(c["k"] * jnp.exp(c["bl"] - c["bc"]), c["v_new"]))


def _state_scratch(geo):
    return pltpu.VMEM((geo.b, N_HEADS, HEAD_DIM, HEAD_DIM), F32)


def _gdn_scan(geo, q, k, v, gb):
    tri, pair = _chunk_consts()
    view = lambda a: a.reshape(geo.b, geo.n_tot, a.shape[-1])
    return pl.pallas_call(
        functools.partial(_gdn_scan_kernel, geo.b),
        grid=_scan_grid(geo),
        in_specs=[_chunk_spec(geo, MIX_W), _chunk_spec(geo, MIX_W), _chunk_spec(geo, MIX_W),
                  _dir_chunk_spec(geo, LANES), _per_dir_spec((CHUNK, CHUNK)),
                  _per_dir_spec((N_LEVELS, CHUNK, CHUNK))],
        out_specs=_dir_chunk_spec(geo, MIX_W),
        out_shape=jax.ShapeDtypeStruct((2, geo.b, geo.n_tot, MIX_W), F32),
        scratch_shapes=[_state_scratch(geo)],
        compiler_params=_cparams(("parallel", "arbitrary")),
        name="gdn_scan",
    )(view(q), view(k), view(v), gb.reshape(2, geo.b, geo.n_tot, LANES), tri, pair)


def _hgrn_consts():
    t = np.arange(CHUNK)
    tt, uu = t[:, None], t[None, :]
    a_all, rowhi, pair = [(uu <= tt)], [], [(uu == tt)]
    for lv in range(N_LEVELS):
        hi = ((t >> lv) & 1) == 1
        same_half = (uu >> lv) == (tt >> lv)
        a_all.append(same_half & np.where(hi[:, None], uu <= tt, uu > tt))
        rowhi.append(np.broadcast_to(hi[:, None], (CHUNK, HEAD_DIM)))
        pair.append(((uu >> (lv + 1)) == (tt >> (lv + 1))) & hi[:, None] & (~hi)[None, :])
    mirror = lambda a: a[::-1, ::-1] if a.shape[1] == CHUNK else a[::-1]
    stack = lambda xs: np.stack([np.stack(xs), np.stack([mirror(x) for x in xs])]).astype(np.float32)
    a = stack(a_all)
    return (jnp.asarray(a.reshape(2, (N_LEVELS + 1) * CHUNK, CHUNK)), jnp.asarray(stack(rowhi)),
            jnp.asarray(stack(pair)))


def _hgrn_scan_kernel(n_batch, q_ref, i_ref, z_ref, lb_ref, a_ref, rowhi_ref, pair_ref, o_ref, st_ref):
    @pl.when(pl.program_id(1) == 0)
    def _():
        st_ref[...] = jnp.zeros_like(st_ref)

    lb = lb_ref[0]
    log_lb = jnp.log(jnp.maximum(lb, LB_MIN))
    log_1mlb = jnp.log1p(-lb)
    a16 = a_ref[0].astype(BF16)
    chains = []
    for b in range(n_batch):
        z = z_ref[b]
        other = log_1mlb + _log_sigmoid(z)
        log_f = jnp.maximum(log_lb, other) + jnp.log1p(jnp.exp(-jnp.abs(log_lb - other)))
        key = (1.0 - lb) * _sigmoid(-z)
        q = _silu(q_ref[b]) * HEAD_DIM ** -0.5
        v = i_ref[b]
        e_all = _dot_exact01(a16, log_f)
        ltot = jnp.sum(log_f, axis=0, keepdims=True)
        for h in range(N_HEADS):
            sl = slice(h * HEAD_DIM, (h + 1) * HEAD_DIM)
            chains.append(dict(b=b, h=h, sl=sl, q=q[:, sl], k=key[:, sl], v=v[:, sl], e=e_all[:, sl],
                               bl=ltot[:, sl], st=st_ref[b, h]))
    for c in chains:
        c["scores"] = pair_ref[0, 0] * _bdot_nt(c["q"], c["k"])
    for lv in range(N_LEVELS):
        hi = rowhi_ref[0, lv]
        for c in chains:
            x = jnp.exp(c["e"][(lv + 1) * CHUNK:(lv + 2) * CHUNK])
            c["scores"] = c["scores"] + pair_ref[0, lv + 1] * _bdot_nt(c["q"] * (x * hi), c["k"] * (x * (1.0 - hi)))
    for c in chains:
        bh = c["e"][:CHUNK]
        o_ref[0, c["b"], :, c["sl"]] = _bdot(c["scores"], c["v"]) + _bdot_nt(c["q"] * jnp.exp(bh), c["st"])
        st_ref[c["b"], c["h"]] = (c["st"] * jnp.exp(c["bl"])
                                  + _bdot_tn(c["v"], c["k"] * jnp.exp(c["bl"] - bh)))


def _hgrn_scan(geo, p, lb):
    a_all, rowhi, pair = _hgrn_consts()
    p3 = p.reshape(geo.b, geo.n_tot, N_PROJ)
    return pl.pallas_call(
        functools.partial(_hgrn_scan_kernel, geo.b),
        grid=_scan_grid(geo),
        in_specs=[_chunk_spec(geo, MIX_W, COL_HG_Q // MIX_W), _chunk_spec(geo, MIX_W, COL_HG_I // MIX_W),
                  pl.BlockSpec((geo.b, CHUNK, MIX_W),
                               lambda d, j: (0, _chunk_index(geo, d, j), COL_HG_F // MIX_W + d)),
                  _per_dir_spec((1, MIX_W)), _per_dir_spec(((N_LEVELS + 1) * CHUNK, CHUNK)),
                  _per_dir_spec((N_LEVELS, CHUNK, HEAD_DIM)), _per_dir_spec((N_LEVELS + 1, CHUNK, CHUNK))],
        out_specs=_dir_chunk_spec(geo, MIX_W),
        out_shape=jax.ShapeDtypeStruct((2, geo.b, geo.n_tot, MIX_W), F32),
        scratch_shapes=[_state_scratch(geo)],
        compiler_params=_cparams(("parallel", "arbitrary")),
        name="hgrn_scan",
    )(p3, p3, p3, lb.reshape(2, 1, MIX_W), a_all, rowhi, pair)


def _ret_consts(n_lat):
    t = np.arange(CHUNK, dtype=np.float64)
    log_gamma = np.log1p(-np.exp2(-5.0 - np.arange(N_HEADS, dtype=np.float64)))
    dmat, qs, ks = [], [], []
    for rev in (False, True):
        te = t[::-1] if rev else t
        diff = te[:, None] - te[None, :]
        dmat.append(np.stack([np.where(diff >= 0, np.exp(lg * np.maximum(diff, 0)), 0.0) for lg in log_gamma]))
        qs.append(np.repeat(np.exp(log_gamma[None, :] * (te[:, None] + 1.0)), HEAD_DIM, axis=1))
        ks.append(np.repeat(np.exp(log_gamma[None, :] * (CHUNK - 1.0 - te[:, None])), HEAD_DIM, axis=1))
    g_chunk = np.repeat(np.exp(log_gamma * CHUNK), HEAD_DIM)[None, :]
    row = np.repeat(np.arange(n_lat // GRID_W, dtype=np.float32), GRID_W)
    colp = np.tile(np.arange(GRID_W, dtype=np.float32), n_lat // GRID_W)
    n_freq = HEAD_DIM // 4
    inv_freq = (np.float32(ROPE_BASE) ** (-np.arange(n_freq, dtype=np.float32) / n_freq)).astype(np.float32)
    ang = np.concatenate([row[:, None] * inv_freq, colp[:, None] * inv_freq], axis=-1).astype(np.float32)
    cos, sin = np.cos(ang), np.sin(ang)
    f32 = lambda a: jnp.asarray(np.asarray(a, np.float32))
    return (f32(np.stack(dmat)), f32(np.stack(qs)), f32(np.stack(ks)), f32(g_chunk),
            f32(np.concatenate([cos, cos], axis=-1)), f32(np.concatenate([-sin, sin], axis=-1)))


def _ret_scan_kernel(n_batch, nc_ctx, q_ref, k_ref, v_ref, cos_ref, sin_ref, dm_ref, qs_ref, ks_ref, gc_ref,
                     o_ref, s_ref):
    j = pl.program_id(1)

    @pl.when(j == 0)
    def _():
        s_ref[...] = jnp.zeros_like(s_ref)

    is_lat = j >= nc_ctx
    cos = jnp.where(is_lat, cos_ref[...], 1.0)
    sin = jnp.where(is_lat, sin_ref[...], 0.0)
    qs, ks, gc = qs_ref[0], ks_ref[0], gc_ref[...]
    chains = []
    for b in range(n_batch):
        q, k, v = q_ref[b], k_ref[b], v_ref[b]
        for h in range(N_HEADS):
            sl = slice(h * HEAD_DIM, (h + 1) * HEAD_DIM)
            qh, kh = q[:, sl], k[:, sl]
            rq = qh * cos + pltpu.roll(qh, HEAD_DIM // 2, 1) * sin
            rk = (kh * cos + pltpu.roll(kh, HEAD_DIM // 2, 1) * sin) * HEAD_DIM ** -0.5
            chains.append(dict(b=b, h=h, sl=sl, rq=rq, rk=rk, v=v[:, sl], s=s_ref[b, h]))
    for c in chains:
        c["scores"] = _bdot_nt(c["rq"], c["rk"]) * dm_ref[0, c["h"]]
    for c in chains:
        sl = c["sl"]
        o_ref[0, c["b"], :, sl] = _bdot(c["scores"], c["v"]) + _bdot(c["rq"] * qs[:, sl], c["s"])
        s_ref[c["b"], c["h"]] = c["s"] * gc[:, sl] + _bdot_tn(c["rk"] * ks[:, sl], c["v"])


def _ret_scan(geo, p):
    dmat, qs, ks, gc, cos2, sin2 = _ret_consts(geo.n_lat)
    nc_ctx = geo.n_ctx // CHUNK
    p3 = p.reshape(geo.b, geo.n_tot, N_PROJ)

    def rope_blk(d, j):
        return (jnp.where(j < nc_ctx, 0, _chunk_index(geo, d, j)), 0)

    return pl.pallas_call(
        functools.partial(_ret_scan_kernel, geo.b, nc_ctx),
        grid=_scan_grid(geo),
        in_specs=[_chunk_spec(geo, MIX_W, COL_RET_Q // MIX_W), _chunk_spec(geo, MIX_W, COL_RET_K // MIX_W),
                  _chunk_spec(geo, MIX_W, COL_RET_V // MIX_W),
                  pl.BlockSpec((CHUNK, HEAD_DIM), rope_blk), pl.BlockSpec((CHUNK, HEAD_DIM), rope_blk),
                  _per_dir_spec((N_HEADS, CHUNK, CHUNK)), _per_dir_spec((CHUNK, MIX_W)),
                  _per_dir_spec((CHUNK, MIX_W)), pl.BlockSpec((1, MIX_W), lambda d, j: (0, 0))],
        out_specs=_dir_chunk_spec(geo, MIX_W),
        out_shape=jax.ShapeDtypeStruct((2, geo.b, geo.n_tot, MIX_W), F32),
        scratch_shapes=[_state_scratch(geo)],
        compiler_params=_cparams(("parallel", "arbitrary")),
        name="ret_scan",
    )(p3, p3, p3, cos2, sin2, dmat, qs, ks, gc)


def _lru_kernel(geo, tr, rev, prev_ref, x_ref, next_ref, cw_ref, cb_ref, wg_ref, bg_ref, lam_ref,
                o_ref, h_ref):
    b, j = pl.program_id(0), pl.program_id(1)

    @pl.when(j == 0)
    def _():
        h_ref[...] = jnp.zeros_like(h_ref)

    first, last = geo.seq_edges(_lru_block(geo, tr, rev, b, j), tr)
    xb = _token_conv(prev_ref, x_ref, next_ref, cw_ref, first, last) + cb_ref[...]
    gates = _bdot(xb, wg_ref[...]) + bg_ref[...]
    gate_r = _sigmoid(gates[:, :MIX_W])
    gate_i = _sigmoid(gates[:, MIX_W:])
    log_a = LRU_C * gate_r * _log_sigmoid(lam_ref[...])
    a = jnp.exp(log_a)
    u = xb * gate_i * jnp.sqrt(jnp.maximum(-jnp.tanh(log_a) * (a * a + 1.0), SQRT_MIN))
    row = lax.broadcasted_iota(jnp.int32, (tr, MIX_W), 0)
    step = 1
    while step < tr:
        shift = tr - step if rev else step
        valid = (row < tr - step) if rev else (row >= step)
        u = jnp.where(valid, a * pltpu.roll(u, shift, 0) + u, u)
        a = jnp.where(valid, a * pltpu.roll(a, shift, 0), a)
        step *= 2
    h = a * h_ref[...] + u
    o_ref[...] = h
    h_ref[...] = h[0:1] if rev else h[tr - 1:tr]


def _lru_block(geo, tr, rev, b, j):
    nb_ctx, nb_lat = geo.n_ctx // tr, geo.n_lat // tr
    jc = nb_ctx - 1 - j if rev else j
    jl = nb_lat - 1 - (j - nb_ctx) if rev else j - nb_ctx
    return b * (geo.n_tot // tr) + jnp.where(j < nb_ctx, nb_lat + jc, jl)


def _lru_scan(geo, p, conv_w, conv_b, wa, ba, wx, bx, lam, rev):
    tr = _largest_tile(geo.row_tile, 256)
    d = 1 if rev else 0
    blockdiag = lambda w: jax.scipy.linalg.block_diag(*[w[n] for n in range(LRU_BLOCKS)])
    wg = jnp.concatenate([blockdiag(wa[d]), blockdiag(wx[d])], axis=1).astype(BF16)
    bg = jnp.concatenate([ba[d], bx[d]]).astype(F32).reshape(1, 2 * MIX_W)
    nb8, r8 = geo.m // SUBLANES, tr // SUBLANES
    cblk = COL_LRU_X // MIX_W
    blk = lambda b, j: _lru_block(geo, tr, rev, b, j)
    full = lambda shape: pl.BlockSpec(shape, lambda b, j: (0,) * len(shape))
    return pl.pallas_call(
        functools.partial(_lru_kernel, geo, tr, rev),
        grid=(geo.b, (geo.n_ctx + geo.n_lat) // tr),
        in_specs=[pl.BlockSpec((SUBLANES, MIX_W), lambda b, j: (jnp.maximum(blk(b, j) * r8 - 1, 0), cblk)),
                  pl.BlockSpec((tr, MIX_W), lambda b, j: (blk(b, j), cblk)),
                  pl.BlockSpec((SUBLANES, MIX_W),
                               lambda b, j: (jnp.minimum((blk(b, j) + 1) * r8, nb8 - 1), cblk)),
                  full((CONV_W, MIX_W)), full((1, MIX_W)), full((MIX_W, 2 * MIX_W)),
                  full((1, 2 * MIX_W)), full((1, MIX_W))],
        out_specs=pl.BlockSpec((tr, MIX_W), lambda b, j: (blk(b, j), 0)),
        out_shape=jax.ShapeDtypeStruct((geo.m, MIX_W), F32),
        scratch_shapes=[pltpu.VMEM((1, MIX_W), F32)],
        compiler_params=_cparams(("parallel", "arbitrary")),
        name="lru_scan_bwd" if rev else "lru_scan_fwd",
    )(p, p, p, conv_w.astype(F32), conv_b.astype(F32).reshape(1, MIX_W), wg, bg,
      lam[d].astype(F32).reshape(1, MIX_W))


def _gelu_tanh(x):
    return 0.5 * x * (1.0 + jnp.tanh(math.sqrt(2.0 / math.pi) * (x + 0.044715 * (x * x * x))))


def _gated_head_norm(o, g, z):
    parts = []
    for h in range(N_HEADS):
        oh = o[:, h * HEAD_DIM:(h + 1) * HEAD_DIM]
        parts.append(oh * lax.rsqrt(jnp.mean(oh * oh, axis=-1, keepdims=True) + EPS) * g)
    return jnp.concatenate(parts, axis=-1) * _silu(z)


def _merge_kernel(og_ref, oh_ref, olf_ref, olb_ref, or_ref, gz_ref, hg_ref, ly_ref, rg_ref, mg_ref,
                  x_ref, gt_ref, gng_ref, hng_ref, rng_ref, wb_ref, wo_ref, out_ref):
    branches = (
        _gated_head_norm(og_ref[0] + og_ref[1], gng_ref[...], gz_ref[...]),
        _gated_head_norm(oh_ref[0] + oh_ref[1], hng_ref[...], hg_ref[...]),
        (olf_ref[...] + olb_ref[...]) * _gelu_tanh(ly_ref[...]),
        _gated_head_norm(or_ref[0] + or_ref[1], rng_ref[...], rg_ref[...]),
    )
    mix = None
    for n, y in enumerate(branches):
        term = _sigmoid(mg_ref[:, n * D_MODEL:(n + 1) * D_MODEL]) * _bdot(y, wb_ref[n])
        mix = term if mix is None else mix + term
    out_ref[...] = x_ref[...] + gt_ref[0] * _bdot(mix, wo_ref[...])


def _merge(geo, compact, x_rows, p, o_gdn, o_hgrn, o_lru_f, o_lru_b, o_ret, mods3, gdn_g, hgrn_g, ret_g,
           w_branch, w_out):
    tm = _largest_tile(geo.row_tile, 256)
    n_rows = geo.m_lat if compact else geo.m
    src = lambda i: geo.full_tile(i, tm, compact)
    both = pl.BlockSpec((2, tm, MIX_W), lambda i: (0, src(i), 0))
    rows = lambda width: pl.BlockSpec((tm, width), lambda i: (src(i), 0))
    col = lambda c, width: pl.BlockSpec((tm, width), lambda i: (src(i), c // width))
    full = lambda shape: pl.BlockSpec(shape, lambda i: (0,) * len(shape))
    hd = lambda g: g.astype(F32).reshape(1, HEAD_DIM)
    return pl.pallas_call(
        _merge_kernel,
        grid=(n_rows // tm,),
        in_specs=[both, both, rows(MIX_W), rows(MIX_W), both,
                  col(COL_GDN_Z, MIX_W), col(COL_HG_G, MIX_W), col(COL_LRU_Y, MIX_W), col(COL_RET_G, MIX_W),
                  col(COL_MERGE, N_BRANCH * D_MODEL), rows(D_MODEL),
                  pl.BlockSpec((1, 1, D_MODEL), lambda i: (geo.mod_row(i, tm, compact), 0, 2)),
                  full((1, HEAD_DIM)), full((1, HEAD_DIM)), full((1, HEAD_DIM)),
                  full((N_BRANCH, MIX_W, D_MODEL)), full((D_MODEL, D_MODEL))],
        out_specs=pl.BlockSpec((tm, D_MODEL), lambda i: (i, 0)),
        out_shape=jax.ShapeDtypeStruct((n_rows, D_MODEL), F32),
        compiler_params=_cparams(("parallel",)),
        name="merge",
    )(o_gdn, o_hgrn, o_lru_f, o_lru_b, o_ret, p, p, p, p, p, x_rows, mods3, hd(gdn_g), hd(hgrn_g), hd(ret_g),
      w_branch.astype(BF16), w_out.astype(BF16))


def _router_kernel(n_experts, x_ref, g_ref, sc_ref, sh_ref, rw_ref, rb_ref, f_ref, idx_ref, wt_ref):
    f = _rms_mod(x_ref[...], g_ref[...], sc_ref[0], sh_ref[0])
    f_ref[...] = f
    logits = _dot(f, rw_ref[...], precision=HIGHEST) + rb_ref[...]
    lane = lax.broadcasted_iota(jnp.int32, logits.shape, 1)
    logits = jnp.where(lane < n_experts, logits, NEG_BIG)
    vals, idxs = [], []
    for _ in range(TOP_K):
        mx = jnp.max(logits, axis=-1, keepdims=True)
        ix = jnp.min(jnp.where(logits == mx, lane, LANES), axis=-1, keepdims=True)
        vals.append(mx)
        idxs.append(ix)
        logits = jnp.where(lane == ix, -jnp.inf, logits)
    ex = [jnp.exp(vl - vals[0]) for vl in vals]
    tot = ex[0] + ex[1] + ex[2] + ex[3]
    idx_out = jnp.zeros(lane.shape, jnp.int32)
    wt_out = jnp.zeros(lane.shape, F32)
    for kk in range(TOP_K):
        idx_out = jnp.where(lane == kk, idxs[kk], idx_out)
        wt_out = jnp.where(lane == kk, ex[kk] / tot, wt_out)
    idx_ref[...] = idx_out
    wt_ref[...] = wt_out


def _router(geo, compact, x_rows, g, mods3, router_w, router_b):
    n_rows, d = x_rows.shape
    n_experts = router_w.shape[1]
    tm = _largest_tile(geo.row_tile, 256)
    rw = jnp.pad(router_w.astype(F32), ((0, 0), (0, LANES - n_experts)))
    rb = jnp.pad(router_b.astype(F32), (0, LANES - n_experts)).reshape(1, LANES)
    rows = lambda width: pl.BlockSpec((tm, width), lambda i: (i, 0))
    full = lambda shape: pl.BlockSpec(shape, lambda i: (0,) * len(shape))
    return pl.pallas_call(
        functools.partial(_router_kernel, n_experts),
        grid=(n_rows // tm,),
        in_specs=[rows(d), full((1, d)),
                  pl.BlockSpec((1, 1, d), lambda i: (geo.mod_row(i, tm, compact), 0, 4)),
                  pl.BlockSpec((1, 1, d), lambda i: (geo.mod_row(i, tm, compact), 0, 3)),
                  full((d, LANES)), full((1, LANES))],
        out_specs=[rows(d), rows(LANES), rows(LANES)],
        out_shape=[jax.ShapeDtypeStruct((n_rows, d), F32), jax.ShapeDtypeStruct((n_rows, LANES), jnp.int32),
                   jax.ShapeDtypeStruct((n_rows, LANES), F32)],
        compiler_params=_cparams(("parallel",)),
        name="router",
    )(x_rows, g.astype(F32).reshape(1, d), mods3, mods3, rw, rb)


def _route_plan(top_idx, n_experts):
    m = top_idx.shape[0]
    n_assign = m * TOP_K
    flat_e = top_idx.reshape(-1)
    order = jnp.argsort(flat_e).astype(jnp.int32)
    rank = jnp.argsort(order).astype(jnp.int32)
    counts = jnp.sum(flat_e[:, None] == jnp.arange(n_experts, dtype=jnp.int32)[None, :], axis=0, dtype=jnp.int32)
    padded = (counts + ROUTE_BLOCK - 1) // ROUTE_BLOCK * ROUTE_BLOCK
    pad_end = jnp.cumsum(padded)
    pad_start = pad_end - padded
    start = jnp.cumsum(counts) - counts
    n_blocks = -(-n_assign // ROUTE_BLOCK) + n_experts
    cap = n_blocks * ROUTE_BLOCK
    block_expert = jnp.minimum(
        jnp.searchsorted(pad_end, jnp.arange(n_blocks, dtype=jnp.int32) * ROUTE_BLOCK, side="right"),
        n_experts - 1).astype(jnp.int32)
    row = jnp.arange(cap, dtype=jnp.int32)
    row_e = jnp.repeat(block_expert, ROUTE_BLOCK)
    within = row - pad_start[row_e]
    valid = jnp.logical_and(within >= 0, within < counts[row_e])
    src_tok = jnp.where(valid, order[jnp.clip(start[row_e] + within, 0, n_assign - 1)] // TOP_K, 0)
    dest = pad_start[flat_e] + rank - start[flat_e]
    return src_tok.reshape(n_blocks, 1, ROUTE_BLOCK), block_expert, dest


def _start_row_gather(idx_ref, n, src_hbm, buf, slot, sem):
    for r in range(n):
        pltpu.make_async_copy(src_hbm.at[pl.ds(idx_ref[0, 0, r], 1)], buf.at[slot, pl.ds(r, 1)],
                              sem.at[slot]).start()


def _wait_row_gather(n, src_hbm, buf, slot, sem):
    pltpu.make_async_copy(src_hbm.at[pl.ds(0, n)], buf.at[slot], sem.at[slot]).wait()


def _gather_step(cur_ref, nxt_ref, n, src_hbm, buf, sem):
    i, nb = pl.program_id(0), pl.num_programs(0)
    slot = i % 2

    @pl.when(i == 0)
    def _():
        _start_row_gather(cur_ref, n, src_hbm, buf, 0, sem)

    _wait_row_gather(n, src_hbm, buf, slot, sem)
    _start_row_gather(nxt_ref, n, src_hbm, buf, 1 - slot, sem)
    return slot


def _gather_drain(n, src_hbm, buf, slot, sem):
    @pl.when(pl.program_id(0) == pl.num_programs(0) - 1)
    def _():
        _wait_row_gather(n, src_hbm, buf, 1 - slot, sem)


def _expert_kernel(be_ref, src_ref, nxt_ref, f_hbm, w1_ref, b1_ref, w2_ref, b2_ref, y_ref, buf, sem):
    del be_ref
    ff = w2_ref.shape[1]
    slot = _gather_step(src_ref, nxt_ref, ROUTE_BLOCK, f_hbm, buf, sem)
    z = _bdot(buf[slot], w1_ref[0]) + b1_ref[0]
    glu = jnp.minimum(z[:, :ff], SWIGLU_LIMIT)
    lin = jnp.clip(z[:, ff:], -SWIGLU_LIMIT, SWIGLU_LIMIT)
    act = glu * _sigmoid(SWIGLU_ALPHA * glu) * (lin + 1.0)
    y_ref[...] = _bdot(act, w2_ref[0]) + b2_ref[0]
    _gather_drain(ROUTE_BLOCK, f_hbm, buf, slot, sem)


def _experts(f, src_tok, block_expert, w1, b1, w2, b2):
    n_blocks = src_tok.shape[0]
    d = f.shape[1]
    n_experts, _, ff2 = w1.shape
    ff = w2.shape[1]
    grid_spec = pltpu.PrefetchScalarGridSpec(
        num_scalar_prefetch=1,
        grid=(n_blocks,),
        in_specs=[pl.BlockSpec((1, 1, ROUTE_BLOCK), lambda i, be: (i, 0, 0), memory_space=pltpu.SMEM),
                  pl.BlockSpec((1, 1, ROUTE_BLOCK), lambda i, be: (jnp.minimum(i + 1, n_blocks - 1), 0, 0),
                               memory_space=pltpu.SMEM),
                  pl.BlockSpec(memory_space=pl.ANY),
                  pl.BlockSpec((1, d, ff2), lambda i, be: (be[i], 0, 0)),
                  pl.BlockSpec((1, 1, ff2), lambda i, be: (be[i], 0, 0)),
                  pl.BlockSpec((1, ff, d), lambda i, be: (be[i], 0, 0)),
                  pl.BlockSpec((1, 1, d), lambda i, be: (be[i], 0, 0))],
        out_specs=pl.BlockSpec((ROUTE_BLOCK, d), lambda i, be: (i, 0)),
        scratch_shapes=[pltpu.VMEM((2, ROUTE_BLOCK, d), F32), pltpu.SemaphoreType.DMA((2,))],
    )
    return pl.pallas_call(
        _expert_kernel,
        grid_spec=grid_spec,
        out_shape=jax.ShapeDtypeStruct((n_blocks * ROUTE_BLOCK, d), F32),
        compiler_params=_cparams(("arbitrary",)),
        name="experts",
    )(block_expert, src_tok, src_tok, f, w1.astype(BF16), b1.astype(F32).reshape(n_experts, 1, ff2),
      w2.astype(BF16), b2.astype(F32).reshape(n_experts, 1, d))


def _combine_kernel(tc, final, dst_ref, nxt_ref, y_hbm, x_ref, wt_ref, gt_ref, fg_ref, out_ref, buf, sem):
    n = TOP_K * tc
    slot = _gather_step(dst_ref, nxt_ref, n, y_hbm, buf, sem)
    wt = wt_ref[...]
    acc = None
    for kk in range(TOP_K):
        term = wt[:, kk:kk + 1] * buf[slot, kk * tc:(kk + 1) * tc]
        acc = term if acc is None else acc + term
    out = x_ref[...] + gt_ref[0] * acc
    if final:
        out = out * lax.rsqrt(jnp.mean(out * out, axis=-1, keepdims=True) + EPS) * fg_ref[...]
    out_ref[...] = out
    _gather_drain(n, y_hbm, buf, slot, sem)


def _combine(geo, y_pad, dest, x_rows, wt, mods3, final_g, final):
    n_rows, d = x_rows.shape
    tc = _largest_tile(geo.row_tile, 128)
    nt = n_rows // tc
    dst = dest.reshape(nt, tc, TOP_K).transpose(0, 2, 1).reshape(nt, 1, TOP_K * tc)
    rows = lambda width: pl.BlockSpec((tc, width), lambda i: (i, 0))
    return pl.pallas_call(
        functools.partial(_combine_kernel, tc, final),
        grid=(nt,),
        in_specs=[pl.BlockSpec((1, 1, TOP_K * tc), lambda i: (i, 0, 0), memory_space=pltpu.SMEM),
                  pl.BlockSpec((1, 1, TOP_K * tc), lambda i: (jnp.minimum(i + 1, nt - 1), 0, 0),
                               memory_space=pltpu.SMEM),
                  pl.BlockSpec(memory_space=pl.ANY), rows(d), rows(LANES),
                  pl.BlockSpec((1, 1, d), lambda i: (geo.mod_row(i, tc, final), 0, 5)),
                  pl.BlockSpec((1, d), lambda i: (0, 0))],
        out_specs=rows(d),
        out_shape=jax.ShapeDtypeStruct((n_rows, d), F32),
        scratch_shapes=[pltpu.VMEM((2, TOP_K * tc, d), F32), pltpu.SemaphoreType.DMA((2,))],
        compiler_params=_cparams(("arbitrary",)),
        name="moe_combine",
    )(dst, dst, y_pad, x_rows, wt, mods3, final_g.astype(F32).reshape(1, d))


def _moe(geo, x_rows, g, mods3, router_w, router_b, w1, b1, w2, b2, final_g, final):
    f, top_idx, top_w = _router(geo, final, x_rows, g, mods3, router_w, router_b)
    src_tok, block_expert, dest = _route_plan(top_idx[:, :TOP_K], router_w.shape[1])
    y_pad = _experts(f, src_tok, block_expert, w1, b1, w2, b2)
    return _combine(geo, y_pad, dest, x_rows, top_w, mods3, final_g, final)


def kernel(x, c, ctx, c_ctx, w_mod, b_mod, norm1_g, norm2_g, w_in, gdn_conv_w, gdn_a_log, gdn_dt_bias, gdn_norm_g, hgrn_lb, hgrn_norm_g, lru_conv_w, lru_conv_b, lru_wa, lru_ba, lru_wx, lru_bx, lru_lambda, ret_norm_g, w_branch, w_out, router_w, router_b, moe_w1, moe_b1, moe_w2, moe_b2, final_norm_g):
    b, n_lat, d = x.shape
    n_ctx = ctx.shape[1]
    depth = w_mod.shape[0]
    assert d == D_MODEL and w_branch.shape[2] == MIX_W
    geo = _Geo(b, n_lat, n_ctx)
    lb_soft = jax.nn.softmax(hgrn_lb.astype(F32), axis=0)
    lower_bounds = jnp.clip(jnp.cumsum(lb_soft, axis=0) - lb_soft[0], 0.0, LB_MAX)
    n_cond = -(-(b + 1) // SUBLANES) * SUBLANES
    cc = jnp.concatenate([c, c_ctx[None, :], jnp.zeros((n_cond - b - 1, d), c.dtype)], axis=0)
    mods = _modulation(cc.astype(F32), w_mod, b_mod)
    rows = jnp.concatenate([x, ctx], axis=1).reshape(geo.m, d)
    flat = lambda o: o.reshape(2, geo.m, MIX_W)
    for l in range(depth):
        last = l == depth - 1
        mods3 = mods[l].reshape(n_cond, 1, N_MOD * d)
        p = _norm_proj(geo, rows, norm1_g[l], mods3, _reorder_w_in(w_in[l]))
        gq, gk, gv, gb = _gdn_prep(geo, p, gdn_conv_w[l], gdn_a_log[l], gdn_dt_bias[l])
        o_gdn = flat(_gdn_scan(geo, gq, gk, gv, gb))
        o_hgrn = flat(_hgrn_scan(geo, p, lower_bounds[l]))
        lru_args = (geo, p, lru_conv_w[l], lru_conv_b[l], lru_wa[l], lru_ba[l], lru_wx[l], lru_bx[l],
                    lru_lambda[l])
        o_lru_f = _lru_scan(*lru_args, rev=False)
        o_lru_b = _lru_scan(*lru_args, rev=True)
        o_ret = flat(_ret_scan(geo, p))
        rows = _merge(geo, last, rows, p, o_gdn, o_hgrn, o_lru_f, o_lru_b, o_ret, mods3, gdn_norm_g[l],
                      hgrn_norm_g[l], ret_norm_g[l], w_branch[l], w_out[l])
        rows = _moe(geo, rows, norm2_g[l], mods3, router_w[l], router_b[l], moe_w1[l], moe_b1[l],
                    moe_w2[l], moe_b2[l], final_norm_g, last)
    return rows.reshape(b, n_lat, d)
```

```python
import functools
import math

import numpy as np
import jax
import jax.numpy as jnp
from jax import lax
from jax.experimental import pallas as pl
from jax.experimental.pallas import tpu as pltpu

F32 = jnp.float32
BF16 = jnp.bfloat16
HIGHEST = lax.Precision.HIGHEST

GRID_W = 64
N_BRANCH = 4
HEAD_DIM = 128
LRU_BLOCKS = 8
LRU_C = 8.0
CONV_W = 4
CHUNK = 64
ROPE_BASE = 10000.0
TOP_K = 4
SWIGLU_ALPHA = 1.702
SWIGLU_LIMIT = 7.0
N_MOD = 6
EPS = 1e-6
LB_MIN = 1e-30
LB_MAX = 1.0 - 1e-4
SQRT_MIN = 1e-12

LANES = 128
SUBLANES = 8
VMEM_LIMIT_BYTES = 56 * 1024 * 1024
ROUTE_BLOCK = 256
NEG_BIG = -1e30


def _cparams(semantics):
    return pltpu.CompilerParams(dimension_semantics=semantics, vmem_limit_bytes=VMEM_LIMIT_BYTES)


def _largest_tile(n, cap, quantum=SUBLANES):
    best = None
    for t in range(quantum, min(n, cap) + 1, quantum):
        if n % t == 0:
            best = t
    assert best is not None, (n, cap)
    return best


def _dot(a, b, precision=None):
    return jnp.dot(a, b, preferred_element_type=F32, precision=precision)


def _dot_nt(a, b):
    return lax.dot_general(a, b, (((1,), (1,)), ((), ())), preferred_element_type=F32)


def _dot_tn(a, b):
    return lax.dot_general(a, b, (((0,), (0,)), ((), ())), preferred_element_type=F32)


def _bdot(a, b):
    return _dot(a.astype(BF16), b.astype(BF16))


def _bdot_nt(a, b):
    return _dot_nt(a.astype(BF16), b.astype(BF16))


def _bdot_tn(a, b):
    return _dot_tn(a.astype(BF16), b.astype(BF16))


def _dot_exact01(a01x3, x):
    hi = x.astype(BF16)
    r1 = x - hi.astype(F32)
    mid = r1.astype(BF16)
    lo = (r1 - mid.astype(F32)).astype(BF16)
    return _dot(a01x3, jnp.concatenate([hi, mid, lo], axis=0))


def _sigmoid(x):
    return 1.0 / (1.0 + jnp.exp(-x))


def _silu(x):
    return x * _sigmoid(x)


def _softplus(x):
    return jnp.maximum(x, 0.0) + jnp.log1p(jnp.exp(-jnp.abs(x)))


def _log_sigmoid(x):
    return -_softplus(-x)


def _mod_kernel(c_ref, w_ref, b_ref, o_ref):
    o_ref[0] = _dot(_silu(c_ref[...]), w_ref[0], precision=HIGHEST) + b_ref[0]


def _modulation(cc, w_mod, b_mod):
    n_layers, d, n = w_mod.shape
    rows = cc.shape[0]
    tn = _largest_tile(n, 1024, LANES)
    return pl.pallas_call(
        _mod_kernel,
        grid=(n_layers, n // tn),
        in_specs=[pl.BlockSpec((rows, d), lambda l, j: (0, 0)),
                  pl.BlockSpec((1, d, tn), lambda l, j: (l, 0, j)),
                  pl.BlockSpec((1, 1, tn), lambda l, j: (l, 0, j))],
        out_specs=pl.BlockSpec((1, rows, tn), lambda l, j: (l, 0, j)),
        out_shape=jax.ShapeDtypeStruct((n_layers, rows, n), F32),
        compiler_params=_cparams(("parallel", "parallel")),
        name="modulation",
    )(cc, w_mod, b_mod.reshape(n_layers, 1, n))


D_MODEL = 1024
MIX_W = 512
N_HEADS = MIX_W // HEAD_DIM
COL_MERGE = 0
COL_GDN_Z = 4096
COL_GDN_QKV = 4608
COL_HG_Q = 6144
COL_HG_I = 6656
COL_HG_F = 7168
COL_HG_G = 8192
COL_LRU_X = 8704
COL_LRU_Y = 9216
COL_RET_Q = 9728
COL_RET_K = 10240
COL_RET_V = 10752
COL_RET_G = 11264
COL_AB = 11776
N_PROJ = 12288


class _Geo:
    def __init__(self, b, n_lat, n_ctx):
        self.b, self.n_lat, self.n_ctx = b, n_lat, n_ctx
        self.n_tot = n_lat + n_ctx
        self.m_lat = b * n_lat
        self.m = b * self.n_tot
        assert n_lat % CHUNK == 0 and n_ctx % CHUNK == 0 and n_lat % GRID_W == 0
        self.row_tile = math.gcd(n_lat, n_ctx)

    def mod_row(self, i, tm, compact=False):
        r0 = i * tm
        if compact:
            return r0 // self.n_lat
        return jnp.where(r0 % self.n_tot < self.n_lat, r0 // self.n_tot, self.b)

    def full_tile(self, i, tm, compact=False):
        if not compact:
            return i
        per_lat, per_tot = self.n_lat // tm, self.n_tot // tm
        return (i // per_lat) * per_tot + i % per_lat

    def seq_edges(self, i, tr):
        pos = (i * tr) % self.n_tot
        first = jnp.logical_or(pos == 0, pos == self.n_lat)
        last = jnp.logical_or(pos + tr == self.n_lat, pos + tr == self.n_tot)
        return first, last


def _reorder_w_in(w_in):
    w = w_in.astype(BF16)
    d = w.shape[0]
    pad = lambda n: jnp.zeros((d, n), BF16)
    ab = [jnp.concatenate([w[:, 2048 + 4 * dr:2052 + 4 * dr], w[:, 2056 + 4 * dr:2060 + 4 * dr],
                           pad(LANES - 8)], axis=1) for dr in range(2)]
    out = jnp.concatenate([w[:, 7696:11792], w[:, 1536:2048], w[:, 0:1536], w[:, 2064:7696],
                           ab[0], ab[1], pad(N_PROJ - COL_AB - 2 * LANES)], axis=1)
    assert out.shape[1] == N_PROJ
    return out


def _rms_mod(x, g, sc, sh):
    y = x * lax.rsqrt(jnp.mean(x * x, axis=-1, keepdims=True) + EPS) * g
    return y * (1.0 + sc) + sh


def _norm_kernel(x_ref, g_ref, sc_ref, sh_ref, h_ref):
    h_ref[...] = _rms_mod(x_ref[...], g_ref[...], sc_ref[0], sh_ref[0]).astype(BF16)


def _proj_kernel(h_ref, w_ref, o_ref):
    o_ref[...] = _dot(h_ref[...], w_ref[...])


def _norm_proj(geo, x_rows, g, mods3, w_p):
    m, d = x_rows.shape
    n = w_p.shape[1]
    tr = _largest_tile(geo.row_tile, 512)
    h = pl.pallas_call(
        _norm_kernel,
        grid=(m // tr,),
        in_specs=[pl.BlockSpec((tr, d), lambda i: (i, 0)),
                  pl.BlockSpec((1, d), lambda i: (0, 0)),
                  pl.BlockSpec((1, 1, d), lambda i: (geo.mod_row(i, tr), 0, 1)),
                  pl.BlockSpec((1, 1, d), lambda i: (geo.mod_row(i, tr), 0, 0))],
        out_specs=pl.BlockSpec((tr, d), lambda i: (i, 0)),
        out_shape=jax.ShapeDtypeStruct((m, d), BF16),
        compiler_params=_cparams(("parallel",)),
        name="norm1",
    )(x_rows, g.reshape(1, d), mods3, mods3)
    tm = _largest_tile(m, 1024)
    tn = _largest_tile(n, 2048, LANES)
    return pl.pallas_call(
        _proj_kernel,
        grid=(n // tn, m // tm),
        in_specs=[pl.BlockSpec((tm, d), lambda j, i: (i, 0)),
                  pl.BlockSpec((d, tn), lambda j, i: (0, j))],
        out_specs=pl.BlockSpec((tm, tn), lambda j, i: (i, j)),
        out_shape=jax.ShapeDtypeStruct((m, n), F32),
        compiler_params=_cparams(("parallel", "parallel")),
        name="in_proj",
    )(h, w_p)


def _halo_specs(geo, tr, width, col_block):
    nb8 = geo.m // SUBLANES
    r8 = tr // SUBLANES
    return [pl.BlockSpec((SUBLANES, width), lambda i: (jnp.maximum(i * r8 - 1, 0), col_block)),
            pl.BlockSpec((tr, width), lambda i: (i, col_block)),
            pl.BlockSpec((SUBLANES, width), lambda i: (jnp.minimum((i + 1) * r8, nb8 - 1), col_block))]


def _token_conv(prev_ref, x_ref, next_ref, w_ref, first, last):
    x = x_ref[...]
    tr = x.shape[0]
    prev = jnp.where(first, 0.0, prev_ref[...])
    nxt = jnp.where(last, 0.0, next_ref[...])
    ext = jnp.concatenate([prev, x, nxt], axis=0)
    w = w_ref[...]
    y = None
    for j in range(CONV_W):
        off = SUBLANES - CONV_W // 2 + j
        term = ext[off:off + tr] * w[j:j + 1]
        y = term if y is None else y + term
    return y


def _gdn_prep_kernel(geo, tr, prev_ref, x_ref, next_ref, ab0_ref, ab1_ref, cw_ref, alog_ref, dtb_ref,
                     q_ref, k_ref, v_ref, gb_ref):
    first, last = geo.seq_edges(pl.program_id(0), tr)
    s = _silu(_token_conv(prev_ref, x_ref, next_ref, cw_ref, first, last))
    for h in range(N_HEADS):
        sl = slice(h * HEAD_DIM, (h + 1) * HEAD_DIM)
        qh = s[:, sl]
        kh = s[:, MIX_W + h * HEAD_DIM:MIX_W + (h + 1) * HEAD_DIM]
        q_ref[:, sl] = qh * (lax.rsqrt(jnp.sum(qh * qh, axis=-1, keepdims=True) + EPS) * HEAD_DIM ** -0.5)
        k_ref[:, sl] = kh * lax.rsqrt(jnp.sum(kh * kh, axis=-1, keepdims=True) + EPS)
    v_ref[...] = s[:, 2 * MIX_W:]
    lane = lax.broadcasted_iota(jnp.int32, (tr, LANES), 1)
    for dr, ab_ref in enumerate((ab0_ref, ab1_ref)):
        a = ab_ref[...]
        g = -jnp.exp(alog_ref[dr]) * _softplus(a + dtb_ref[dr])
        gb_ref[dr] = jnp.where(lane < N_HEADS, g, _sigmoid(a))


def _gdn_prep(geo, p, conv_w, a_log, dt_bias):
    tr = _largest_tile(math.gcd(geo.n_lat, geo.n_ctx), 256)
    m = geo.m
    lane_pad = lambda v: jnp.pad(v.astype(F32), ((0, 0), (0, LANES - N_HEADS))).reshape(2, 1, LANES)
    ab_blk = COL_AB // LANES
    rows = lambda width: pl.BlockSpec((tr, width), lambda i: (i, 0))
    full3 = pl.BlockSpec((2, 1, LANES), lambda i: (0, 0, 0))
    return pl.pallas_call(
        functools.partial(_gdn_prep_kernel, geo, tr),
        grid=(m // tr,),
        in_specs=_halo_specs(geo, tr, 3 * MIX_W, COL_GDN_QKV // (3 * MIX_W)) + [
            pl.BlockSpec((tr, LANES), lambda i: (i, ab_blk)),
            pl.BlockSpec((tr, LANES), lambda i: (i, ab_blk + 1)),
            pl.BlockSpec((CONV_W, 3 * MIX_W), lambda i: (0, 0)), full3, full3],
        out_specs=[rows(MIX_W), rows(MIX_W), rows(MIX_W),
                   pl.BlockSpec((2, tr, LANES), lambda i: (0, i, 0))],
        out_shape=[jax.ShapeDtypeStruct((m, MIX_W), F32)] * 3 + [jax.ShapeDtypeStruct((2, m, LANES), F32)],
        compiler_params=_cparams(("parallel",)),
        name="gdn_prep",
    )(p, p, p, p, p, conv_w.astype(F32), lane_pad(a_log), lane_pad(dt_bias))


def _chunk_index(geo, d, j):
    nc_ctx, nc_lat = geo.n_ctx // CHUNK, geo.n_lat // CHUNK
    jc = jnp.where(d == 0, j, nc_ctx - 1 - j)
    jl = jnp.where(d == 0, j - nc_ctx, nc_lat - 1 - (j - nc_ctx))
    return jnp.where(j < nc_ctx, nc_lat + jc, jl)


def _scan_grid(geo):
    return (2, geo.n_tot // CHUNK)


def _chunk_spec(geo, width, col_block=0):
    return pl.BlockSpec((geo.b, CHUNK, width), lambda d, j: (0, _chunk_index(geo, d, j), col_block))


def _dir_chunk_spec(geo, width):
    return pl.BlockSpec((1, geo.b, CHUNK, width), lambda d, j: (d, 0, _chunk_index(geo, d, j), 0))


def _per_dir_spec(shape):
    return pl.BlockSpec((1,) + shape, lambda d, j: (d,) + (0,) * len(shape))


N_LEVELS = int(math.log2(CHUNK))


def _chunk_consts():
    t = np.arange(CHUNK)
    tt, uu = t[:, None], t[None, :]
    tri = uu <= tt
    pair = []
    for lv in range(N_LEVELS):
        hi = ((t >> lv) & 1) == 1
        pair.append(((uu >> (lv + 1)) == (tt >> (lv + 1))) & hi[:, None] & (~hi)[None, :])
    pair = np.stack(pair)
    both = lambda a: np.stack([a, a[..., ::-1, ::-1]]).astype(np.float32)
    return jnp.asarray(both(tri)), jnp.asarray(both(pair))


def _gdn_scan_kernel(n_batch, q_ref, k_ref, v_ref, gb_ref, tri_ref, pair_ref, o_ref, s_ref):
    @pl.when(pl.program_id(1) == 0)
    def _():
        s_ref[...] = jnp.zeros_like(s_ref)

    tri = tri_ref[0]
    row = lax.broadcasted_iota(jnp.int32, (CHUNK, CHUNK), 0)
    col = lax.broadcasted_iota(jnp.int32, (CHUNK, CHUNK), 1)
    incl = tri > 0.5
    eye = (row == col).astype(F32)
    chains = []
    for b in range(n_batch):
        gb = gb_ref[0, b]
        gcum = _dot(tri, gb, precision=HIGHEST)
        gcum_t = gcum.T
        gtot = jnp.sum(gb, axis=0, keepdims=True)
        q, k, v = q_ref[b], k_ref[b], v_ref[b]
        for h in range(N_HEADS):
            sl = slice(h * HEAD_DIM, (h + 1) * HEAD_DIM)
            chains.append(dict(b=b, h=h, sl=sl, bc=gcum[:, h:h + 1], br=gcum_t[h:h + 1, :],
                               beta=gb[:, N_HEADS + h:N_HEADS + h + 1], bl=gtot[:, h:h + 1],
                               q=q[:, sl], k=k[:, sl], v=v[:, sl], s=s_ref[b, h]))
    for c in chains:
        c["decay"] = jnp.where(incl, jnp.exp(jnp.where(incl, c["bc"] - c["br"], 0.0)), 0.0)
        c["kbeta"] = c["k"] * c["beta"]
        c["k16"] = c["k"].astype(BF16)
        c["mm"] = _dot_nt(c["kbeta"].astype(BF16), c["k16"]) * c["decay"]
        c["dinv"] = eye - pair_ref[0, 0] * c["mm"]
    for lv in range(1, N_LEVELS):
        for c in chains:
            c["ld"] = _bdot(pair_ref[0, lv] * c["mm"], c["dinv"])
        for c in chains:
            c["dinv"] = c["dinv"] - _bdot(c["dinv"], c["ld"])
    for c in chains:
        c["eb"] = jnp.exp(c["bc"])
        rhs = jnp.concatenate([c["v"] * c["beta"], c["kbeta"] * c["eb"]], axis=-1)
        c["sol"] = rhs + _bdot(c["dinv"] - eye, rhs)
        c["attn"] = _dot_nt(c["q"].astype(BF16), c["k16"]) * c["decay"]
    for c in chains:
        c["v_new"] = c["sol"][:, :HEAD_DIM] - _bdot(c["sol"][:, HEAD_DIM:], c["s"])
    for c in chains:
        o_ref[0, c["b"], :, c["sl"]] = _bdot(c["q"] * c["eb"], c["s"]) + _bdot(c["attn"], c["v_new"])
        s_ref[c["b"], c["h"]] = (c["s"] * jnp.exp(c["bl"])
                                 + _bdot_tn(c["k"] * jnp.exp(c["bl"] - c["bc"]), c["v_new"]))


def _state_scratch(geo):
    return pltpu.VMEM((geo.b, N_HEADS, HEAD_DIM, HEAD_DIM), F32)


def _gdn_scan(geo, q, k, v, gb):
    tri, pair = _chunk_consts()
    view = lambda a: a.reshape(geo.b, geo.n_tot, a.shape[-1])
    return pl.pallas_call(
        functools.partial(_gdn_scan_kernel, geo.b),
        grid=_scan_grid(geo),
        in_specs=[_chunk_spec(geo, MIX_W), _chunk_spec(geo, MIX_W), _chunk_spec(geo, MIX_W),
                  _dir_chunk_spec(geo, LANES), _per_dir_spec((CHUNK, CHUNK)),
                  _per_dir_spec((N_LEVELS, CHUNK, CHUNK))],
        out_specs=_dir_chunk_spec(geo, MIX_W),
        out_shape=jax.ShapeDtypeStruct((2, geo.b, geo.n_tot, MIX_W), F32),
        scratch_shapes=[_state_scratch(geo)],
        compiler_params=_cparams(("parallel", "arbitrary")),
        name="gdn_scan",
    )(view(q), view(k), view(v), gb.reshape(2, geo.b, geo.n_tot, LANES), tri, pair)


def _hgrn_consts():
    t = np.arange(CHUNK)
    tt, uu = t[:, None], t[None, :]
    a_all, rowhi, pair = [(uu <= tt)], [], [(uu == tt)]
    for lv in range(N_LEVELS):
        hi = ((t >> lv) & 1) == 1
        same_half = (uu >> lv) == (tt >> lv)
        a_all.append(same_half & np.where(hi[:, None], uu <= tt, uu > tt))
        rowhi.append(np.broadcast_to(hi[:, None], (CHUNK, HEAD_DIM)))
        pair.append(((uu >> (lv + 1)) == (tt >> (lv + 1))) & hi[:, None] & (~hi)[None, :])
    mirror = lambda a: a[::-1, ::-1] if a.shape[1] == CHUNK else a[::-1]
    stack = lambda xs: np.stack([np.stack(xs), np.stack([mirror(x) for x in xs])]).astype(np.float32)
    a = stack(a_all)
    return (jnp.asarray(a.reshape(2, (N_LEVELS + 1) * CHUNK, CHUNK)), jnp.asarray(stack(rowhi)),
            jnp.asarray(stack(pair)))


def _hgrn_scan_kernel(n_batch, q_ref, i_ref, z_ref, lb_ref, a_ref, rowhi_ref, pair_ref, o_ref, st_ref):
    @pl.when(pl.program_id(1) == 0)
    def _():
        st_ref[...] = jnp.zeros_like(st_ref)

    lb = lb_ref[0]
    log_lb = jnp.log(jnp.maximum(lb, LB_MIN))
    log_1mlb = jnp.log1p(-lb)
    a16 = a_ref[0].astype(BF16)
    a16 = jnp.concatenate([a16, a16, a16], axis=1)

    def prep(b):
        z = z_ref[b]
        other = log_1mlb + _log_sigmoid(z)
        log_f = jnp.maximum(log_lb, other) + jnp.log1p(jnp.exp(-jnp.abs(log_lb - other)))
        return dict(b=b, key=(1.0 - lb) * _sigmoid(-z), q=_silu(q_ref[b]) * HEAD_DIM ** -0.5, v=i_ref[b],
                    e_all=_dot_exact01(a16, log_f),
                    ltot=jnp.sum(log_f, axis=0, keepdims=True))

    def levels(pb, h):
        sl = slice(h * HEAD_DIM, (h + 1) * HEAD_DIM)
        q, k, e = pb["q"][:, sl], pb["key"][:, sl], pb["e_all"][:, sl]
        scores = pair_ref[0, 0] * _bdot_nt(q, k)
        for lv in range(N_LEVELS):
            z16 = (jnp.where(rowhi_ref[0, lv] > 0.5, q, k) * jnp.exp(e[(lv + 1) * CHUNK:(lv + 2) * CHUNK])).astype(BF16)
            scores = scores + pair_ref[0, lv + 1] * _dot_nt(z16, z16)
        return dict(b=pb["b"], h=h, sl=sl, q=q, k=k, v=pb["v"][:, sl], bh=e[:CHUNK], bl=pb["ltot"][:, sl],
                    scores=scores)

    def tail(c):
        st = st_ref[c["b"], c["h"]]
        o_ref[0, c["b"], :, c["sl"]] = _bdot(c["scores"], c["v"]) + _bdot_nt(c["q"] * jnp.exp(c["bh"]), st)
        st_ref[c["b"], c["h"]] = st * jnp.exp(c["bl"]) + _bdot_tn(c["v"], c["k"] * jnp.exp(c["bl"] - c["bh"]))

    pending = None
    pb = prep(0)
    for b in range(n_batch):
        pb_next = prep(b + 1) if b + 1 < n_batch else None
        for h in range(N_HEADS):
            cur = levels(pb, h)
            if pending is not None:
                tail(pending)
            pending = cur
        pb = pb_next
    tail(pending)


def _hgrn_scan(geo, p, lb):
    a_all, rowhi, pair = _hgrn_consts()
    p3 = p.reshape(geo.b, geo.n_tot, N_PROJ)
    return pl.pallas_call(
        functools.partial(_hgrn_scan_kernel, geo.b),
        grid=_scan_grid(geo),
        in_specs=[_chunk_spec(geo, MIX_W, COL_HG_Q // MIX_W), _chunk_spec(geo, MIX_W, COL_HG_I // MIX_W),
                  pl.BlockSpec((geo.b, CHUNK, MIX_W),
                               lambda d, j: (0, _chunk_index(geo, d, j), COL_HG_F // MIX_W + d)),
                  _per_dir_spec((1, MIX_W)), _per_dir_spec(((N_LEVELS + 1) * CHUNK, CHUNK)),
                  _per_dir_spec((N_LEVELS, CHUNK, HEAD_DIM)), _per_dir_spec((N_LEVELS + 1, CHUNK, CHUNK))],
        out_specs=_dir_chunk_spec(geo, MIX_W),
        out_shape=jax.ShapeDtypeStruct((2, geo.b, geo.n_tot, MIX_W), F32),
        scratch_shapes=[_state_scratch(geo)],
        compiler_params=_cparams(("parallel", "arbitrary")),
        name="hgrn_scan",
    )(p3, p3, p3, lb.reshape(2, 1, MIX_W), a_all, rowhi, pair)


def _ret_consts(n_lat):
    t = np.arange(CHUNK, dtype=np.float64)
    log_gamma = np.log1p(-np.exp2(-5.0 - np.arange(N_HEADS, dtype=np.float64)))
    dmat, qs, ks = [], [], []
    for rev in (False, True):
        te = t[::-1] if rev else t
        diff = te[:, None] - te[None, :]
        dmat.append(np.stack([np.where(diff >= 0, np.exp(lg * np.maximum(diff, 0)), 0.0) for lg in log_gamma]))
        qs.append(np.repeat(np.exp(log_gamma[None, :] * (te[:, None] + 1.0)), HEAD_DIM, axis=1))
        ks.append(np.repeat(np.exp(log_gamma[None, :] * (CHUNK - 1.0 - te[:, None])), HEAD_DIM, axis=1))
    g_chunk = np.repeat(np.exp(log_gamma * CHUNK), HEAD_DIM)[None, :]
    row = np.repeat(np.arange(n_lat // GRID_W, dtype=np.float32), GRID_W)
    colp = np.tile(np.arange(GRID_W, dtype=np.float32), n_lat // GRID_W)
    n_freq = HEAD_DIM // 4
    inv_freq = (np.float32(ROPE_BASE) ** (-np.arange(n_freq, dtype=np.float32) / n_freq)).astype(np.float32)
    ang = np.concatenate([row[:, None] * inv_freq, colp[:, None] * inv_freq], axis=-1).astype(np.float32)
    cos, sin = np.cos(ang), np.sin(ang)
    f32 = lambda a: jnp.asarray(np.asarray(a, np.float32))
    return (f32(np.stack(dmat)), f32(np.stack(qs)), f32(np.stack(ks)), f32(g_chunk),
            f32(np.concatenate([cos, cos], axis=-1)), f32(np.concatenate([-sin, sin], axis=-1)))


def _ret_scan_kernel(n_batch, nc_ctx, q_ref, k_ref, v_ref, cos_ref, sin_ref, dm_ref, qs_ref, ks_ref, gc_ref,
                     o_ref, s_ref):
    j = pl.program_id(1)

    @pl.when(j == 0)
    def _():
        s_ref[...] = jnp.zeros_like(s_ref)

    is_lat = j >= nc_ctx
    cos = jnp.where(is_lat, cos_ref[...], 1.0)
    sin = jnp.where(is_lat, sin_ref[...], 0.0)
    qs, ks, gc = qs_ref[0], ks_ref[0], gc_ref[...]
    chains = []
    for b in range(n_batch):
        q, k, v = q_ref[b], k_ref[b], v_ref[b]
        for h in range(N_HEADS):
            sl = slice(h * HEAD_DIM, (h + 1) * HEAD_DIM)
            qh, kh = q[:, sl], k[:, sl]
            rq = qh * cos + pltpu.roll(qh, HEAD_DIM // 2, 1) * sin
            rk = (kh * cos + pltpu.roll(kh, HEAD_DIM // 2, 1) * sin) * HEAD_DIM ** -0.5
            chains.append(dict(b=b, h=h, sl=sl, rq=rq, rk=rk, v=v[:, sl], s=s_ref[b, h]))
    for c in chains:
        c["scores"] = _bdot_nt(c["rq"], c["rk"]) * dm_ref[0, c["h"]]
    for c in chains:
        sl = c["sl"]
        o_ref[0, c["b"], :, sl] = _bdot(c["scores"], c["v"]) + _bdot(c["rq"] * qs[:, sl], c["s"])
        s_ref[c["b"], c["h"]] = c["s"] * gc[:, sl] + _bdot_tn(c["rk"] * ks[:, sl], c["v"])


def _ret_scan(geo, p):
    dmat, qs, ks, gc, cos2, sin2 = _ret_consts(geo.n_lat)
    nc_ctx = geo.n_ctx // CHUNK
    p3 = p.reshape(geo.b, geo.n_tot, N_PROJ)

    def rope_blk(d, j):
        return (jnp.where(j < nc_ctx, 0, _chunk_index(geo, d, j)), 0)

    return pl.pallas_call(
        functools.partial(_ret_scan_kernel, geo.b, nc_ctx),
        grid=_scan_grid(geo),
        in_specs=[_chunk_spec(geo, MIX_W, COL_RET_Q // MIX_W), _chunk_spec(geo, MIX_W, COL_RET_K // MIX_W),
                  _chunk_spec(geo, MIX_W, COL_RET_V // MIX_W),
                  pl.BlockSpec((CHUNK, HEAD_DIM), rope_blk), pl.BlockSpec((CHUNK, HEAD_DIM), rope_blk),
                  _per_dir_spec((N_HEADS, CHUNK, CHUNK)), _per_dir_spec((CHUNK, MIX_W)),
                  _per_dir_spec((CHUNK, MIX_W)), pl.BlockSpec((1, MIX_W), lambda d, j: (0, 0))],
        out_specs=_dir_chunk_spec(geo, MIX_W),
        out_shape=jax.ShapeDtypeStruct((2, geo.b, geo.n_tot, MIX_W), F32),
        scratch_shapes=[_state_scratch(geo)],
        compiler_params=_cparams(("parallel", "arbitrary")),
        name="ret_scan",
    )(p3, p3, p3, cos2, sin2, dmat, qs, ks, gc)


def _lru_kernel(geo, tr, rev, prev_ref, x_ref, next_ref, cw_ref, cb_ref, wg_ref, bg_ref, lam_ref,
                o_ref, h_ref):
    b, j = pl.program_id(0), pl.program_id(1)

    @pl.when(j == 0)
    def _():
        h_ref[...] = jnp.zeros_like(h_ref)

    first, last = geo.seq_edges(_lru_block(geo, tr, rev, b, j), tr)
    xb = _token_conv(prev_ref, x_ref, next_ref, cw_ref, first, last) + cb_ref[...]
    gates = _bdot(xb, wg_ref[...]) + bg_ref[...]
    gate_r = _sigmoid(gates[:, :MIX_W])
    gate_i = _sigmoid(gates[:, MIX_W:])
    log_a = LRU_C * gate_r * _log_sigmoid(lam_ref[...])
    a = jnp.exp(log_a)
    u = xb * gate_i * jnp.sqrt(jnp.maximum(-jnp.tanh(log_a) * (a * a + 1.0), SQRT_MIN))
    row = lax.broadcasted_iota(jnp.int32, (tr, MIX_W), 0)
    step = 1
    while step < tr:
        shift = tr - step if rev else step
        valid = (row < tr - step) if rev else (row >= step)
        u = jnp.where(valid, a * pltpu.roll(u, shift, 0) + u, u)
        a = jnp.where(valid, a * pltpu.roll(a, shift, 0), a)
        step *= 2
    h = a * h_ref[...] + u
    o_ref[...] = h
    h_ref[...] = h[0:1] if rev else h[tr - 1:tr]


def _lru_block(geo, tr, rev, b, j):
    nb_ctx, nb_lat = geo.n_ctx // tr, geo.n_lat // tr
    jc = nb_ctx - 1 - j if rev else j
    jl = nb_lat - 1 - (j - nb_ctx) if rev else j - nb_ctx
    return b * (geo.n_tot // tr) + jnp.where(j < nb_ctx, nb_lat + jc, jl)


def _lru_scan(geo, p, conv_w, conv_b, wa, ba, wx, bx, lam, rev):
    tr = _largest_tile(geo.row_tile, 256)
    d = 1 if rev else 0
    blockdiag = lambda w: jax.scipy.linalg.block_diag(*[w[n] for n in range(LRU_BLOCKS)])
    wg = jnp.concatenate([blockdiag(wa[d]), blockdiag(wx[d])], axis=1).astype(BF16)
    bg = jnp.concatenate([ba[d], bx[d]]).astype(F32).reshape(1, 2 * MIX_W)
    nb8, r8 = geo.m // SUBLANES, tr // SUBLANES
    cblk = COL_LRU_X // MIX_W
    blk = lambda b, j: _lru_block(geo, tr, rev, b, j)
    full = lambda shape: pl.BlockSpec(shape, lambda b, j: (0,) * len(shape))
    return pl.pallas_call(
        functools.partial(_lru_kernel, geo, tr, rev),
        grid=(geo.b, (geo.n_ctx + geo.n_lat) // tr),
        in_specs=[pl.BlockSpec((SUBLANES, MIX_W), lambda b, j: (jnp.maximum(blk(b, j) * r8 - 1, 0), cblk)),
                  pl.BlockSpec((tr, MIX_W), lambda b, j: (blk(b, j), cblk)),
                  pl.BlockSpec((SUBLANES, MIX_W),
                               lambda b, j: (jnp.minimum((blk(b, j) + 1) * r8, nb8 - 1), cblk)),
                  full((CONV_W, MIX_W)), full((1, MIX_W)), full((MIX_W, 2 * MIX_W)),
                  full((1, 2 * MIX_W)), full((1, MIX_W))],
        out_specs=pl.BlockSpec((tr, MIX_W), lambda b, j: (blk(b, j), 0)),
        out_shape=jax.ShapeDtypeStruct((geo.m, MIX_W), F32),
        scratch_shapes=[pltpu.VMEM((1, MIX_W), F32)],
        compiler_params=_cparams(("parallel", "arbitrary")),
        name="lru_scan_bwd" if rev else "lru_scan_fwd",
    )(p, p, p, conv_w.astype(F32), conv_b.astype(F32).reshape(1, MIX_W), wg, bg,
      lam[d].astype(F32).reshape(1, MIX_W))


def _gelu_tanh(x):
    return 0.5 * x * (1.0 + jnp.tanh(math.sqrt(2.0 / math.pi) * (x + 0.044715 * (x * x * x))))


def _gated_head_norm(o, g, z):
    parts = []
    for h in range(N_HEADS):
        oh = o[:, h * HEAD_DIM:(h + 1) * HEAD_DIM]
        parts.append(oh * lax.rsqrt(jnp.mean(oh * oh, axis=-1, keepdims=True) + EPS) * g)
    return jnp.concatenate(parts, axis=-1) * _silu(z)


def _merge_kernel(og_ref, oh_ref, olf_ref, olb_ref, or_ref, gz_ref, hg_ref, ly_ref, rg_ref, mg_ref,
                  x_ref, gt_ref, gng_ref, hng_ref, rng_ref, wb_ref, wo_ref, out_ref):
    branches = (
        _gated_head_norm(og_ref[0] + og_ref[1], gng_ref[...], gz_ref[...]),
        _gated_head_norm(oh_ref[0] + oh_ref[1], hng_ref[...], hg_ref[...]),
        (olf_ref[...] + olb_ref[...]) * _gelu_tanh(ly_ref[...]),
        _gated_head_norm(or_ref[0] + or_ref[1], rng_ref[...], rg_ref[...]),
    )
    mix = None
    for n, y in enumerate(branches):
        term = _sigmoid(mg_ref[:, n * D_MODEL:(n + 1) * D_MODEL]) * _bdot(y, wb_ref[n])
        mix = term if mix is None else mix + term
    out_ref[...] = x_ref[...] + gt_ref[0] * _bdot(mix, wo_ref[...])


def _merge(geo, compact, x_rows, p, o_gdn, o_hgrn, o_lru_f, o_lru_b, o_ret, mods3, gdn_g, hgrn_g, ret_g,
           w_branch, w_out):
    tm = _largest_tile(geo.row_tile, 256)
    n_rows = geo.m_lat if compact else geo.m
    src = lambda i: geo.full_tile(i, tm, compact)
    both = pl.BlockSpec((2, tm, MIX_W), lambda i: (0, src(i), 0))
    rows = lambda width: pl.BlockSpec((tm, width), lambda i: (src(i), 0))
    col = lambda c, width: pl.BlockSpec((tm, width), lambda i: (src(i), c // width))
    full = lambda shape: pl.BlockSpec(shape, lambda i: (0,) * len(shape))
    hd = lambda g: g.astype(F32).reshape(1, HEAD_DIM)
    return pl.pallas_call(
        _merge_kernel,
        grid=(n_rows // tm,),
        in_specs=[both, both, rows(MIX_W), rows(MIX_W), both,
                  col(COL_GDN_Z, MIX_W), col(COL_HG_G, MIX_W), col(COL_LRU_Y, MIX_W), col(COL_RET_G, MIX_W),
                  col(COL_MERGE, N_BRANCH * D_MODEL), rows(D_MODEL),
                  pl.BlockSpec((1, 1, D_MODEL), lambda i: (geo.mod_row(i, tm, compact), 0, 2)),
                  full((1, HEAD_DIM)), full((1, HEAD_DIM)), full((1, HEAD_DIM)),
                  full((N_BRANCH, MIX_W, D_MODEL)), full((D_MODEL, D_MODEL))],
        out_specs=pl.BlockSpec((tm, D_MODEL), lambda i: (i, 0)),
        out_shape=jax.ShapeDtypeStruct((n_rows, D_MODEL), F32),
        compiler_params=_cparams(("parallel",)),
        name="merge",
    )(o_gdn, o_hgrn, o_lru_f, o_lru_b, o_ret, p, p, p, p, p, x_rows, mods3, hd(gdn_g), hd(hgrn_g), hd(ret_g),
      w_branch.astype(BF16), w_out.astype(BF16))


ROW_TILE = D_MODEL // LANES
assert ROW_TILE == SUBLANES


def _store_row_tiles(ref, val):
    n = val.shape[0]
    for s in range(ROW_TILE):
        ref[pl.ds(s, n, stride=ROW_TILE), :] = val[:, s * LANES:(s + 1) * LANES]


def _load_row_tiles(ref_at, start, n):
    return jnp.concatenate([ref_at[pl.ds(start + s, n, stride=ROW_TILE), :] for s in range(ROW_TILE)], axis=1)


def _router_kernel(n_experts, x_ref, g_ref, sc_ref, sh_ref, rw_ref, rb_ref, f_ref, idx_ref, wt_ref):
    f = _rms_mod(x_ref[...], g_ref[...], sc_ref[0], sh_ref[0])
    _store_row_tiles(f_ref, f)
    logits = _dot(f, rw_ref[...], precision=HIGHEST) + rb_ref[...]
    lane = lax.broadcasted_iota(jnp.int32, logits.shape, 1)
    logits = jnp.where(lane < n_experts, logits, NEG_BIG)
    vals, idxs = [], []
    for _ in range(TOP_K):
        mx = jnp.max(logits, axis=-1, keepdims=True)
        ix = jnp.min(jnp.where(logits == mx, lane, LANES), axis=-1, keepdims=True)
        vals.append(mx)
        idxs.append(ix)
        logits = jnp.where(lane == ix, -jnp.inf, logits)
    ex = [jnp.exp(vl - vals[0]) for vl in vals]
    tot = ex[0] + ex[1] + ex[2] + ex[3]
    idx_out = jnp.zeros(lane.shape, jnp.int32)
    wt_out = jnp.zeros(lane.shape, F32)
    for kk in range(TOP_K):
        idx_out = jnp.where(lane == kk, idxs[kk], idx_out)
        wt_out = jnp.where(lane == kk, ex[kk] / tot, wt_out)
    idx_ref[...] = idx_out
    wt_ref[...] = wt_out


def _router(geo, compact, x_rows, g, mods3, router_w, router_b):
    n_rows, d = x_rows.shape
    n_experts = router_w.shape[1]
    tm = _largest_tile(geo.row_tile, 256)
    rw = jnp.pad(router_w.astype(F32), ((0, 0), (0, LANES - n_experts)))
    rb = jnp.pad(router_b.astype(F32), (0, LANES - n_experts)).reshape(1, LANES)
    rows = lambda width: pl.BlockSpec((tm, width), lambda i: (i, 0))
    full = lambda shape: pl.BlockSpec(shape, lambda i: (0,) * len(shape))
    return pl.pallas_call(
        functools.partial(_router_kernel, n_experts),
        grid=(n_rows // tm,),
        in_specs=[rows(d), full((1, d)),
                  pl.BlockSpec((1, 1, d), lambda i: (geo.mod_row(i, tm, compact), 0, 4)),
                  pl.BlockSpec((1, 1, d), lambda i: (geo.mod_row(i, tm, compact), 0, 3)),
                  full((d, LANES)), full((1, LANES))],
        out_specs=[pl.BlockSpec((tm * ROW_TILE, LANES), lambda i: (i, 0)), rows(LANES), rows(LANES)],
        out_shape=[jax.ShapeDtypeStruct((n_rows * ROW_TILE, LANES), F32),
                   jax.ShapeDtypeStruct((n_rows, LANES), jnp.int32), jax.ShapeDtypeStruct((n_rows, LANES), F32)],
        compiler_params=_cparams(("parallel",)),
        name="router",
    )(x_rows, g.astype(F32).reshape(1, d), mods3, mods3, rw, rb)


def _route_plan(top_idx, n_experts):
    m = top_idx.shape[0]
    n_assign = m * TOP_K
    flat_e = top_idx.reshape(-1)
    order = jnp.argsort(flat_e).astype(jnp.int32)
    rank = jnp.argsort(order).astype(jnp.int32)
    counts = jnp.sum(flat_e[:, None] == jnp.arange(n_experts, dtype=jnp.int32)[None, :], axis=0, dtype=jnp.int32)
    padded = (counts + ROUTE_BLOCK - 1) // ROUTE_BLOCK * ROUTE_BLOCK
    pad_end = jnp.cumsum(padded)
    pad_start = pad_end - padded
    start = jnp.cumsum(counts) - counts
    n_blocks = -(-n_assign // ROUTE_BLOCK) + n_experts
    cap = n_blocks * ROUTE_BLOCK
    block_row0 = jnp.arange(n_blocks, dtype=jnp.int32) * ROUTE_BLOCK
    block_expert = jnp.minimum(jnp.sum(pad_end[None, :] <= block_row0[:, None], axis=1, dtype=jnp.int32),
                               n_experts - 1)
    n_used = (pad_end[-1:] // ROUTE_BLOCK).astype(jnp.int32)
    row = jnp.arange(cap, dtype=jnp.int32)
    row_e = jnp.repeat(block_expert, ROUTE_BLOCK)
    within = row - pad_start[row_e]
    valid = jnp.logical_and(within >= 0, within < counts[row_e])
    src_tok = jnp.where(valid, order[jnp.clip(start[row_e] + within, 0, n_assign - 1)] // TOP_K, 0)
    dest = pad_start[flat_e] + rank - start[flat_e]
    return src_tok.reshape(n_blocks, 1, ROUTE_BLOCK), block_expert, n_used, dest


def _start_row_gather(idx_ref, n, src_hbm, buf, slot, sem):
    for r in range(n):
        src_row = pl.multiple_of(idx_ref[0, 0, r] * ROW_TILE, ROW_TILE)
        pltpu.make_async_copy(src_hbm.at[pl.ds(src_row, ROW_TILE)], buf.at[slot, pl.ds(r * ROW_TILE, ROW_TILE)],
                              sem.at[slot]).start()


def _wait_row_gather(n, src_hbm, buf, slot, sem):
    pltpu.make_async_copy(src_hbm.at[pl.ds(0, n * ROW_TILE)], buf.at[slot], sem.at[slot]).wait()


def _gather_step(cur_ref, nxt_ref, n, src_hbm, buf, sem):
    i, nb = pl.program_id(0), pl.num_programs(0)
    slot = i % 2

    @pl.when(i == 0)
    def _():
        _start_row_gather(cur_ref, n, src_hbm, buf, 0, sem)

    _wait_row_gather(n, src_hbm, buf, slot, sem)
    _start_row_gather(nxt_ref, n, src_hbm, buf, 1 - slot, sem)
    return slot


def _gather_drain(n, src_hbm, buf, slot, sem):
    @pl.when(pl.program_id(0) == pl.num_programs(0) - 1)
    def _():
        _wait_row_gather(n, src_hbm, buf, 1 - slot, sem)


def _expert_kernel(be_ref, nu_ref, src_ref, nxt_ref, f_hbm, w1_ref, b1_ref, w2_ref, b2_ref, y_ref,
                   buf, sem, w1b_ref, w2b_ref):
    i = pl.program_id(0)
    ff = w2_ref.shape[1]
    slot = _gather_step(src_ref, nxt_ref, ROUTE_BLOCK, f_hbm, buf, sem)

    @pl.when(jnp.logical_or(i == 0, be_ref[i] != be_ref[jnp.maximum(i - 1, 0)]))
    def _():
        w1b_ref[...] = w1_ref[0].astype(BF16)
        w2b_ref[...] = w2_ref[0].astype(BF16)

    @pl.when(i >= nu_ref[0])
    def _():
        y_ref[...] = jnp.zeros_like(y_ref)

    @pl.when(i < nu_ref[0])
    def _():
        z = _bdot(_load_row_tiles(buf.at[slot], 0, ROUTE_BLOCK), w1b_ref[...]) + b1_ref[0]
        glu = jnp.minimum(z[:, :ff], SWIGLU_LIMIT)
        lin = jnp.clip(z[:, ff:], -SWIGLU_LIMIT, SWIGLU_LIMIT)
        act = glu * _sigmoid(SWIGLU_ALPHA * glu) * (lin + 1.0)
        _store_row_tiles(y_ref, _bdot(act, w2b_ref[...]) + b2_ref[0])

    _gather_drain(ROUTE_BLOCK, f_hbm, buf, slot, sem)


def _experts(f_tiles, src_tok, block_expert, n_used, w1, b1, w2, b2):
    n_blocks = src_tok.shape[0]
    n_experts, d, ff2 = w1.shape
    ff = w2.shape[1]
    grid_spec = pltpu.PrefetchScalarGridSpec(
        num_scalar_prefetch=2,
        grid=(n_blocks,),
        in_specs=[pl.BlockSpec((1, 1, ROUTE_BLOCK), lambda i, be, nu: (i, 0, 0), memory_space=pltpu.SMEM),
                  pl.BlockSpec((1, 1, ROUTE_BLOCK), lambda i, be, nu: (jnp.minimum(i + 1, n_blocks - 1), 0, 0),
                               memory_space=pltpu.SMEM),
                  pl.BlockSpec(memory_space=pl.ANY),
                  pl.BlockSpec((1, d, ff2), lambda i, be, nu: (be[i], 0, 0)),
                  pl.BlockSpec((1, 1, ff2), lambda i, be, nu: (be[i], 0, 0)),
                  pl.BlockSpec((1, ff, d), lambda i, be, nu: (be[i], 0, 0)),
                  pl.BlockSpec((1, 1, d), lambda i, be, nu: (be[i], 0, 0))],
        out_specs=pl.BlockSpec((ROUTE_BLOCK * ROW_TILE, LANES), lambda i, be, nu: (i, 0)),
        scratch_shapes=[pltpu.VMEM((2, ROUTE_BLOCK * ROW_TILE, LANES), F32), pltpu.SemaphoreType.DMA((2,)),
                        pltpu.VMEM((d, ff2), BF16), pltpu.VMEM((ff, d), BF16)],
    )
    return pl.pallas_call(
        _expert_kernel,
        grid_spec=grid_spec,
        out_shape=jax.ShapeDtypeStruct((n_blocks * ROUTE_BLOCK * ROW_TILE, LANES), F32),
        compiler_params=_cparams(("arbitrary",)),
        name="experts",
    )(block_expert, n_used, src_tok, src_tok, f_tiles, w1, b1.astype(F32).reshape(n_experts, 1, ff2),
      w2, b2.astype(F32).reshape(n_experts, 1, d))


def _combine_kernel(tc, final, dst_ref, nxt_ref, y_hbm, x_ref, wt_ref, gt_ref, fg_ref, out_ref, buf, sem):
    n = TOP_K * tc
    slot = _gather_step(dst_ref, nxt_ref, n, y_hbm, buf, sem)
    wt = wt_ref[...]
    acc = None
    for kk in range(TOP_K):
        term = wt[:, kk:kk + 1] * _load_row_tiles(buf.at[slot], kk * tc * ROW_TILE, tc)
        acc = term if acc is None else acc + term
    out = x_ref[...] + gt_ref[0] * acc
    if final:
        out = out * lax.rsqrt(jnp.mean(out * out, axis=-1, keepdims=True) + EPS) * fg_ref[...]
    out_ref[...] = out
    _gather_drain(n, y_hbm, buf, slot, sem)


def _combine(geo, y_pad, dest, x_rows, wt, mods3, final_g, final):
    n_rows, d = x_rows.shape
    tc = _largest_tile(geo.row_tile, 128)
    nt = n_rows // tc
    dst = dest.reshape(nt, tc, TOP_K).transpose(0, 2, 1).reshape(nt, 1, TOP_K * tc)
    rows = lambda width: pl.BlockSpec((tc, width), lambda i: (i, 0))
    return pl.pallas_call(
        functools.partial(_combine_kernel, tc, final),
        grid=(nt,),
        in_specs=[pl.BlockSpec((1, 1, TOP_K * tc), lambda i: (i, 0, 0), memory_space=pltpu.SMEM),
                  pl.BlockSpec((1, 1, TOP_K * tc), lambda i: (jnp.minimum(i + 1, nt - 1), 0, 0),
                               memory_space=pltpu.SMEM),
                  pl.BlockSpec(memory_space=pl.ANY), rows(d), rows(LANES),
                  pl.BlockSpec((1, 1, d), lambda i: (geo.mod_row(i, tc, final), 0, 5)),
                  pl.BlockSpec((1, d), lambda i: (0, 0))],
        out_specs=rows(d),
        out_shape=jax.ShapeDtypeStruct((n_rows, d), F32),
        scratch_shapes=[pltpu.VMEM((2, TOP_K * tc * ROW_TILE, LANES), F32), pltpu.SemaphoreType.DMA((2,))],
        compiler_params=_cparams(("arbitrary",)),
        name="moe_combine",
    )(dst, dst, y_pad, x_rows, wt, mods3, final_g.astype(F32).reshape(1, d))


def _moe(geo, x_rows, g, mods3, router_w, router_b, w1, b1, w2, b2, final_g, final):
    f_tiles, top_idx, top_w = _router(geo, final, x_rows, g, mods3, router_w, router_b)
    src_tok, block_expert, n_used, dest = _route_plan(top_idx[:, :TOP_K], router_w.shape[1])
    y_tiles = _experts(f_tiles, src_tok, block_expert, n_used, w1, b1, w2, b2)
    return _combine(geo, y_tiles, dest, x_rows, top_w, mods3, final_g, final)


def kernel(x, c, ctx, c_ctx, w_mod, b_mod, norm1_g, norm2_g, w_in, gdn_conv_w, gdn_a_log, gdn_dt_bias, gdn_norm_g, hgrn_lb, hgrn_norm_g, lru_conv_w, lru_conv_b, lru_wa, lru_ba, lru_wx, lru_bx, lru_lambda, ret_norm_g, w_branch, w_out, router_w, router_b, moe_w1, moe_b1, moe_w2, moe_b2, final_norm_g):
    b, n_lat, d = x.shape
    n_ctx = ctx.shape[1]
    depth = w_mod.shape[0]
    assert d == D_MODEL and w_branch.shape[2] == MIX_W
    geo = _Geo(b, n_lat, n_ctx)
    lb_soft = jax.nn.softmax(hgrn_lb.astype(F32), axis=0)
    lower_bounds = jnp.clip(jnp.cumsum(lb_soft, axis=0) - lb_soft[0], 0.0, LB_MAX)
    n_cond = -(-(b + 1) // SUBLANES) * SUBLANES
    cc = jnp.concatenate([c, c_ctx[None, :], jnp.zeros((n_cond - b - 1, d), c.dtype)], axis=0)
    mods = _modulation(cc.astype(F32), w_mod, b_mod)
    rows = jnp.concatenate([x, ctx], axis=1).reshape(geo.m, d)
    flat = lambda o: o.reshape(2, geo.m, MIX_W)
    for l in range(depth):
        last = l == depth - 1
        mods3 = mods[l].reshape(n_cond, 1, N_MOD * d)
        p = _norm_proj(geo, rows, norm1_g[l], mods3, _reorder_w_in(w_in[l]))
        gq, gk, gv, gb = _gdn_prep(geo, p, gdn_conv_w[l], gdn_a_log[l], gdn_dt_bias[l])
        o_gdn = flat(_gdn_scan(geo, gq, gk, gv, gb))
        o_hgrn = flat(_hgrn_scan(geo, p, lower_bounds[l]))
        lru_args = (geo, p, lru_conv_w[l], lru_conv_b[l], lru_wa[l], lru_ba[l], lru_wx[l], lru_bx[l],
                    lru_lambda[l])
        o_lru_f = _lru_scan(*lru_args, rev=False)
        o_lru_b = _lru_scan(*lru_args, rev=True)
        o_ret = flat(_ret_scan(geo, p))
        rows = _merge(geo, last, rows, p, o_gdn, o_hgrn, o_lru_f, o_lru_b, o_ret, mods3, gdn_norm_g[l],
                      hgrn_norm_g[l], ret_norm_g[l], w_branch[l], w_out[l])
        rows = _moe(geo, rows, norm2_g[l], mods3, router_w[l], router_b[l], moe_w1[l], moe_b1[l],
                    moe_w2[l], moe_b2[l], final_norm_g, last)
    return rows.reshape(b, n_lat, d)
```

```python
import functools
import math

import numpy as np
import jax
import jax.numpy as jnp
from jax import lax
from jax.experimental import pallas as pl
from jax.experimental.pallas import tpu as pltpu

F32 = jnp.float32
BF16 = jnp.bfloat16
HIGHEST = lax.Precision.HIGHEST

GRID_W = 64
N_BRANCH = 4
HEAD_DIM = 128
LRU_BLOCKS = 8
LRU_C = 8.0
CONV_W = 4
CHUNK = 64
ROPE_BASE = 10000.0
TOP_K = 4
SWIGLU_ALPHA = 1.702
SWIGLU_LIMIT = 7.0
N_MOD = 6
EPS = 1e-6
LB_MIN = 1e-30
LB_MAX = 1.0 - 1e-4
SQRT_MIN = 1e-12

LANES = 128
SUBLANES = 8
VMEM_LIMIT_BYTES = 56 * 1024 * 1024
ROUTE_BLOCK = 256
NEG_BIG = -1e30


def _cparams(semantics):
    return pltpu.CompilerParams(dimension_semantics=semantics, vmem_limit_bytes=VMEM_LIMIT_BYTES)


def _largest_tile(n, cap, quantum=SUBLANES):
    best = None
    for t in range(quantum, min(n, cap) + 1, quantum):
        if n % t == 0:
            best = t
    assert best is not None, (n, cap)
    return best


def _dot(a, b, precision=None):
    return jnp.dot(a, b, preferred_element_type=F32, precision=precision)


def _dot_nt(a, b):
    return lax.dot_general(a, b, (((1,), (1,)), ((), ())), preferred_element_type=F32)


def _dot_tn(a, b):
    return lax.dot_general(a, b, (((0,), (0,)), ((), ())), preferred_element_type=F32)


def _bdot(a, b):
    return _dot(a.astype(BF16), b.astype(BF16))


def _bdot_nt(a, b):
    return _dot_nt(a.astype(BF16), b.astype(BF16))


def _bdot_tn(a, b):
    return _dot_tn(a.astype(BF16), b.astype(BF16))


def _dot_exact01(a01x3, x):
    hi = x.astype(BF16)
    r1 = x - hi.astype(F32)
    mid = r1.astype(BF16)
    lo = (r1 - mid.astype(F32)).astype(BF16)
    return _dot(a01x3, jnp.concatenate([hi, mid, lo], axis=0))


def _sigmoid(x):
    return 1.0 / (1.0 + jnp.exp(-x))


def _silu(x):
    return x * _sigmoid(x)


def _softplus(x):
    return jnp.maximum(x, 0.0) + jnp.log1p(jnp.exp(-jnp.abs(x)))


def _log_sigmoid(x):
    return -_softplus(-x)


def _mod_kernel(c_ref, w_ref, b_ref, o_ref):
    o_ref[0] = _dot(_silu(c_ref[...]), w_ref[0], precision=HIGHEST) + b_ref[0]


def _modulation(cc, w_mod, b_mod):
    n_layers, d, n = w_mod.shape
    rows = cc.shape[0]
    tn = _largest_tile(n, 1024, LANES)
    return pl.pallas_call(
        _mod_kernel,
        grid=(n_layers, n // tn),
        in_specs=[pl.BlockSpec((rows, d), lambda l, j: (0, 0)),
                  pl.BlockSpec((1, d, tn), lambda l, j: (l, 0, j)),
                  pl.BlockSpec((1, 1, tn), lambda l, j: (l, 0, j))],
        out_specs=pl.BlockSpec((1, rows, tn), lambda l, j: (l, 0, j)),
        out_shape=jax.ShapeDtypeStruct((n_layers, rows, n), F32),
        compiler_params=_cparams(("parallel", "parallel")),
        name="modulation",
    )(cc, w_mod, b_mod.reshape(n_layers, 1, n))


D_MODEL = 1024
MIX_W = 512
N_HEADS = MIX_W // HEAD_DIM
COL_MERGE = 0
COL_GDN_Z = 4096
COL_GDN_QKV = 4608
COL_HG_Q = 6144
COL_HG_I = 6656
COL_HG_F = 7168
COL_HG_G = 8192
COL_LRU_X = 8704
COL_LRU_Y = 9216
COL_RET_Q = 9728
COL_RET_K = 10240
COL_RET_V = 10752
COL_RET_G = 11264
COL_AB = 11776
N_PROJ = 12288


class _Geo:
    def __init__(self, b, n_lat, n_ctx):
        self.b, self.n_lat, self.n_ctx = b, n_lat, n_ctx
        self.n_tot = n_lat + n_ctx
        self.m_lat = b * n_lat
        self.m = b * self.n_tot
        assert n_lat % CHUNK == 0 and n_ctx % CHUNK == 0 and n_lat % GRID_W == 0
        self.row_tile = math.gcd(n_lat, n_ctx)

    def mod_row(self, i, tm, compact=False):
        r0 = i * tm
        if compact:
            return r0 // self.n_lat
        return jnp.where(r0 % self.n_tot < self.n_lat, r0 // self.n_tot, self.b)

    def full_tile(self, i, tm, compact=False):
        if not compact:
            return i
        per_lat, per_tot = self.n_lat // tm, self.n_tot // tm
        return (i // per_lat) * per_tot + i % per_lat

    def seq_edges(self, i, tr):
        pos = (i * tr) % self.n_tot
        first = jnp.logical_or(pos == 0, pos == self.n_lat)
        last = jnp.logical_or(pos + tr == self.n_lat, pos + tr == self.n_tot)
        return first, last


def _reorder_w_in(w_in):
    w = w_in.astype(BF16)
    d = w.shape[0]
    pad = lambda n: jnp.zeros((d, n), BF16)
    ab = [jnp.concatenate([w[:, 2048 + 4 * dr:2052 + 4 * dr], w[:, 2056 + 4 * dr:2060 + 4 * dr],
                           pad(LANES - 8)], axis=1) for dr in range(2)]
    out = jnp.concatenate([w[:, 7696:11792], w[:, 1536:2048], w[:, 0:1536], w[:, 2064:7696],
                           ab[0], ab[1], pad(N_PROJ - COL_AB - 2 * LANES)], axis=1)
    assert out.shape[1] == N_PROJ
    return out


def _rms_mod(x, g, sc, sh):
    y = x * lax.rsqrt(jnp.mean(x * x, axis=-1, keepdims=True) + EPS) * g
    return y * (1.0 + sc) + sh


def _norm_kernel(x_ref, g_ref, sc_ref, sh_ref, h_ref):
    h_ref[...] = _rms_mod(x_ref[...], g_ref[...], sc_ref[0], sh_ref[0]).astype(BF16)


def _proj_kernel(h_ref, w_ref, o_ref):
    o_ref[...] = _dot(h_ref[...], w_ref[...])


def _norm_proj(geo, x_rows, g, mods3, w_p):
    m, d = x_rows.shape
    n = w_p.shape[1]
    tr = _largest_tile(geo.row_tile, 512)
    h = pl.pallas_call(
        _norm_kernel,
        grid=(m // tr,),
        in_specs=[pl.BlockSpec((tr, d), lambda i: (i, 0)),
                  pl.BlockSpec((1, d), lambda i: (0, 0)),
                  pl.BlockSpec((1, 1, d), lambda i: (geo.mod_row(i, tr), 0, 1)),
                  pl.BlockSpec((1, 1, d), lambda i: (geo.mod_row(i, tr), 0, 0))],
        out_specs=pl.BlockSpec((tr, d), lambda i: (i, 0)),
        out_shape=jax.ShapeDtypeStruct((m, d), BF16),
        compiler_params=_cparams(("parallel",)),
        name="norm1",
    )(x_rows, g.reshape(1, d), mods3, mods3)
    tm = _largest_tile(m, 1024)
    tn = _largest_tile(n, 2048, LANES)
    return pl.pallas_call(
        _proj_kernel,
        grid=(n // tn, m // tm),
        in_specs=[pl.BlockSpec((tm, d), lambda j, i: (i, 0)),
                  pl.BlockSpec((d, tn), lambda j, i: (0, j))],
        out_specs=pl.BlockSpec((tm, tn), lambda j, i: (i, j)),
        out_shape=jax.ShapeDtypeStruct((m, n), F32),
        compiler_params=_cparams(("parallel", "parallel")),
        name="in_proj",
    )(h, w_p)


def _halo_specs(geo, tr, width, col_block):
    nb8 = geo.m // SUBLANES
    r8 = tr // SUBLANES
    return [pl.BlockSpec((SUBLANES, width), lambda i: (jnp.maximum(i * r8 - 1, 0), col_block)),
            pl.BlockSpec((tr, width), lambda i: (i, col_block)),
            pl.BlockSpec((SUBLANES, width), lambda i: (jnp.minimum((i + 1) * r8, nb8 - 1), col_block))]


def _token_conv(prev_ref, x_ref, next_ref, w_ref, first, last):
    x = x_ref[...]
    tr = x.shape[0]
    prev = jnp.where(first, 0.0, prev_ref[...])
    nxt = jnp.where(last, 0.0, next_ref[...])
    ext = jnp.concatenate([prev, x, nxt], axis=0)
    w = w_ref[...]
    y = None
    for j in range(CONV_W):
        off = SUBLANES - CONV_W // 2 + j
        term = ext[off:off + tr] * w[j:j + 1]
        y = term if y is None else y + term
    return y


def _gdn_prep_kernel(geo, tr, prev_ref, x_ref, next_ref, ab0_ref, ab1_ref, cw_ref, alog_ref, dtb_ref,
                     q_ref, k_ref, v_ref, gb_ref):
    first, last = geo.seq_edges(pl.program_id(0), tr)
    s = _silu(_token_conv(prev_ref, x_ref, next_ref, cw_ref, first, last))
    for h in range(N_HEADS):
        sl = slice(h * HEAD_DIM, (h + 1) * HEAD_DIM)
        qh = s[:, sl]
        kh = s[:, MIX_W + h * HEAD_DIM:MIX_W + (h + 1) * HEAD_DIM]
        q_ref[:, sl] = qh * (lax.rsqrt(jnp.sum(qh * qh, axis=-1, keepdims=True) + EPS) * HEAD_DIM ** -0.5)
        k_ref[:, sl] = kh * lax.rsqrt(jnp.sum(kh * kh, axis=-1, keepdims=True) + EPS)
    v_ref[...] = s[:, 2 * MIX_W:]
    lane = lax.broadcasted_iota(jnp.int32, (tr, LANES), 1)
    for dr, ab_ref in enumerate((ab0_ref, ab1_ref)):
        a = ab_ref[...]
        g = -jnp.exp(alog_ref[dr]) * _softplus(a + dtb_ref[dr])
        gb_ref[dr] = jnp.where(lane < N_HEADS, g, _sigmoid(a))


def _gdn_prep(geo, p, conv_w, a_log, dt_bias):
    tr = _largest_tile(math.gcd(geo.n_lat, geo.n_ctx), 256)
    m = geo.m
    lane_pad = lambda v: jnp.pad(v.astype(F32), ((0, 0), (0, LANES - N_HEADS))).reshape(2, 1, LANES)
    ab_blk = COL_AB // LANES
    rows = lambda width: pl.BlockSpec((tr, width), lambda i: (i, 0))
    full3 = pl.BlockSpec((2, 1, LANES), lambda i: (0, 0, 0))
    return pl.pallas_call(
        functools.partial(_gdn_prep_kernel, geo, tr),
        grid=(m // tr,),
        in_specs=_halo_specs(geo, tr, 3 * MIX_W, COL_GDN_QKV // (3 * MIX_W)) + [
            pl.BlockSpec((tr, LANES), lambda i: (i, ab_blk)),
            pl.BlockSpec((tr, LANES), lambda i: (i, ab_blk + 1)),
            pl.BlockSpec((CONV_W, 3 * MIX_W), lambda i: (0, 0)), full3, full3],
        out_specs=[rows(MIX_W), rows(MIX_W), rows(MIX_W),
                   pl.BlockSpec((2, tr, LANES), lambda i: (0, i, 0))],
        out_shape=[jax.ShapeDtypeStruct((m, MIX_W), F32)] * 3 + [jax.ShapeDtypeStruct((2, m, LANES), F32)],
        compiler_params=_cparams(("parallel",)),
        name="gdn_prep",
    )(p, p, p, p, p, conv_w.astype(F32), lane_pad(a_log), lane_pad(dt_bias))


def _chunk_index(geo, d, j):
    nc_ctx, nc_lat = geo.n_ctx // CHUNK, geo.n_lat // CHUNK
    jc = jnp.where(d == 0, j, nc_ctx - 1 - j)
    jl = jnp.where(d == 0, j - nc_ctx, nc_lat - 1 - (j - nc_ctx))
    return jnp.where(j < nc_ctx, nc_lat + jc, jl)


def _scan_grid(geo):
    return (2, geo.n_tot // CHUNK)


def _chunk_spec(geo, width, col_block=0):
    return pl.BlockSpec((geo.b, CHUNK, width), lambda d, j: (0, _chunk_index(geo, d, j), col_block))


def _dir_chunk_spec(geo, width):
    return pl.BlockSpec((1, geo.b, CHUNK, width), lambda d, j: (d, 0, _chunk_index(geo, d, j), 0))


def _per_dir_spec(shape):
    return pl.BlockSpec((1,) + shape, lambda d, j: (d,) + (0,) * len(shape))


N_LEVELS = int(math.log2(CHUNK))


def _chunk_consts():
    t = np.arange(CHUNK)
    tt, uu = t[:, None], t[None, :]
    tri = uu <= tt
    pair = []
    for lv in range(N_LEVELS):
        hi = ((t >> lv) & 1) == 1
        pair.append(((uu >> (lv + 1)) == (tt >> (lv + 1))) & hi[:, None] & (~hi)[None, :])
    pair = np.stack(pair)
    both = lambda a: np.stack([a, a[..., ::-1, ::-1]]).astype(np.float32)
    return jnp.asarray(both(tri)), jnp.asarray(both(pair))


def _gdn_scan_kernel(n_batch, q_ref, k_ref, v_ref, gb_ref, tri_ref, pair_ref, o_ref, s_ref):
    @pl.when(pl.program_id(1) == 0)
    def _():
        s_ref[...] = jnp.zeros_like(s_ref)

    tri = tri_ref[0]
    row = lax.broadcasted_iota(jnp.int32, (CHUNK, CHUNK), 0)
    col = lax.broadcasted_iota(jnp.int32, (CHUNK, CHUNK), 1)
    incl = tri > 0.5
    eye = (row == col).astype(F32)
    chains = []
    for b in range(n_batch):
        gb = gb_ref[0, b]
        gcum = _dot(tri, gb, precision=HIGHEST)
        gcum_t = gcum.T
        gtot = jnp.sum(gb, axis=0, keepdims=True)
        q, k, v = q_ref[b], k_ref[b], v_ref[b]
        for h in range(N_HEADS):
            sl = slice(h * HEAD_DIM, (h + 1) * HEAD_DIM)
            chains.append(dict(b=b, h=h, sl=sl, bc=gcum[:, h:h + 1], br=gcum_t[h:h + 1, :],
                               beta=gb[:, N_HEADS + h:N_HEADS + h + 1], bl=gtot[:, h:h + 1],
                               q=q[:, sl], k=k[:, sl], v=v[:, sl], s=s_ref[b, h]))
    for c in chains:
        c["decay"] = jnp.where(incl, jnp.exp(jnp.where(incl, c["bc"] - c["br"], 0.0)), 0.0)
        c["kbeta"] = c["k"] * c["beta"]
        c["k16"] = c["k"].astype(BF16)
        c["mm"] = _dot_nt(c["kbeta"].astype(BF16), c["k16"]) * c["decay"]
        c["dinv"] = eye - pair_ref[0, 0] * c["mm"]
    for lv in range(1, N_LEVELS):
        for c in chains:
            c["ld"] = _bdot(pair_ref[0, lv] * c["mm"], c["dinv"])
        for c in chains:
            c["dinv"] = c["dinv"] - _bdot(c["dinv"], c["ld"])
    for c in chains:
        c["eb"] = jnp.exp(c["bc"])
        rhs = jnp.concatenate([c["v"] * c["beta"], c["kbeta"] * c["eb"]], axis=-1)
        c["sol"] = rhs + _bdot(c["dinv"] - eye, rhs)
        c["attn"] = _dot_nt(c["q"].astype(BF16), c["k16"]) * c["decay"]
    for c in chains:
        c["v_new"] = c["sol"][:, :HEAD_DIM] - _bdot(c["sol"][:, HEAD_DIM:], c["s"])
    for c in chains:
        o_ref[0, c["b"], :, c["sl"]] = _bdot(c["q"] * c["eb"], c["s"]) + _bdot(c["attn"], c["v_new"])
        s_ref[c["b"], c["h"]] = (c["s"] * jnp.exp(c["bl"])
                                 + _bdot_tn(c["k"] * jnp.exp(c["bl"] - c["bc"]), c["v_new"]))


def _state_scratch(geo):
    return pltpu.VMEM((geo.b, N_HEADS, HEAD_DIM, HEAD_DIM), F32)


def _gdn_scan(geo, q, k, v, gb):
    tri, pair = _chunk_consts()
    view = lambda a: a.reshape(geo.b, geo.n_tot, a.shape[-1])
    return pl.pallas_call(
        functools.partial(_gdn_scan_kernel, geo.b),
        grid=_scan_grid(geo),
        in_specs=[_chunk_spec(geo, MIX_W), _chunk_spec(geo, MIX_W), _chunk_spec(geo, MIX_W),
                  _dir_chunk_spec(geo, LANES), _per_dir_spec((CHUNK, CHUNK)),
                  _per_dir_spec((N_LEVELS, CHUNK, CHUNK))],
        out_specs=_dir_chunk_spec(geo, MIX_W),
        out_shape=jax.ShapeDtypeStruct((2, geo.b, geo.n_tot, MIX_W), F32),
        scratch_shapes=[_state_scratch(geo)],
        compiler_params=_cparams(("parallel", "arbitrary")),
        name="gdn_scan",
    )(view(q), view(k), view(v), gb.reshape(2, geo.b, geo.n_tot, LANES), tri, pair)


def _hgrn_consts():
    t = np.arange(CHUNK)
    tt, uu = t[:, None], t[None, :]
    a_all, rowhi, pair = [(uu <= tt)], [], [(uu == tt)]
    for lv in range(N_LEVELS):
        hi = ((t >> lv) & 1) == 1
        same_half = (uu >> lv) == (tt >> lv)
        a_all.append(same_half & np.where(hi[:, None], uu <= tt, uu > tt))
        rowhi.append(np.broadcast_to(hi[:, None], (CHUNK, HEAD_DIM)))
        pair.append(((uu >> (lv + 1)) == (tt >> (lv + 1))) & hi[:, None] & (~hi)[None, :])
    mirror = lambda a: a[::-1, ::-1] if a.shape[1] == CHUNK else a[::-1]
    stack = lambda xs: np.stack([np.stack(xs), np.stack([mirror(x) for x in xs])]).astype(np.float32)
    a = stack(a_all)
    return (jnp.asarray(a.reshape(2, (N_LEVELS + 1) * CHUNK, CHUNK)), jnp.asarray(stack(rowhi)),
            jnp.asarray(stack(pair)))


def _hgrn_scan_kernel(n_batch, q_ref, i_ref, z_ref, lb_ref, a_ref, rowhi_ref, pair_ref, o_ref, st_ref):
    @pl.when(pl.program_id(1) == 0)
    def _():
        st_ref[...] = jnp.zeros_like(st_ref)

    lb = lb_ref[0]
    log_lb = jnp.log(jnp.maximum(lb, LB_MIN))
    log_1mlb = jnp.log1p(-lb)
    a16 = a_ref[0].astype(BF16)
    a16 = jnp.concatenate([a16, a16, a16], axis=1)

    def prep(b):
        z = z_ref[b]
        other = log_1mlb + _log_sigmoid(z)
        log_f = jnp.maximum(log_lb, other) + jnp.log1p(jnp.exp(-jnp.abs(log_lb - other)))
        return dict(b=b, key=(1.0 - lb) * _sigmoid(-z), q=_silu(q_ref[b]) * HEAD_DIM ** -0.5, v=i_ref[b],
                    e_all=_dot_exact01(a16, log_f),
                    ltot=jnp.sum(log_f, axis=0, keepdims=True))

    def levels(pb, h):
        sl = slice(h * HEAD_DIM, (h + 1) * HEAD_DIM)
        q, k, e = pb["q"][:, sl], pb["key"][:, sl], pb["e_all"][:, sl]
        scores = pair_ref[0, 0] * _bdot_nt(q, k)
        for lv in range(N_LEVELS):
            z16 = (jnp.where(rowhi_ref[0, lv] > 0.5, q, k) * jnp.exp(e[(lv + 1) * CHUNK:(lv + 2) * CHUNK])).astype(BF16)
            scores = scores + pair_ref[0, lv + 1] * _dot_nt(z16, z16)
        return dict(b=pb["b"], h=h, sl=sl, q=q, k=k, v=pb["v"][:, sl], bh=e[:CHUNK], bl=pb["ltot"][:, sl],
                    scores=scores)

    def tail(c):
        st = st_ref[c["b"], c["h"]]
        o_ref[0, c["b"], :, c["sl"]] = _bdot(c["scores"], c["v"]) + _bdot_nt(c["q"] * jnp.exp(c["bh"]), st)
        st_ref[c["b"], c["h"]] = st * jnp.exp(c["bl"]) + _bdot_tn(c["v"], c["k"] * jnp.exp(c["bl"] - c["bh"]))

    pending = None
    pb = prep(0)
    for b in range(n_batch):
        pb_next = prep(b + 1) if b + 1 < n_batch else None
        for h in range(N_HEADS):
            cur = levels(pb, h)
            if pending is not None:
                tail(pending)
            pending = cur
        pb = pb_next
    tail(pending)


def _hgrn_scan(geo, p, lb):
    a_all, rowhi, pair = _hgrn_consts()
    p3 = p.reshape(geo.b, geo.n_tot, N_PROJ)
    return pl.pallas_call(
        functools.partial(_hgrn_scan_kernel, geo.b),
        grid=_scan_grid(geo),
        in_specs=[_chunk_spec(geo, MIX_W, COL_HG_Q // MIX_W), _chunk_spec(geo, MIX_W, COL_HG_I // MIX_W),
                  pl.BlockSpec((geo.b, CHUNK, MIX_W),
                               lambda d, j: (0, _chunk_index(geo, d, j), COL_HG_F // MIX_W + d)),
                  _per_dir_spec((1, MIX_W)), _per_dir_spec(((N_LEVELS + 1) * CHUNK, CHUNK)),
                  _per_dir_spec((N_LEVELS, CHUNK, HEAD_DIM)), _per_dir_spec((N_LEVELS + 1, CHUNK, CHUNK))],
        out_specs=_dir_chunk_spec(geo, MIX_W),
        out_shape=jax.ShapeDtypeStruct((2, geo.b, geo.n_tot, MIX_W), F32),
        scratch_shapes=[_state_scratch(geo)],
        compiler_params=_cparams(("parallel", "arbitrary")),
        name="hgrn_scan",
    )(p3, p3, p3, lb.reshape(2, 1, MIX_W), a_all, rowhi, pair)


def _ret_consts(n_lat):
    t = np.arange(CHUNK, dtype=np.float64)
    log_gamma = np.log1p(-np.exp2(-5.0 - np.arange(N_HEADS, dtype=np.float64)))
    dmat, qs, ks = [], [], []
    for rev in (False, True):
        te = t[::-1] if rev else t
        diff = te[:, None] - te[None, :]
        dmat.append(np.stack([np.where(diff >= 0, np.exp(lg * np.maximum(diff, 0)), 0.0) for lg in log_gamma]))
        qs.append(np.repeat(np.exp(log_gamma[None, :] * (te[:, None] + 1.0)), HEAD_DIM, axis=1))
        ks.append(np.repeat(np.exp(log_gamma[None, :] * (CHUNK - 1.0 - te[:, None])), HEAD_DIM, axis=1))
    g_chunk = np.repeat(np.exp(log_gamma * CHUNK), HEAD_DIM)[None, :]
    row = np.repeat(np.arange(n_lat // GRID_W, dtype=np.float32), GRID_W)
    colp = np.tile(np.arange(GRID_W, dtype=np.float32), n_lat // GRID_W)
    n_freq = HEAD_DIM // 4
    inv_freq = (np.float32(ROPE_BASE) ** (-np.arange(n_freq, dtype=np.float32) / n_freq)).astype(np.float32)
    ang = np.concatenate([row[:, None] * inv_freq, colp[:, None] * inv_freq], axis=-1).astype(np.float32)
    cos, sin = np.cos(ang), np.sin(ang)
    f32 = lambda a: jnp.asarray(np.asarray(a, np.float32))
    return (f32(np.stack(dmat)), f32(np.stack(qs)), f32(np.stack(ks)), f32(g_chunk),
            f32(np.concatenate([cos, cos], axis=-1)), f32(np.concatenate([-sin, sin], axis=-1)))


def _ret_scan_kernel(n_batch, nc_ctx, q_ref, k_ref, v_ref, cos_ref, sin_ref, dm_ref, qs_ref, ks_ref, gc_ref,
                     o_ref, s_ref):
    j = pl.program_id(1)

    @pl.when(j == 0)
    def _():
        s_ref[...] = jnp.zeros_like(s_ref)

    is_lat = j >= nc_ctx
    cos = jnp.where(is_lat, cos_ref[...], 1.0)
    sin = jnp.where(is_lat, sin_ref[...], 0.0)
    qs, ks, gc = qs_ref[0], ks_ref[0], gc_ref[...]
    chains = []
    for b in range(n_batch):
        q, k, v = q_ref[b], k_ref[b], v_ref[b]
        for h in range(N_HEADS):
            sl = slice(h * HEAD_DIM, (h + 1) * HEAD_DIM)
            qh, kh = q[:, sl], k[:, sl]
            rq = qh * cos + pltpu.roll(qh, HEAD_DIM // 2, 1) * sin
            rk = (kh * cos + pltpu.roll(kh, HEAD_DIM // 2, 1) * sin) * HEAD_DIM ** -0.5
            chains.append(dict(b=b, h=h, sl=sl, rq=rq, rk=rk, v=v[:, sl], s=s_ref[b, h]))
    for c in chains:
        c["scores"] = _bdot_nt(c["rq"], c["rk"]) * dm_ref[0, c["h"]]
    for c in chains:
        sl = c["sl"]
        o_ref[0, c["b"], :, sl] = _bdot(c["scores"], c["v"]) + _bdot(c["rq"] * qs[:, sl], c["s"])
        s_ref[c["b"], c["h"]] = c["s"] * gc[:, sl] + _bdot_tn(c["rk"] * ks[:, sl], c["v"])


def _ret_scan(geo, p):
    dmat, qs, ks, gc, cos2, sin2 = _ret_consts(geo.n_lat)
    nc_ctx = geo.n_ctx // CHUNK
    p3 = p.reshape(geo.b, geo.n_tot, N_PROJ)

    def rope_blk(d, j):
        return (jnp.where(j < nc_ctx, 0, _chunk_index(geo, d, j)), 0)

    return pl.pallas_call(
        functools.partial(_ret_scan_kernel, geo.b, nc_ctx),
        grid=_scan_grid(geo),
        in_specs=[_chunk_spec(geo, MIX_W, COL_RET_Q // MIX_W), _chunk_spec(geo, MIX_W, COL_RET_K // MIX_W),
                  _chunk_spec(geo, MIX_W, COL_RET_V // MIX_W),
                  pl.BlockSpec((CHUNK, HEAD_DIM), rope_blk), pl.BlockSpec((CHUNK, HEAD_DIM), rope_blk),
                  _per_dir_spec((N_HEADS, CHUNK, CHUNK)), _per_dir_spec((CHUNK, MIX_W)),
                  _per_dir_spec((CHUNK, MIX_W)), pl.BlockSpec((1, MIX_W), lambda d, j: (0, 0))],
        out_specs=_dir_chunk_spec(geo, MIX_W),
        out_shape=jax.ShapeDtypeStruct((2, geo.b, geo.n_tot, MIX_W), F32),
        scratch_shapes=[_state_scratch(geo)],
        compiler_params=_cparams(("parallel", "arbitrary")),
        name="ret_scan",
    )(p3, p3, p3, cos2, sin2, dmat, qs, ks, gc)


def _lru_kernel(geo, tr, rev, prev_ref, x_ref, next_ref, cw_ref, cb_ref, wg_ref, bg_ref, lam_ref,
                o_ref, h_ref):
    b, j = pl.program_id(0), pl.program_id(1)

    @pl.when(j == 0)
    def _():
        h_ref[...] = jnp.zeros_like(h_ref)

    first, last = geo.seq_edges(_lru_block(geo, tr, rev, b, j), tr)
    xb = _token_conv(prev_ref, x_ref, next_ref, cw_ref, first, last) + cb_ref[...]
    gates = _bdot(xb, wg_ref[...]) + bg_ref[...]
    gate_r = _sigmoid(gates[:, :MIX_W])
    gate_i = _sigmoid(gates[:, MIX_W:])
    log_a = LRU_C * gate_r * _log_sigmoid(lam_ref[...])
    a = jnp.exp(log_a)
    u = xb * gate_i * jnp.sqrt(jnp.maximum(-jnp.tanh(log_a) * (a * a + 1.0), SQRT_MIN))
    row = lax.broadcasted_iota(jnp.int32, (tr, MIX_W), 0)
    step = 1
    while step < tr:
        shift = tr - step if rev else step
        valid = (row < tr - step) if rev else (row >= step)
        u = jnp.where(valid, a * pltpu.roll(u, shift, 0) + u, u)
        a = jnp.where(valid, a * pltpu.roll(a, shift, 0), a)
        step *= 2
    h = a * h_ref[...] + u
    o_ref[...] = h
    h_ref[...] = h[0:1] if rev else h[tr - 1:tr]


def _lru_block(geo, tr, rev, b, j):
    nb_ctx, nb_lat = geo.n_ctx // tr, geo.n_lat // tr
    jc = nb_ctx - 1 - j if rev else j
    jl = nb_lat - 1 - (j - nb_ctx) if rev else j - nb_ctx
    return b * (geo.n_tot // tr) + jnp.where(j < nb_ctx, nb_lat + jc, jl)


def _lru_scan(geo, p, conv_w, conv_b, wa, ba, wx, bx, lam, rev):
    tr = _largest_tile(geo.row_tile, 256)
    d = 1 if rev else 0
    blockdiag = lambda w: jax.scipy.linalg.block_diag(*[w[n] for n in range(LRU_BLOCKS)])
    wg = jnp.concatenate([blockdiag(wa[d]), blockdiag(wx[d])], axis=1).astype(BF16)
    bg = jnp.concatenate([ba[d], bx[d]]).astype(F32).reshape(1, 2 * MIX_W)
    nb8, r8 = geo.m // SUBLANES, tr // SUBLANES
    cblk = COL_LRU_X // MIX_W
    blk = lambda b, j: _lru_block(geo, tr, rev, b, j)
    full = lambda shape: pl.BlockSpec(shape, lambda b, j: (0,) * len(shape))
    return pl.pallas_call(
        functools.partial(_lru_kernel, geo, tr, rev),
        grid=(geo.b, (geo.n_ctx + geo.n_lat) // tr),
        in_specs=[pl.BlockSpec((SUBLANES, MIX_W), lambda b, j: (jnp.maximum(blk(b, j) * r8 - 1, 0), cblk)),
                  pl.BlockSpec((tr, MIX_W), lambda b, j: (blk(b, j), cblk)),
                  pl.BlockSpec((SUBLANES, MIX_W),
                               lambda b, j: (jnp.minimum((blk(b, j) + 1) * r8, nb8 - 1), cblk)),
                  full((CONV_W, MIX_W)), full((1, MIX_W)), full((MIX_W, 2 * MIX_W)),
                  full((1, 2 * MIX_W)), full((1, MIX_W))],
        out_specs=pl.BlockSpec((tr, MIX_W), lambda b, j: (blk(b, j), 0)),
        out_shape=jax.ShapeDtypeStruct((geo.m, MIX_W), F32),
        scratch_shapes=[pltpu.VMEM((1, MIX_W), F32)],
        compiler_params=_cparams(("parallel", "arbitrary")),
        name="lru_scan_bwd" if rev else "lru_scan_fwd",
    )(p, p, p, conv_w.astype(F32), conv_b.astype(F32).reshape(1, MIX_W), wg, bg,
      lam[d].astype(F32).reshape(1, MIX_W))


def _gelu_tanh(x):
    return 0.5 * x * (1.0 + jnp.tanh(math.sqrt(2.0 / math.pi) * (x + 0.044715 * (x * x * x))))


def _gated_head_norm(o, g, z):
    parts = []
    for h in range(N_HEADS):
        oh = o[:, h * HEAD_DIM:(h + 1) * HEAD_DIM]
        parts.append(oh * lax.rsqrt(jnp.mean(oh * oh, axis=-1, keepdims=True) + EPS) * g)
    return jnp.concatenate(parts, axis=-1) * _silu(z)


def _merge_kernel(og_ref, oh_ref, olf_ref, olb_ref, or_ref, gz_ref, hg_ref, ly_ref, rg_ref, mg_ref,
                  x_ref, gt_ref, gng_ref, hng_ref, rng_ref, wb_ref, wo_ref, out_ref):
    branches = (
        _gated_head_norm(og_ref[0] + og_ref[1], gng_ref[...], gz_ref[...]),
        _gated_head_norm(oh_ref[0] + oh_ref[1], hng_ref[...], hg_ref[...]),
        (olf_ref[...] + olb_ref[...]) * _gelu_tanh(ly_ref[...]),
        _gated_head_norm(or_ref[0] + or_ref[1], rng_ref[...], rg_ref[...]),
    )
    mix = None
    for n, y in enumerate(branches):
        term = _sigmoid(mg_ref[:, n * D_MODEL:(n + 1) * D_MODEL]) * _bdot(y, wb_ref[n])
        mix = term if mix is None else mix + term
    out_ref[...] = x_ref[...] + gt_ref[0] * _bdot(mix, wo_ref[...])


def _merge(geo, compact, x_rows, p, o_gdn, o_hgrn, o_lru_f, o_lru_b, o_ret, mods3, gdn_g, hgrn_g, ret_g,
           w_branch, w_out):
    tm = _largest_tile(geo.row_tile, 256)
    n_rows = geo.m_lat if compact else geo.m
    src = lambda i: geo.full_tile(i, tm, compact)
    both = pl.BlockSpec((2, tm, MIX_W), lambda i: (0, src(i), 0))
    rows = lambda width: pl.BlockSpec((tm, width), lambda i: (src(i), 0))
    col = lambda c, width: pl.BlockSpec((tm, width), lambda i: (src(i), c // width))
    full = lambda shape: pl.BlockSpec(shape, lambda i: (0,) * len(shape))
    hd = lambda g: g.astype(F32).reshape(1, HEAD_DIM)
    return pl.pallas_call(
        _merge_kernel,
        grid=(n_rows // tm,),
        in_specs=[both, both, rows(MIX_W), rows(MIX_W), both,
                  col(COL_GDN_Z, MIX_W), col(COL_HG_G, MIX_W), col(COL_LRU_Y, MIX_W), col(COL_RET_G, MIX_W),
                  col(COL_MERGE, N_BRANCH * D_MODEL), rows(D_MODEL),
                  pl.BlockSpec((1, 1, D_MODEL), lambda i: (geo.mod_row(i, tm, compact), 0, 2)),
                  full((1, HEAD_DIM)), full((1, HEAD_DIM)), full((1, HEAD_DIM)),
                  full((N_BRANCH, MIX_W, D_MODEL)), full((D_MODEL, D_MODEL))],
        out_specs=pl.BlockSpec((tm, D_MODEL), lambda i: (i, 0)),
        out_shape=jax.ShapeDtypeStruct((n_rows, D_MODEL), F32),
        compiler_params=_cparams(("parallel",)),
        name="merge",
    )(o_gdn, o_hgrn, o_lru_f, o_lru_b, o_ret, p, p, p, p, p, x_rows, mods3, hd(gdn_g), hd(hgrn_g), hd(ret_g),
      w_branch.astype(BF16), w_out.astype(BF16))


ROW_TILE = D_MODEL // LANES
assert ROW_TILE == SUBLANES


def _store_row_tiles(ref, val):
    n = val.shape[0]
    for s in range(ROW_TILE):
        ref[pl.ds(s, n, stride=ROW_TILE), :] = val[:, s * LANES:(s + 1) * LANES]


def _load_row_tiles(ref_at, start, n):
    return jnp.concatenate([ref_at[pl.ds(start + s, n, stride=ROW_TILE), :] for s in range(ROW_TILE)], axis=1)


def _router_kernel(n_experts, x_ref, g_ref, sc_ref, sh_ref, rw_ref, rb_ref, f_ref, idx_ref, wt_ref):
    f = _rms_mod(x_ref[...], g_ref[...], sc_ref[0], sh_ref[0])
    _store_row_tiles(f_ref, f)
    logits = _dot(f, rw_ref[...], precision=HIGHEST) + rb_ref[...]
    lane = lax.broadcasted_iota(jnp.int32, logits.shape, 1)
    logits = jnp.where(lane < n_experts, logits, NEG_BIG)
    vals, idxs = [], []
    for _ in range(TOP_K):
        mx = jnp.max(logits, axis=-1, keepdims=True)
        ix = jnp.min(jnp.where(logits == mx, lane, LANES), axis=-1, keepdims=True)
        vals.append(mx)
        idxs.append(ix)
        logits = jnp.where(lane == ix, -jnp.inf, logits)
    ex = [jnp.exp(vl - vals[0]) for vl in vals]
    tot = ex[0] + ex[1] + ex[2] + ex[3]
    idx_out = jnp.zeros(lane.shape, jnp.int32)
    wt_out = jnp.zeros(lane.shape, F32)
    for kk in range(TOP_K):
        idx_out = jnp.where(lane == kk, idxs[kk], idx_out)
        wt_out = jnp.where(lane == kk, ex[kk] / tot, wt_out)
    idx_ref[...] = idx_out
    wt_ref[...] = wt_out


def _router(geo, compact, x_rows, g, mods3, router_w, router_b):
    n_rows, d = x_rows.shape
    n_experts = router_w.shape[1]
    tm = _largest_tile(geo.row_tile, 256)
    rw = jnp.pad(router_w.astype(F32), ((0, 0), (0, LANES - n_experts)))
    rb = jnp.pad(router_b.astype(F32), (0, LANES - n_experts)).reshape(1, LANES)
    rows = lambda width: pl.BlockSpec((tm, width), lambda i: (i, 0))
    full = lambda shape: pl.BlockSpec(shape, lambda i: (0,) * len(shape))
    return pl.pallas_call(
        functools.partial(_router_kernel, n_experts),
        grid=(n_rows // tm,),
        in_specs=[rows(d), full((1, d)),
                  pl.BlockSpec((1, 1, d), lambda i: (geo.mod_row(i, tm, compact), 0, 4)),
                  pl.BlockSpec((1, 1, d), lambda i: (geo.mod_row(i, tm, compact), 0, 3)),
                  full((d, LANES)), full((1, LANES))],
        out_specs=[pl.BlockSpec((tm * ROW_TILE, LANES), lambda i: (i, 0)), rows(LANES), rows(LANES)],
        out_shape=[jax.ShapeDtypeStruct((n_rows * ROW_TILE, LANES), F32),
                   jax.ShapeDtypeStruct((n_rows, LANES), jnp.int32), jax.ShapeDtypeStruct((n_rows, LANES), F32)],
        compiler_params=_cparams(("parallel",)),
        name="router",
    )(x_rows, g.astype(F32).reshape(1, d), mods3, mods3, rw, rb)


def _route_plan(top_idx, n_experts):
    m = top_idx.shape[0]
    n_assign = m * TOP_K
    flat_e = top_idx.reshape(-1)
    order = jnp.argsort(flat_e).astype(jnp.int32)
    rank = jnp.argsort(order).astype(jnp.int32)
    counts = jnp.sum(flat_e[:, None] == jnp.arange(n_experts, dtype=jnp.int32)[None, :], axis=0, dtype=jnp.int32)
    padded = (counts + ROUTE_BLOCK - 1) // ROUTE_BLOCK * ROUTE_BLOCK
    pad_end = jnp.cumsum(padded)
    pad_start = pad_end - padded
    start = jnp.cumsum(counts) - counts
    n_blocks = -(-n_assign // ROUTE_BLOCK) + n_experts
    cap = n_blocks * ROUTE_BLOCK
    block_row0 = jnp.arange(n_blocks, dtype=jnp.int32) * ROUTE_BLOCK
    block_expert = jnp.minimum(jnp.sum(pad_end[None, :] <= block_row0[:, None], axis=1, dtype=jnp.int32),
                               n_experts - 1)
    n_used = (pad_end[-1:] // ROUTE_BLOCK).astype(jnp.int32)
    row = jnp.arange(cap, dtype=jnp.int32)
    row_e = jnp.repeat(block_expert, ROUTE_BLOCK)
    within = row - pad_start[row_e]
    valid = jnp.logical_and(within >= 0, within < counts[row_e])
    src_tok = jnp.where(valid, order[jnp.clip(start[row_e] + within, 0, n_assign - 1)] // TOP_K, 0)
    dest = pad_start[flat_e] + rank - start[flat_e]
    return src_tok.reshape(n_blocks, 1, ROUTE_BLOCK), block_expert, n_used, dest


def _start_row_gather(idx_ref, n, src_hbm, buf, slot, sem):
    for r in range(n):
        src_row = pl.multiple_of(idx_ref[0, 0, r] * ROW_TILE, ROW_TILE)
        pltpu.make_async_copy(src_hbm.at[pl.ds(src_row, ROW_TILE)], buf.at[slot, pl.ds(r * ROW_TILE, ROW_TILE)],
                              sem.at[slot]).start(priority=r % 2)


def _wait_row_gather(n, src_hbm, buf, slot, sem):
    pltpu.make_async_copy(src_hbm.at[pl.ds(0, n * ROW_TILE)], buf.at[slot], sem.at[slot]).wait()


def _gather_step(cur_ref, nxt_ref, n, src_hbm, buf, sem):
    i, nb = pl.program_id(0), pl.num_programs(0)
    slot = i % 2

    @pl.when(i == 0)
    def _():
        _start_row_gather(cur_ref, n, src_hbm, buf, 0, sem)

    _wait_row_gather(n, src_hbm, buf, slot, sem)
    _start_row_gather(nxt_ref, n, src_hbm, buf, 1 - slot, sem)
    return slot


def _gather_drain(n, src_hbm, buf, slot, sem):
    @pl.when(pl.program_id(0) == pl.num_programs(0) - 1)
    def _():
        _wait_row_gather(n, src_hbm, buf, 1 - slot, sem)


def _expert_kernel(be_ref, nu_ref, src_ref, nxt_ref, f_hbm, w1_ref, b1_ref, w2_ref, b2_ref, y_ref,
                   buf, sem, w1b_ref, w2b_ref):
    i = pl.program_id(0)
    ff = w2_ref.shape[1]
    slot = _gather_step(src_ref, nxt_ref, ROUTE_BLOCK, f_hbm, buf, sem)

    @pl.when(jnp.logical_or(i == 0, be_ref[i] != be_ref[jnp.maximum(i - 1, 0)]))
    def _():
        w1b_ref[...] = w1_ref[0].astype(BF16)
        w2b_ref[...] = w2_ref[0].astype(BF16)

    @pl.when(i >= nu_ref[0])
    def _():
        y_ref[...] = jnp.zeros_like(y_ref)

    @pl.when(i < nu_ref[0])
    def _():
        z = _bdot(_load_row_tiles(buf.at[slot], 0, ROUTE_BLOCK), w1b_ref[...]) + b1_ref[0]
        glu = jnp.minimum(z[:, :ff], SWIGLU_LIMIT)
        lin = jnp.clip(z[:, ff:], -SWIGLU_LIMIT, SWIGLU_LIMIT)
        act = glu * _sigmoid(SWIGLU_ALPHA * glu) * (lin + 1.0)
        _store_row_tiles(y_ref, _bdot(act, w2b_ref[...]) + b2_ref[0])

    _gather_drain(ROUTE_BLOCK, f_hbm, buf, slot, sem)


def _experts(f_tiles, src_tok, block_expert, n_used, layer, w1_all, b1_all, w2_all, b2_all):
    n_blocks = src_tok.shape[0]
    n_layers, n_experts, d, ff2 = w1_all.shape
    ff = w2_all.shape[2]
    w1 = w1_all.reshape(n_layers * n_experts, d, ff2)
    w2 = w2_all.reshape(n_layers * n_experts, ff, d)
    b1 = b1_all.reshape(n_layers * n_experts, ff2)
    b2 = b2_all.reshape(n_layers * n_experts, d)
    block_expert = block_expert + layer * n_experts
    n_experts = n_layers * n_experts
    grid_spec = pltpu.PrefetchScalarGridSpec(
        num_scalar_prefetch=2,
        grid=(n_blocks,),
        in_specs=[pl.BlockSpec((1, 1, ROUTE_BLOCK), lambda i, be, nu: (i, 0, 0), memory_space=pltpu.SMEM),
                  pl.BlockSpec((1, 1, ROUTE_BLOCK), lambda i, be, nu: (jnp.minimum(i + 1, n_blocks - 1), 0, 0),
                               memory_space=pltpu.SMEM),
                  pl.BlockSpec(memory_space=pl.ANY),
                  pl.BlockSpec((1, d, ff2), lambda i, be, nu: (be[i], 0, 0)),
                  pl.BlockSpec((1, 1, ff2), lambda i, be, nu: (be[i], 0, 0)),
                  pl.BlockSpec((1, ff, d), lambda i, be, nu: (be[i], 0, 0)),
                  pl.BlockSpec((1, 1, d), lambda i, be, nu: (be[i], 0, 0))],
        out_specs=pl.BlockSpec((ROUTE_BLOCK * ROW_TILE, LANES), lambda i, be, nu: (i, 0)),
        scratch_shapes=[pltpu.VMEM((2, ROUTE_BLOCK * ROW_TILE, LANES), F32), pltpu.SemaphoreType.DMA((2,)),
                        pltpu.VMEM((d, ff2), BF16), pltpu.VMEM((ff, d), BF16)],
    )
    return pl.pallas_call(
        _expert_kernel,
        grid_spec=grid_spec,
        out_shape=jax.ShapeDtypeStruct((n_blocks * ROUTE_BLOCK * ROW_TILE, LANES), F32),
        compiler_params=_cparams(("arbitrary",)),
        name="experts",
    )(block_expert, n_used, src_tok, src_tok, f_tiles, w1, b1.astype(F32).reshape(n_experts, 1, ff2),
      w2, b2.astype(F32).reshape(n_experts, 1, d))


def _combine_kernel(tc, final, dst_ref, nxt_ref, y_hbm, x_ref, wt_ref, gt_ref, fg_ref, out_ref, buf, sem):
    n = TOP_K * tc
    slot = _gather_step(dst_ref, nxt_ref, n, y_hbm, buf, sem)
    wt = wt_ref[...]
    acc = None
    for kk in range(TOP_K):
        term = wt[:, kk:kk + 1] * _load_row_tiles(buf.at[slot], kk * tc * ROW_TILE, tc)
        acc = term if acc is None else acc + term
    out = x_ref[...] + gt_ref[0] * acc
    if final:
        out = out * lax.rsqrt(jnp.mean(out * out, axis=-1, keepdims=True) + EPS) * fg_ref[...]
    out_ref[...] = out
    _gather_drain(n, y_hbm, buf, slot, sem)


def _combine(geo, y_pad, dest, x_rows, wt, mods3, final_g, final):
    n_rows, d = x_rows.shape
    tc = _largest_tile(geo.row_tile, 128)
    nt = n_rows // tc
    dst = dest.reshape(nt, tc, TOP_K).transpose(0, 2, 1).reshape(nt, 1, TOP_K * tc)
    rows = lambda width: pl.BlockSpec((tc, width), lambda i: (i, 0))
    return pl.pallas_call(
        functools.partial(_combine_kernel, tc, final),
        grid=(nt,),
        in_specs=[pl.BlockSpec((1, 1, TOP_K * tc), lambda i: (i, 0, 0), memory_space=pltpu.SMEM),
                  pl.BlockSpec((1, 1, TOP_K * tc), lambda i: (jnp.minimum(i + 1, nt - 1), 0, 0),
                               memory_space=pltpu.SMEM),
                  pl.BlockSpec(memory_space=pl.ANY), rows(d), rows(LANES),
                  pl.BlockSpec((1, 1, d), lambda i: (geo.mod_row(i, tc, final), 0, 5)),
                  pl.BlockSpec((1, d), lambda i: (0, 0))],
        out_specs=rows(d),
        out_shape=jax.ShapeDtypeStruct((n_rows, d), F32),
        scratch_shapes=[pltpu.VMEM((2, TOP_K * tc * ROW_TILE, LANES), F32), pltpu.SemaphoreType.DMA((2,))],
        compiler_params=_cparams(("arbitrary",)),
        name="moe_combine",
    )(dst, dst, y_pad, x_rows, wt, mods3, final_g.astype(F32).reshape(1, d))


def _moe(geo, x_rows, g, mods3, router_w, router_b, layer, w1_all, b1_all, w2_all, b2_all, final_g, final):
    f_tiles, top_idx, top_w = _router(geo, final, x_rows, g, mods3, router_w, router_b)
    src_tok, block_expert, n_used, dest = _route_plan(top_idx[:, :TOP_K], router_w.shape[1])
    y_tiles = _experts(f_tiles, src_tok, block_expert, n_used, layer, w1_all, b1_all, w2_all, b2_all)
    return _combine(geo, y_tiles, dest, x_rows, top_w, mods3, final_g, final)


def kernel(x, c, ctx, c_ctx, w_mod, b_mod, norm1_g, norm2_g, w_in, gdn_conv_w, gdn_a_log, gdn_dt_bias, gdn_norm_g, hgrn_lb, hgrn_norm_g, lru_conv_w, lru_conv_b, lru_wa, lru_ba, lru_wx, lru_bx, lru_lambda, ret_norm_g, w_branch, w_out, router_w, router_b, moe_w1, moe_b1, moe_w2, moe_b2, final_norm_g):
    b, n_lat, d = x.shape
    n_ctx = ctx.shape[1]
    depth = w_mod.shape[0]
    assert d == D_MODEL and w_branch.shape[2] == MIX_W
    geo = _Geo(b, n_lat, n_ctx)
    lb_soft = jax.nn.softmax(hgrn_lb.astype(F32), axis=0)
    lower_bounds = jnp.clip(jnp.cumsum(lb_soft, axis=0) - lb_soft[0], 0.0, LB_MAX)
    n_cond = -(-(b + 1) // SUBLANES) * SUBLANES
    cc = jnp.concatenate([c, c_ctx[None, :], jnp.zeros((n_cond - b - 1, d), c.dtype)], axis=0)
    mods = _modulation(cc.astype(F32), w_mod, b_mod)
    rows = jnp.concatenate([x, ctx], axis=1).reshape(geo.m, d)
    flat = lambda o: o.reshape(2, geo.m, MIX_W)
    for l in range(depth):
        last = l == depth - 1
        mods3 = mods[l].reshape(n_cond, 1, N_MOD * d)
        p = _norm_proj(geo, rows, norm1_g[l], mods3, _reorder_w_in(w_in[l]))
        gq, gk, gv, gb = _gdn_prep(geo, p, gdn_conv_w[l], gdn_a_log[l], gdn_dt_bias[l])
        o_gdn = flat(_gdn_scan(geo, gq, gk, gv, gb))
        o_hgrn = flat(_hgrn_scan(geo, p, lower_bounds[l]))
        lru_args = (geo, p, lru_conv_w[l], lru_conv_b[l], lru_wa[l], lru_ba[l], lru_wx[l], lru_bx[l],
                    lru_lambda[l])
        o_lru_f = _lru_scan(*lru_args, rev=False)
        o_lru_b = _lru_scan(*lru_args, rev=True)
        o_ret = flat(_ret_scan(geo, p))
        rows = _merge(geo, last, rows, p, o_gdn, o_hgrn, o_lru_f, o_lru_b, o_ret, mods3, gdn_norm_g[l],
                      hgrn_norm_g[l], ret_norm_g[l], w_branch[l], w_out[l])
        rows = _moe(geo, rows, norm2_g[l], mods3, router_w[l], router_b[l], l, moe_w1, moe_b1, moe_w2, moe_b2,
                    final_norm_g, last)
    return rows.reshape(b, n_lat, d)
```

```python
import functools
import math

import numpy as np
import jax
import jax.numpy as jnp
from jax import lax
from jax.experimental import pallas as pl
from jax.experimental.pallas import tpu as pltpu

F32 = jnp.float32
BF16 = jnp.bfloat16
HIGHEST = lax.Precision.HIGHEST

GRID_W = 64
N_BRANCH = 4
HEAD_DIM = 128
LRU_BLOCKS = 8
LRU_C = 8.0
CONV_W = 4
CHUNK = 64
ROPE_BASE = 10000.0
TOP_K = 4
SWIGLU_ALPHA = 1.702
SWIGLU_LIMIT = 7.0
N_MOD = 6
EPS = 1e-6
LB_MIN = 1e-30
LB_MAX = 1.0 - 1e-4
SQRT_MIN = 1e-12

LANES = 128
SUBLANES = 8
VMEM_LIMIT_BYTES = 56 * 1024 * 1024
ROUTE_BLOCK = 256
NEG_BIG = -1e30


def _cparams(semantics):
    return pltpu.CompilerParams(dimension_semantics=semantics, vmem_limit_bytes=VMEM_LIMIT_BYTES)


def _largest_tile(n, cap, quantum=SUBLANES):
    best = None
    for t in range(quantum, min(n, cap) + 1, quantum):
        if n % t == 0:
            best = t
    assert best is not None, (n, cap)
    return best


def _dot(a, b, precision=None):
    return jnp.dot(a, b, preferred_element_type=F32, precision=precision)


def _dot_nt(a, b):
    return lax.dot_general(a, b, (((1,), (1,)), ((), ())), preferred_element_type=F32)


def _dot_tn(a, b):
    return lax.dot_general(a, b, (((0,), (0,)), ((), ())), preferred_element_type=F32)


def _bdot(a, b):
    return _dot(a.astype(BF16), b.astype(BF16))


def _bdot_nt(a, b):
    return _dot_nt(a.astype(BF16), b.astype(BF16))


def _bdot_tn(a, b):
    return _dot_tn(a.astype(BF16), b.astype(BF16))


def _dot_exact01(a01x3, x):
    hi = x.astype(BF16)
    r1 = x - hi.astype(F32)
    mid = r1.astype(BF16)
    lo = (r1 - mid.astype(F32)).astype(BF16)
    return _dot(a01x3, jnp.concatenate([hi, mid, lo], axis=0))


def _sigmoid(x):
    return 1.0 / (1.0 + jnp.exp(-x))


def _silu(x):
    return x * _sigmoid(x)


def _softplus(x):
    return jnp.maximum(x, 0.0) + jnp.log1p(jnp.exp(-jnp.abs(x)))


def _log_sigmoid(x):
    return -_softplus(-x)


def _mod_kernel(c_ref, w_ref, b_ref, o_ref):
    o_ref[0] = _dot(_silu(c_ref[...]), w_ref[0], precision=HIGHEST) + b_ref[0]


def _modulation(cc, w_mod, b_mod):
    n_layers, d, n = w_mod.shape
    rows = cc.shape[0]
    tn = _largest_tile(n, 1024, LANES)
    return pl.pallas_call(
        _mod_kernel,
        grid=(n_layers, n // tn),
        in_specs=[pl.BlockSpec((rows, d), lambda l, j: (0, 0)),
                  pl.BlockSpec((1, d, tn), lambda l, j: (l, 0, j)),
                  pl.BlockSpec((1, 1, tn), lambda l, j: (l, 0, j))],
        out_specs=pl.BlockSpec((1, rows, tn), lambda l, j: (l, 0, j)),
        out_shape=jax.ShapeDtypeStruct((n_layers, rows, n), F32),
        compiler_params=_cparams(("parallel", "parallel")),
        name="modulation",
    )(cc, w_mod, b_mod.reshape(n_layers, 1, n))


D_MODEL = 1024
MIX_W = 512
N_HEADS = MIX_W // HEAD_DIM
COL_MERGE = 0
COL_GDN_Z = 4096
COL_GDN_QKV = 4608
COL_HG_Q = 6144
COL_HG_I = 6656
COL_HG_F = 7168
COL_HG_G = 8192
COL_LRU_X = 8704
COL_LRU_Y = 9216
COL_RET_Q = 9728
COL_RET_K = 10240
COL_RET_V = 10752
COL_RET_G = 11264
COL_AB = 11776
N_PROJ = 12288


class _Geo:
    def __init__(self, b, n_lat, n_ctx):
        self.b, self.n_lat, self.n_ctx = b, n_lat, n_ctx
        self.n_tot = n_lat + n_ctx
        self.m_lat = b * n_lat
        self.m = b * self.n_tot
        assert n_lat % CHUNK == 0 and n_ctx % CHUNK == 0 and n_lat % GRID_W == 0
        self.row_tile = math.gcd(n_lat, n_ctx)

    def mod_row(self, i, tm, compact=False):
        r0 = i * tm
        if compact:
            return r0 // self.n_lat
        return jnp.where(r0 % self.n_tot < self.n_lat, r0 // self.n_tot, self.b)

    def full_tile(self, i, tm, compact=False):
        if not compact:
            return i
        per_lat, per_tot = self.n_lat // tm, self.n_tot // tm
        return (i // per_lat) * per_tot + i % per_lat

    def seq_edges(self, i, tr):
        pos = (i * tr) % self.n_tot
        first = jnp.logical_or(pos == 0, pos == self.n_lat)
        last = jnp.logical_or(pos + tr == self.n_lat, pos + tr == self.n_tot)
        return first, last


def _reorder_w_in(w_in):
    w = w_in.astype(BF16)
    d = w.shape[0]
    pad = lambda n: jnp.zeros((d, n), BF16)
    ab = [jnp.concatenate([w[:, 2048 + 4 * dr:2052 + 4 * dr], w[:, 2056 + 4 * dr:2060 + 4 * dr],
                           pad(LANES - 8)], axis=1) for dr in range(2)]
    out = jnp.concatenate([w[:, 7696:11792], w[:, 1536:2048], w[:, 0:1536], w[:, 2064:7696],
                           ab[0], ab[1], pad(N_PROJ - COL_AB - 2 * LANES)], axis=1)
    assert out.shape[1] == N_PROJ
    return out


def _rms_mod(x, g, sc, sh):
    y = x * lax.rsqrt(jnp.mean(x * x, axis=-1, keepdims=True) + EPS) * g
    return y * (1.0 + sc) + sh


def _norm_kernel(x_ref, g_ref, sc_ref, sh_ref, h_ref):
    h_ref[...] = _rms_mod(x_ref[...], g_ref[...], sc_ref[0], sh_ref[0]).astype(BF16)


def _proj_kernel(h_ref, w_ref, o_ref):
    o_ref[...] = _dot(h_ref[...], w_ref[...])


def _norm_proj(geo, x_rows, g, mods3, w_p):
    m, d = x_rows.shape
    n = w_p.shape[1]
    tr = _largest_tile(geo.row_tile, 512)
    h = pl.pallas_call(
        _norm_kernel,
        grid=(m // tr,),
        in_specs=[pl.BlockSpec((tr, d), lambda i: (i, 0)),
                  pl.BlockSpec((1, d), lambda i: (0, 0)),
                  pl.BlockSpec((1, 1, d), lambda i: (geo.mod_row(i, tr), 0, 1)),
                  pl.BlockSpec((1, 1, d), lambda i: (geo.mod_row(i, tr), 0, 0))],
        out_specs=pl.BlockSpec((tr, d), lambda i: (i, 0)),
        out_shape=jax.ShapeDtypeStruct((m, d), BF16),
        compiler_params=_cparams(("parallel",)),
        name="norm1",
    )(x_rows, g.reshape(1, d), mods3, mods3)
    tm = _largest_tile(m, 1024)
    tn = _largest_tile(n, 2048, LANES)
    return pl.pallas_call(
        _proj_kernel,
        grid=(n // tn, m // tm),
        in_specs=[pl.BlockSpec((tm, d), lambda j, i: (i, 0)),
                  pl.BlockSpec((d, tn), lambda j, i: (0, j))],
        out_specs=pl.BlockSpec((tm, tn), lambda j, i: (i, j)),
        out_shape=jax.ShapeDtypeStruct((m, n), F32),
        compiler_params=_cparams(("parallel", "parallel")),
        name="in_proj",
    )(h, w_p)


def _halo_specs(geo, tr, width, col_block):
    nb8 = geo.m // SUBLANES
    r8 = tr // SUBLANES
    return [pl.BlockSpec((SUBLANES, width), lambda i: (jnp.maximum(i * r8 - 1, 0), col_block)),
            pl.BlockSpec((tr, width), lambda i: (i, col_block)),
            pl.BlockSpec((SUBLANES, width), lambda i: (jnp.minimum((i + 1) * r8, nb8 - 1), col_block))]


def _token_conv(prev_ref, x_ref, next_ref, w_ref, first, last):
    x = x_ref[...]
    tr = x.shape[0]
    prev = jnp.where(first, 0.0, prev_ref[...])
    nxt = jnp.where(last, 0.0, next_ref[...])
    ext = jnp.concatenate([prev, x, nxt], axis=0)
    w = w_ref[...]
    y = None
    for j in range(CONV_W):
        off = SUBLANES - CONV_W // 2 + j
        term = ext[off:off + tr] * w[j:j + 1]
        y = term if y is None else y + term
    return y


def _gdn_prep_kernel(geo, tr, prev_ref, x_ref, next_ref, ab0_ref, ab1_ref, cw_ref, alog_ref, dtb_ref,
                     q_ref, k_ref, v_ref, gb_ref):
    first, last = geo.seq_edges(pl.program_id(0), tr)
    s = _silu(_token_conv(prev_ref, x_ref, next_ref, cw_ref, first, last))
    for h in range(N_HEADS):
        sl = slice(h * HEAD_DIM, (h + 1) * HEAD_DIM)
        qh = s[:, sl]
        kh = s[:, MIX_W + h * HEAD_DIM:MIX_W + (h + 1) * HEAD_DIM]
        q_ref[:, sl] = qh * (lax.rsqrt(jnp.sum(qh * qh, axis=-1, keepdims=True) + EPS) * HEAD_DIM ** -0.5)
        k_ref[:, sl] = kh * lax.rsqrt(jnp.sum(kh * kh, axis=-1, keepdims=True) + EPS)
    v_ref[...] = s[:, 2 * MIX_W:]
    lane = lax.broadcasted_iota(jnp.int32, (tr, LANES), 1)
    for dr, ab_ref in enumerate((ab0_ref, ab1_ref)):
        a = ab_ref[...]
        g = -jnp.exp(alog_ref[dr]) * _softplus(a + dtb_ref[dr])
        gb_ref[dr] = jnp.where(lane < N_HEADS, g, _sigmoid(a))


def _gdn_prep(geo, p, conv_w, a_log, dt_bias):
    tr = _largest_tile(math.gcd(geo.n_lat, geo.n_ctx), 256)
    m = geo.m
    lane_pad = lambda v: jnp.pad(v.astype(F32), ((0, 0), (0, LANES - N_HEADS))).reshape(2, 1, LANES)
    ab_blk = COL_AB // LANES
    rows = lambda width: pl.BlockSpec((tr, width), lambda i: (i, 0))
    full3 = pl.BlockSpec((2, 1, LANES), lambda i: (0, 0, 0))
    return pl.pallas_call(
        functools.partial(_gdn_prep_kernel, geo, tr),
        grid=(m // tr,),
        in_specs=_halo_specs(geo, tr, 3 * MIX_W, COL_GDN_QKV // (3 * MIX_W)) + [
            pl.BlockSpec((tr, LANES), lambda i: (i, ab_blk)),
            pl.BlockSpec((tr, LANES), lambda i: (i, ab_blk + 1)),
            pl.BlockSpec((CONV_W, 3 * MIX_W), lambda i: (0, 0)), full3, full3],
        out_specs=[rows(MIX_W), rows(MIX_W), rows(MIX_W),
                   pl.BlockSpec((2, tr, LANES), lambda i: (0, i, 0))],
        out_shape=[jax.ShapeDtypeStruct((m, MIX_W), F32)] * 3 + [jax.ShapeDtypeStruct((2, m, LANES), F32)],
        compiler_params=_cparams(("parallel",)),
        name="gdn_prep",
    )(p, p, p, p, p, conv_w.astype(F32), lane_pad(a_log), lane_pad(dt_bias))


def _chunk_index(geo, d, j):
    nc_ctx, nc_lat = geo.n_ctx // CHUNK, geo.n_lat // CHUNK
    jc = jnp.where(d == 0, j, nc_ctx - 1 - j)
    jl = jnp.where(d == 0, j - nc_ctx, nc_lat - 1 - (j - nc_ctx))
    return jnp.where(j < nc_ctx, nc_lat + jc, jl)


def _scan_grid(geo):
    return (2, geo.n_tot // CHUNK)


def _chunk_spec(geo, width, col_block=0):
    return pl.BlockSpec((geo.b, CHUNK, width), lambda d, j: (0, _chunk_index(geo, d, j), col_block))


def _dir_chunk_spec(geo, width):
    return pl.BlockSpec((1, geo.b, CHUNK, width), lambda d, j: (d, 0, _chunk_index(geo, d, j), 0))


def _per_dir_spec(shape):
    return pl.BlockSpec((1,) + shape, lambda d, j: (d,) + (0,) * len(shape))


N_LEVELS = int(math.log2(CHUNK))


def _chunk_consts():
    t = np.arange(CHUNK)
    tt, uu = t[:, None], t[None, :]
    tri = uu <= tt
    pair = []
    for lv in range(N_LEVELS):
        hi = ((t >> lv) & 1) == 1
        pair.append(((uu >> (lv + 1)) == (tt >> (lv + 1))) & hi[:, None] & (~hi)[None, :])
    pair = np.stack(pair)
    both = lambda a: np.stack([a, a[..., ::-1, ::-1]]).astype(np.float32)
    return jnp.asarray(both(tri)), jnp.asarray(both(pair))


def _gdn_scan_kernel(n_batch, qf_ref, kf_ref, vf_ref, qb_ref, kb_ref, vb_ref, gbf_ref, gbb_ref, tri_ref,
                     pair_ref, of_ref, ob_ref, s_ref):
    @pl.when(pl.program_id(0) == 0)
    def _():
        s_ref[...] = jnp.zeros_like(s_ref)

    row = lax.broadcasted_iota(jnp.int32, (CHUNK, CHUNK), 0)
    col = lax.broadcasted_iota(jnp.int32, (CHUNK, CHUNK), 1)
    eye = (row == col).astype(F32)
    chains = []
    for d, (q_ref, k_ref, v_ref, gb_ref, o_ref) in enumerate(((qf_ref, kf_ref, vf_ref, gbf_ref, of_ref),
                                                              (qb_ref, kb_ref, vb_ref, gbb_ref, ob_ref))):
        tri = tri_ref[d]
        incl = tri > 0.5
        for b in range(n_batch):
            gb = gb_ref[0, b]
            gcum = _dot(tri, gb, precision=HIGHEST)
            gcum_t = gcum.T
            gtot = jnp.sum(gb, axis=0, keepdims=True)
            q, k, v = q_ref[b], k_ref[b], v_ref[b]
            for h in range(N_HEADS):
                sl = slice(h * HEAD_DIM, (h + 1) * HEAD_DIM)
                chains.append(dict(d=d, b=b, h=h, sl=sl, o_ref=o_ref, incl=incl, bc=gcum[:, h:h + 1],
                                   br=gcum_t[h:h + 1, :], beta=gb[:, N_HEADS + h:N_HEADS + h + 1],
                                   bl=gtot[:, h:h + 1], q=q[:, sl], k=k[:, sl], v=v[:, sl], s=s_ref[d, b, h]))
    for c in chains:
        c["decay"] = jnp.where(c["incl"], jnp.exp(jnp.where(c["incl"], c["bc"] - c["br"], 0.0)), 0.0)
        c["kbeta"] = c["k"] * c["beta"]
        c["k16"] = c["k"].astype(BF16)
        c["mm"] = _dot_nt(c["kbeta"].astype(BF16), c["k16"]) * c["decay"]
        c["dinv"] = eye - pair_ref[c["d"], 0] * c["mm"]
    for lv in range(1, N_LEVELS):
        for c in chains:
            c["ld"] = _bdot(pair_ref[c["d"], lv] * c["mm"], c["dinv"])
        for c in chains:
            c["dinv"] = c["dinv"] - _bdot(c["dinv"], c["ld"])
    for c in chains:
        c["eb"] = jnp.exp(c["bc"])
        rhs = jnp.concatenate([c["v"] * c["beta"], c["kbeta"] * c["eb"]], axis=-1)
        c["sol"] = rhs + _bdot(c["dinv"] - eye, rhs)
        c["attn"] = _dot_nt(c["q"].astype(BF16), c["k16"]) * c["decay"]
    for c in chains:
        c["v_new"] = c["sol"][:, :HEAD_DIM] - _bdot(c["sol"][:, HEAD_DIM:], c["s"])
    for c in chains:
        c["o_ref"][c["b"], :, c["sl"]] = _bdot(c["q"] * c["eb"], c["s"]) + _bdot(c["attn"], c["v_new"])
        s_ref[c["d"], c["b"], c["h"]] = (c["s"] * jnp.exp(c["bl"])
                                         + _bdot_tn(c["k"] * jnp.exp(c["bl"] - c["bc"]), c["v_new"]))


def _state_scratch(geo):
    return pltpu.VMEM((geo.b, N_HEADS, HEAD_DIM, HEAD_DIM), F32)


def _gdn_scan(geo, q, k, v, gb):
    tri, pair = _chunk_consts()
    view = lambda a: a.reshape(geo.b, geo.n_tot, a.shape[-1])
    chunk = lambda d, width: pl.BlockSpec((geo.b, CHUNK, width), lambda j: (0, _chunk_index(geo, d, j), 0))
    gate = lambda d: pl.BlockSpec((1, geo.b, CHUNK, LANES), lambda j: (d, 0, _chunk_index(geo, d, j), 0))
    full = lambda shape: pl.BlockSpec(shape, lambda j: (0,) * len(shape))
    q3, k3, v3 = view(q), view(k), view(v)
    gb4 = gb.reshape(2, geo.b, geo.n_tot, LANES)
    o_f, o_b = pl.pallas_call(
        functools.partial(_gdn_scan_kernel, geo.b),
        grid=(geo.n_tot // CHUNK,),
        in_specs=[chunk(0, MIX_W)] * 3 + [chunk(1, MIX_W)] * 3 + [gate(0), gate(1),
                  full((2, CHUNK, CHUNK)), full((2, N_LEVELS, CHUNK, CHUNK))],
        out_specs=[chunk(0, MIX_W), chunk(1, MIX_W)],
        out_shape=[jax.ShapeDtypeStruct((geo.b, geo.n_tot, MIX_W), F32)] * 2,
        scratch_shapes=[pltpu.VMEM((2, geo.b, N_HEADS, HEAD_DIM, HEAD_DIM), F32)],
        compiler_params=_cparams(("arbitrary",)),
        name="gdn_scan",
    )(q3, k3, v3, q3, k3, v3, gb4, gb4, tri, pair)
    return o_f.reshape(geo.m, MIX_W), o_b.reshape(geo.m, MIX_W)


def _hgrn_consts():
    t = np.arange(CHUNK)
    tt, uu = t[:, None], t[None, :]
    a_all, rowhi, pair = [(uu <= tt)], [], [(uu == tt)]
    for lv in range(N_LEVELS):
        hi = ((t >> lv) & 1) == 1
        same_half = (uu >> lv) == (tt >> lv)
        a_all.append(same_half & np.where(hi[:, None], uu <= tt, uu > tt))
        rowhi.append(np.broadcast_to(hi[:, None], (CHUNK, HEAD_DIM)))
        pair.append(((uu >> (lv + 1)) == (tt >> (lv + 1))) & hi[:, None] & (~hi)[None, :])
    mirror = lambda a: a[::-1, ::-1] if a.shape[1] == CHUNK else a[::-1]
    stack = lambda xs: np.stack([np.stack(xs), np.stack([mirror(x) for x in xs])]).astype(np.float32)
    a = stack(a_all)
    return (jnp.asarray(a.reshape(2, (N_LEVELS + 1) * CHUNK, CHUNK)), jnp.asarray(stack(rowhi)),
            jnp.asarray(stack(pair)))


def _hgrn_scan_kernel(n_batch, q_ref, i_ref, z_ref, lb_ref, a_ref, rowhi_ref, pair_ref, o_ref, st_ref):
    @pl.when(pl.program_id(1) == 0)
    def _():
        st_ref[...] = jnp.zeros_like(st_ref)

    lb = lb_ref[0]
    log_lb = jnp.log(jnp.maximum(lb, LB_MIN))
    log_1mlb = jnp.log1p(-lb)
    a16 = a_ref[0].astype(BF16)
    a16 = jnp.concatenate([a16, a16, a16], axis=1)

    def prep(b):
        z = z_ref[b]
        other = log_1mlb + _log_sigmoid(z)
        log_f = jnp.maximum(log_lb, other) + jnp.log1p(jnp.exp(-jnp.abs(log_lb - other)))
        return dict(b=b, key=(1.0 - lb) * _sigmoid(-z), q=_silu(q_ref[b]) * HEAD_DIM ** -0.5, v=i_ref[b],
                    e_all=_dot_exact01(a16, log_f),
                    ltot=jnp.sum(log_f, axis=0, keepdims=True))

    def levels(pb, h):
        sl = slice(h * HEAD_DIM, (h + 1) * HEAD_DIM)
        q, k, e = pb["q"][:, sl], pb["key"][:, sl], pb["e_all"][:, sl]
        scores = pair_ref[0, 0] * _bdot_nt(q, k)
        for lv in range(N_LEVELS):
            z16 = (jnp.where(rowhi_ref[0, lv] > 0.5, q, k) * jnp.exp(e[(lv + 1) * CHUNK:(lv + 2) * CHUNK])).astype(BF16)
            scores = scores + pair_ref[0, lv + 1] * _dot_nt(z16, z16)
        return dict(b=pb["b"], h=h, sl=sl, q=q, k=k, v=pb["v"][:, sl], bh=e[:CHUNK], bl=pb["ltot"][:, sl],
                    scores=scores)

    def tail(c):
        st = st_ref[c["b"], c["h"]]
        o_ref[0, c["b"], :, c["sl"]] = _bdot(c["scores"], c["v"]) + _bdot_nt(c["q"] * jnp.exp(c["bh"]), st)
        st_ref[c["b"], c["h"]] = st * jnp.exp(c["bl"]) + _bdot_tn(c["v"], c["k"] * jnp.exp(c["bl"] - c["bh"]))

    pending = None
    pb = prep(0)
    for b in range(n_batch):
        pb_next = prep(b + 1) if b + 1 < n_batch else None
        for h in range(N_HEADS):
            cur = levels(pb, h)
            if pending is not None:
                tail(pending)
            pending = cur
        pb = pb_next
    tail(pending)


def _hgrn_scan(geo, p, lb):
    a_all, rowhi, pair = _hgrn_consts()
    p3 = p.reshape(geo.b, geo.n_tot, N_PROJ)
    return pl.pallas_call(
        functools.partial(_hgrn_scan_kernel, geo.b),
        grid=_scan_grid(geo),
        in_specs=[_chunk_spec(geo, MIX_W, COL_HG_Q // MIX_W), _chunk_spec(geo, MIX_W, COL_HG_I // MIX_W),
                  pl.BlockSpec((geo.b, CHUNK, MIX_W),
                               lambda d, j: (0, _chunk_index(geo, d, j), COL_HG_F // MIX_W + d)),
                  _per_dir_spec((1, MIX_W)), _per_dir_spec(((N_LEVELS + 1) * CHUNK, CHUNK)),
                  _per_dir_spec((N_LEVELS, CHUNK, HEAD_DIM)), _per_dir_spec((N_LEVELS + 1, CHUNK, CHUNK))],
        out_specs=_dir_chunk_spec(geo, MIX_W),
        out_shape=jax.ShapeDtypeStruct((2, geo.b, geo.n_tot, MIX_W), F32),
        scratch_shapes=[_state_scratch(geo)],
        compiler_params=_cparams(("parallel", "arbitrary")),
        name="hgrn_scan",
    )(p3, p3, p3, lb.reshape(2, 1, MIX_W), a_all, rowhi, pair)


def _ret_consts(n_lat):
    t = np.arange(CHUNK, dtype=np.float64)
    log_gamma = np.log1p(-np.exp2(-5.0 - np.arange(N_HEADS, dtype=np.float64)))
    dmat, qs, ks = [], [], []
    for rev in (False, True):
        te = t[::-1] if rev else t
        diff = te[:, None] - te[None, :]
        dmat.append(np.stack([np.where(diff >= 0, np.exp(lg * np.maximum(diff, 0)), 0.0) for lg in log_gamma]))
        qs.append(np.repeat(np.exp(log_gamma[None, :] * (te[:, None] + 1.0)), HEAD_DIM, axis=1))
        ks.append(np.repeat(np.exp(log_gamma[None, :] * (CHUNK - 1.0 - te[:, None])), HEAD_DIM, axis=1))
    g_chunk = np.repeat(np.exp(log_gamma * CHUNK), HEAD_DIM)[None, :]
    row = np.repeat(np.arange(n_lat // GRID_W, dtype=np.float32), GRID_W)
    colp = np.tile(np.arange(GRID_W, dtype=np.float32), n_lat // GRID_W)
    n_freq = HEAD_DIM // 4
    inv_freq = (np.float32(ROPE_BASE) ** (-np.arange(n_freq, dtype=np.float32) / n_freq)).astype(np.float32)
    ang = np.concatenate([row[:, None] * inv_freq, colp[:, None] * inv_freq], axis=-1).astype(np.float32)
    cos, sin = np.cos(ang), np.sin(ang)
    f32 = lambda a: jnp.asarray(np.asarray(a, np.float32))
    return (f32(np.stack(dmat)), f32(np.stack(qs)), f32(np.stack(ks)), f32(g_chunk),
            f32(np.concatenate([cos, cos], axis=-1)), f32(np.concatenate([-sin, sin], axis=-1)))


def _ret_scan_kernel(n_batch, nc_ctx, q_ref, k_ref, v_ref, cos_ref, sin_ref, dm_ref, qs_ref, ks_ref, gc_ref,
                     o_ref, s_ref):
    j = pl.program_id(1)

    @pl.when(j == 0)
    def _():
        s_ref[...] = jnp.zeros_like(s_ref)

    is_lat = j >= nc_ctx
    cos = jnp.where(is_lat, cos_ref[...], 1.0)
    sin = jnp.where(is_lat, sin_ref[...], 0.0)
    qs, ks, gc = qs_ref[0], ks_ref[0], gc_ref[...]
    chains = []
    for b in range(n_batch):
        q, k, v = q_ref[b], k_ref[b], v_ref[b]
        for h in range(N_HEADS):
            sl = slice(h * HEAD_DIM, (h + 1) * HEAD_DIM)
            qh, kh = q[:, sl], k[:, sl]
            rq = qh * cos + pltpu.roll(qh, HEAD_DIM // 2, 1) * sin
            rk = (kh * cos + pltpu.roll(kh, HEAD_DIM // 2, 1) * sin) * HEAD_DIM ** -0.5
            chains.append(dict(b=b, h=h, sl=sl, rq=rq, rk=rk, v=v[:, sl], s=s_ref[b, h]))
    for c in chains:
        c["scores"] = _bdot_nt(c["rq"], c["rk"]) * dm_ref[0, c["h"]]
    for c in chains:
        sl = c["sl"]
        o_ref[0, c["b"], :, sl] = _bdot(c["scores"], c["v"]) + _bdot(c["rq"] * qs[:, sl], c["s"])
        s_ref[c["b"], c["h"]] = c["s"] * gc[:, sl] + _bdot_tn(c["rk"] * ks[:, sl], c["v"])


def _ret_scan(geo, p):
    dmat, qs, ks, gc, cos2, sin2 = _ret_consts(geo.n_lat)
    nc_ctx = geo.n_ctx // CHUNK
    p3 = p.reshape(geo.b, geo.n_tot, N_PROJ)

    def rope_blk(d, j):
        return (jnp.where(j < nc_ctx, 0, _chunk_index(geo, d, j)), 0)

    return pl.pallas_call(
        functools.partial(_ret_scan_kernel, geo.b, nc_ctx),
        grid=_scan_grid(geo),
        in_specs=[_chunk_spec(geo, MIX_W, COL_RET_Q // MIX_W), _chunk_spec(geo, MIX_W, COL_RET_K // MIX_W),
                  _chunk_spec(geo, MIX_W, COL_RET_V // MIX_W),
                  pl.BlockSpec((CHUNK, HEAD_DIM), rope_blk), pl.BlockSpec((CHUNK, HEAD_DIM), rope_blk),
                  _per_dir_spec((N_HEADS, CHUNK, CHUNK)), _per_dir_spec((CHUNK, MIX_W)),
                  _per_dir_spec((CHUNK, MIX_W)), pl.BlockSpec((1, MIX_W), lambda d, j: (0, 0))],
        out_specs=_dir_chunk_spec(geo, MIX_W),
        out_shape=jax.ShapeDtypeStruct((2, geo.b, geo.n_tot, MIX_W), F32),
        scratch_shapes=[_state_scratch(geo)],
        compiler_params=_cparams(("parallel", "arbitrary")),
        name="ret_scan",
    )(p3, p3, p3, cos2, sin2, dmat, qs, ks, gc)


def _lru_kernel(geo, tr, rev, prev_ref, x_ref, next_ref, cw_ref, cb_ref, wg_ref, bg_ref, lam_ref,
                o_ref, h_ref):
    b, j = pl.program_id(0), pl.program_id(1)

    @pl.when(j == 0)
    def _():
        h_ref[...] = jnp.zeros_like(h_ref)

    first, last = geo.seq_edges(_lru_block(geo, tr, rev, b, j), tr)
    xb = _token_conv(prev_ref, x_ref, next_ref, cw_ref, first, last) + cb_ref[...]
    gates = _bdot(xb, wg_ref[...]) + bg_ref[...]
    gate_r = _sigmoid(gates[:, :MIX_W])
    gate_i = _sigmoid(gates[:, MIX_W:])
    log_a = LRU_C * gate_r * _log_sigmoid(lam_ref[...])
    a = jnp.exp(log_a)
    u = xb * gate_i * jnp.sqrt(jnp.maximum(-jnp.tanh(log_a) * (a * a + 1.0), SQRT_MIN))
    ng = tr // SUBLANES
    a3 = a.reshape(ng, SUBLANES, MIX_W)
    u3 = u.reshape(ng, SUBLANES, MIX_W)
    sub = lax.broadcasted_iota(jnp.int32, (ng, SUBLANES, MIX_W), 1)
    step = 1
    while step < SUBLANES:
        shift = SUBLANES - step if rev else step
        valid = (sub < SUBLANES - step) if rev else (sub >= step)
        u3 = jnp.where(valid, a3 * pltpu.roll(u3, shift, 1) + u3, u3)
        a3 = jnp.where(valid, a3 * pltpu.roll(a3, shift, 1), a3)
        step *= 2
    edge = 0 if rev else SUBLANES - 1
    h_in = jnp.broadcast_to(h_ref[...], (SUBLANES, MIX_W))
    h_groups = [None] * ng
    for g in (range(ng - 1, -1, -1) if rev else range(ng)):
        h_groups[g] = h_in
        h_in = a3[g, edge:edge + 1] * h_in + u3[g, edge:edge + 1]
    h = a3 * jnp.stack(h_groups, axis=0) + u3
    o_ref[...] = h.reshape(tr, MIX_W)
    h_ref[...] = h_in[0:1]


def _lru_block(geo, tr, rev, b, j):
    nb_ctx, nb_lat = geo.n_ctx // tr, geo.n_lat // tr
    jc = nb_ctx - 1 - j if rev else j
    jl = nb_lat - 1 - (j - nb_ctx) if rev else j - nb_ctx
    return b * (geo.n_tot // tr) + jnp.where(j < nb_ctx, nb_lat + jc, jl)


def _lru_scan(geo, p, conv_w, conv_b, wa, ba, wx, bx, lam, rev):
    tr = _largest_tile(geo.row_tile, 256)
    d = 1 if rev else 0
    blockdiag = lambda w: jax.scipy.linalg.block_diag(*[w[n] for n in range(LRU_BLOCKS)])
    wg = jnp.concatenate([blockdiag(wa[d]), blockdiag(wx[d])], axis=1).astype(BF16)
    bg = jnp.concatenate([ba[d], bx[d]]).astype(F32).reshape(1, 2 * MIX_W)
    nb8, r8 = geo.m // SUBLANES, tr // SUBLANES
    cblk = COL_LRU_X // MIX_W
    blk = lambda b, j: _lru_block(geo, tr, rev, b, j)
    full = lambda shape: pl.BlockSpec(shape, lambda b, j: (0,) * len(shape))
    return pl.pallas_call(
        functools.partial(_lru_kernel, geo, tr, rev),
        grid=(geo.b, (geo.n_ctx + geo.n_lat) // tr),
        in_specs=[pl.BlockSpec((SUBLANES, MIX_W), lambda b, j: (jnp.maximum(blk(b, j) * r8 - 1, 0), cblk)),
                  pl.BlockSpec((tr, MIX_W), lambda b, j: (blk(b, j), cblk)),
                  pl.BlockSpec((SUBLANES, MIX_W),
                               lambda b, j: (jnp.minimum((blk(b, j) + 1) * r8, nb8 - 1), cblk)),
                  full((CONV_W, MIX_W)), full((1, MIX_W)), full((MIX_W, 2 * MIX_W)),
                  full((1, 2 * MIX_W)), full((1, MIX_W))],
        out_specs=pl.BlockSpec((tr, MIX_W), lambda b, j: (blk(b, j), 0)),
        out_shape=jax.ShapeDtypeStruct((geo.m, MIX_W), F32),
        scratch_shapes=[pltpu.VMEM((1, MIX_W), F32)],
        compiler_params=_cparams(("parallel", "arbitrary")),
        name="lru_scan_bwd" if rev else "lru_scan_fwd",
    )(p, p, p, conv_w.astype(F32), conv_b.astype(F32).reshape(1, MIX_W), wg, bg,
      lam[d].astype(F32).reshape(1, MIX_W))


def _gelu_tanh(x):
    return 0.5 * x * (1.0 + jnp.tanh(math.sqrt(2.0 / math.pi) * (x + 0.044715 * (x * x * x))))


def _gated_head_norm(o, g, z):
    parts = []
    for h in range(N_HEADS):
        oh = o[:, h * HEAD_DIM:(h + 1) * HEAD_DIM]
        parts.append(oh * lax.rsqrt(jnp.mean(oh * oh, axis=-1, keepdims=True) + EPS) * g)
    return jnp.concatenate(parts, axis=-1) * _silu(z)


def _merge_kernel(ogf_ref, ogb_ref, oh_ref, olf_ref, olb_ref, or_ref, gz_ref, hg_ref, ly_ref, rg_ref, mg_ref,
                  x_ref, gt_ref, gng_ref, hng_ref, rng_ref, wb_ref, wo_ref, out_ref):
    branches = (
        _gated_head_norm(ogf_ref[...] + ogb_ref[...], gng_ref[...], gz_ref[...]),
        _gated_head_norm(oh_ref[0] + oh_ref[1], hng_ref[...], hg_ref[...]),
        (olf_ref[...] + olb_ref[...]) * _gelu_tanh(ly_ref[...]),
        _gated_head_norm(or_ref[0] + or_ref[1], rng_ref[...], rg_ref[...]),
    )
    mix = None
    for n, y in enumerate(branches):
        term = _sigmoid(mg_ref[:, n * D_MODEL:(n + 1) * D_MODEL]) * _bdot(y, wb_ref[n])
        mix = term if mix is None else mix + term
    out_ref[...] = x_ref[...] + gt_ref[0] * _bdot(mix, wo_ref[...])


def _merge(geo, compact, x_rows, p, o_gdn, o_hgrn, o_lru_f, o_lru_b, o_ret, mods3, gdn_g, hgrn_g, ret_g,
           w_branch, w_out):
    tm = _largest_tile(geo.row_tile, 256)
    n_rows = geo.m_lat if compact else geo.m
    src = lambda i: geo.full_tile(i, tm, compact)
    both = pl.BlockSpec((2, tm, MIX_W), lambda i: (0, src(i), 0))
    rows = lambda width: pl.BlockSpec((tm, width), lambda i: (src(i), 0))
    col = lambda c, width: pl.BlockSpec((tm, width), lambda i: (src(i), c // width))
    full = lambda shape: pl.BlockSpec(shape, lambda i: (0,) * len(shape))
    hd = lambda g: g.astype(F32).reshape(1, HEAD_DIM)
    return pl.pallas_call(
        _merge_kernel,
        grid=(n_rows // tm,),
        in_specs=[rows(MIX_W), rows(MIX_W), both, rows(MIX_W), rows(MIX_W), both,
                  col(COL_GDN_Z, MIX_W), col(COL_HG_G, MIX_W), col(COL_LRU_Y, MIX_W), col(COL_RET_G, MIX_W),
                  col(COL_MERGE, N_BRANCH * D_MODEL), rows(D_MODEL),
                  pl.BlockSpec((1, 1, D_MODEL), lambda i: (geo.mod_row(i, tm, compact), 0, 2)),
                  full((1, HEAD_DIM)), full((1, HEAD_DIM)), full((1, HEAD_DIM)),
                  full((N_BRANCH, MIX_W, D_MODEL)), full((D_MODEL, D_MODEL))],
        out_specs=pl.BlockSpec((tm, D_MODEL), lambda i: (i, 0)),
        out_shape=jax.ShapeDtypeStruct((n_rows, D_MODEL), F32),
        compiler_params=_cparams(("parallel",)),
        name="merge",
    )(o_gdn[0], o_gdn[1], o_hgrn, o_lru_f, o_lru_b, o_ret, p, p, p, p, p, x_rows, mods3, hd(gdn_g), hd(hgrn_g),
      hd(ret_g), w_branch.astype(BF16), w_out.astype(BF16))


ROW_TILE = D_MODEL // LANES
assert ROW_TILE == SUBLANES


def _store_row_tiles(ref, val):
    n = val.shape[0]
    v = jnp.stack([val[:, s * LANES:(s + 1) * LANES].reshape(n // SUBLANES, SUBLANES, LANES)
                   for s in range(ROW_TILE)], axis=1)
    ref[...] = jnp.swapaxes(v, 1, 2).reshape(n * ROW_TILE, LANES)


def _load_row_tiles(ref_at, start, n):
    v = ref_at[pl.ds(start, n * ROW_TILE), :].reshape(n // SUBLANES, SUBLANES, ROW_TILE, LANES)
    v = jnp.swapaxes(v, 1, 2)
    return jnp.concatenate([v[:, s].reshape(n, LANES) for s in range(ROW_TILE)], axis=1)


def _router_kernel(n_experts, x_ref, g_ref, sc_ref, sh_ref, rw_ref, rb_ref, f_ref, idx_ref, wt_ref):
    f = _rms_mod(x_ref[...], g_ref[...], sc_ref[0], sh_ref[0])
    _store_row_tiles(f_ref, f)
    logits = _dot(f, rw_ref[...], precision=HIGHEST) + rb_ref[...]
    lane = lax.broadcasted_iota(jnp.int32, logits.shape, 1)
    logits = jnp.where(lane < n_experts, logits, NEG_BIG)
    vals, idxs = [], []
    for _ in range(TOP_K):
        mx = jnp.max(logits, axis=-1, keepdims=True)
        ix = jnp.min(jnp.where(logits == mx, lane, LANES), axis=-1, keepdims=True)
        vals.append(mx)
        idxs.append(ix)
        logits = jnp.where(lane == ix, -jnp.inf, logits)
    ex = [jnp.exp(vl - vals[0]) for vl in vals]
    tot = ex[0] + ex[1] + ex[2] + ex[3]
    idx_out = jnp.zeros(lane.shape, jnp.int32)
    wt_out = jnp.zeros(lane.shape, F32)
    for kk in range(TOP_K):
        idx_out = jnp.where(lane == kk, idxs[kk], idx_out)
        wt_out = jnp.where(lane == kk, ex[kk] / tot, wt_out)
    idx_ref[...] = idx_out
    wt_ref[...] = wt_out


def _router(geo, compact, x_rows, g, mods3, router_w, router_b):
    n_rows, d = x_rows.shape
    n_experts = router_w.shape[1]
    tm = _largest_tile(geo.row_tile, 256)
    rw = jnp.pad(router_w.astype(F32), ((0, 0), (0, LANES - n_experts)))
    rb = jnp.pad(router_b.astype(F32), (0, LANES - n_experts)).reshape(1, LANES)
    rows = lambda width: pl.BlockSpec((tm, width), lambda i: (i, 0))
    full = lambda shape: pl.BlockSpec(shape, lambda i: (0,) * len(shape))
    return pl.pallas_call(
        functools.partial(_router_kernel, n_experts),
        grid=(n_rows // tm,),
        in_specs=[rows(d), full((1, d)),
                  pl.BlockSpec((1, 1, d), lambda i: (geo.mod_row(i, tm, compact), 0, 4)),
                  pl.BlockSpec((1, 1, d), lambda i: (geo.mod_row(i, tm, compact), 0, 3)),
                  full((d, LANES)), full((1, LANES))],
        out_specs=[pl.BlockSpec((tm * ROW_TILE, LANES), lambda i: (i, 0)), rows(LANES), rows(LANES)],
        out_shape=[jax.ShapeDtypeStruct((n_rows * ROW_TILE, LANES), F32),
                   jax.ShapeDtypeStruct((n_rows, LANES), jnp.int32), jax.ShapeDtypeStruct((n_rows, LANES), F32)],
        compiler_params=_cparams(("parallel",)),
        name="router",
    )(x_rows, g.astype(F32).reshape(1, d), mods3, mods3, rw, rb)


def _route_plan(top_idx, n_experts):
    m = top_idx.shape[0]
    n_assign = m * TOP_K
    flat_e = top_idx.reshape(-1)
    order = jnp.argsort(flat_e).astype(jnp.int32)
    rank = jnp.argsort(order).astype(jnp.int32)
    counts = jnp.sum(flat_e[:, None] == jnp.arange(n_experts, dtype=jnp.int32)[None, :], axis=0, dtype=jnp.int32)
    padded = (counts + ROUTE_BLOCK - 1) // ROUTE_BLOCK * ROUTE_BLOCK
    pad_end = jnp.cumsum(padded)
    pad_start = pad_end - padded
    start = jnp.cumsum(counts) - counts
    n_blocks = -(-n_assign // ROUTE_BLOCK) + n_experts
    cap = n_blocks * ROUTE_BLOCK
    block_row0 = jnp.arange(n_blocks, dtype=jnp.int32) * ROUTE_BLOCK
    block_expert = jnp.minimum(jnp.sum(pad_end[None, :] <= block_row0[:, None], axis=1, dtype=jnp.int32),
                               n_experts - 1)
    n_used = (pad_end[-1:] // ROUTE_BLOCK).astype(jnp.int32)
    row = jnp.arange(cap, dtype=jnp.int32)
    row_e = jnp.repeat(block_expert, ROUTE_BLOCK)
    within = row - pad_start[row_e]
    valid = jnp.logical_and(within >= 0, within < counts[row_e])
    src_tok = jnp.where(valid, order[jnp.clip(start[row_e] + within, 0, n_assign - 1)] // TOP_K, 0)
    dest = pad_start[flat_e] + rank - start[flat_e]
    return src_tok.reshape(n_blocks, 1, ROUTE_BLOCK), block_expert, n_used, dest


def _start_row_gather(idx_ref, n, src_hbm, buf, slot, sem):
    for r in range(n):
        src_row = pl.multiple_of(idx_ref[0, 0, r] * ROW_TILE, ROW_TILE)
        pltpu.make_async_copy(src_hbm.at[pl.ds(src_row, ROW_TILE)], buf.at[slot, pl.ds(r * ROW_TILE, ROW_TILE)],
                              sem.at[slot]).start(priority=r % 2)


def _wait_row_gather(n, src_hbm, buf, slot, sem):
    pltpu.make_async_copy(src_hbm.at[pl.ds(0, n * ROW_TILE)], buf.at[slot], sem.at[slot]).wait()


def _gather_step(cur_ref, nxt_ref, n, src_hbm, buf, sem):
    i, nb = pl.program_id(0), pl.num_programs(0)
    slot = i % 2

    @pl.when(i == 0)
    def _():
        _start_row_gather(cur_ref, n, src_hbm, buf, 0, sem)

    _wait_row_gather(n, src_hbm, buf, slot, sem)
    _start_row_gather(nxt_ref, n, src_hbm, buf, 1 - slot, sem)
    return slot


def _gather_drain(n, src_hbm, buf, slot, sem):
    @pl.when(pl.program_id(0) == pl.num_programs(0) - 1)
    def _():
        _wait_row_gather(n, src_hbm, buf, 1 - slot, sem)


def _expert_kernel(be_ref, nu_ref, src_ref, nxt_ref, f_hbm, w1_ref, b1_ref, w2_ref, b2_ref, y_ref,
                   buf, sem, w1b_ref, w2b_ref):
    i = pl.program_id(0)
    ff = w2_ref.shape[1]
    slot = _gather_step(src_ref, nxt_ref, ROUTE_BLOCK, f_hbm, buf, sem)

    @pl.when(jnp.logical_or(i == 0, be_ref[i] != be_ref[jnp.maximum(i - 1, 0)]))
    def _():
        w1b_ref[...] = w1_ref[0].astype(BF16)
        w2b_ref[...] = w2_ref[0].astype(BF16)

    @pl.when(i >= nu_ref[0])
    def _():
        y_ref[...] = jnp.zeros_like(y_ref)

    @pl.when(i < nu_ref[0])
    def _():
        z = _bdot(_load_row_tiles(buf.at[slot], 0, ROUTE_BLOCK), w1b_ref[...]) + b1_ref[0]
        glu = jnp.minimum(z[:, :ff], SWIGLU_LIMIT)
        lin = jnp.clip(z[:, ff:], -SWIGLU_LIMIT, SWIGLU_LIMIT)
        act = glu * _sigmoid(SWIGLU_ALPHA * glu) * (lin + 1.0)
        _store_row_tiles(y_ref, _bdot(act, w2b_ref[...]) + b2_ref[0])

    _gather_drain(ROUTE_BLOCK, f_hbm, buf, slot, sem)


def _experts(f_tiles, src_tok, block_expert, n_used, layer, w1_all, b1_all, w2_all, b2_all):
    n_blocks = src_tok.shape[0]
    n_layers, n_experts, d, ff2 = w1_all.shape
    ff = w2_all.shape[2]
    w1 = w1_all.reshape(n_layers * n_experts, d, ff2)
    w2 = w2_all.reshape(n_layers * n_experts, ff, d)
    b1 = b1_all.reshape(n_layers * n_experts, ff2)
    b2 = b2_all.reshape(n_layers * n_experts, d)
    block_expert = block_expert + layer * n_experts
    n_experts = n_layers * n_experts
    grid_spec = pltpu.PrefetchScalarGridSpec(
        num_scalar_prefetch=2,
        grid=(n_blocks,),
        in_specs=[pl.BlockSpec((1, 1, ROUTE_BLOCK), lambda i, be, nu: (i, 0, 0), memory_space=pltpu.SMEM),
                  pl.BlockSpec((1, 1, ROUTE_BLOCK), lambda i, be, nu: (jnp.minimum(i + 1, n_blocks - 1), 0, 0),
                               memory_space=pltpu.SMEM),
                  pl.BlockSpec(memory_space=pl.ANY),
                  pl.BlockSpec((1, d, ff2), lambda i, be, nu: (be[i], 0, 0)),
                  pl.BlockSpec((1, 1, ff2), lambda i, be, nu: (be[i], 0, 0)),
                  pl.BlockSpec((1, ff, d), lambda i, be, nu: (be[i], 0, 0)),
                  pl.BlockSpec((1, 1, d), lambda i, be, nu: (be[i], 0, 0))],
        out_specs=pl.BlockSpec((ROUTE_BLOCK * ROW_TILE, LANES), lambda i, be, nu: (i, 0)),
        scratch_shapes=[pltpu.VMEM((2, ROUTE_BLOCK * ROW_TILE, LANES), F32), pltpu.SemaphoreType.DMA((2,)),
                        pltpu.VMEM((d, ff2), BF16), pltpu.VMEM((ff, d), BF16)],
    )
    return pl.pallas_call(
        _expert_kernel,
        grid_spec=grid_spec,
        out_shape=jax.ShapeDtypeStruct((n_blocks * ROUTE_BLOCK * ROW_TILE, LANES), F32),
        compiler_params=_cparams(("arbitrary",)),
        name="experts",
    )(block_expert, n_used, src_tok, src_tok, f_tiles, w1, b1.astype(F32).reshape(n_experts, 1, ff2),
      w2, b2.astype(F32).reshape(n_experts, 1, d))


def _combine_kernel(tc, final, dst_ref, nxt_ref, y_hbm, x_ref, wt_ref, gt_ref, fg_ref, out_ref, buf, sem):
    n = TOP_K * tc
    slot = _gather_step(dst_ref, nxt_ref, n, y_hbm, buf, sem)
    wt = wt_ref[...]
    acc = None
    for kk in range(TOP_K):
        term = wt[:, kk:kk + 1] * _load_row_tiles(buf.at[slot], kk * tc * ROW_TILE, tc)
        acc = term if acc is None else acc + term
    out = x_ref[...] + gt_ref[0] * acc
    if final:
        out = out * lax.rsqrt(jnp.mean(out * out, axis=-1, keepdims=True) + EPS) * fg_ref[...]
    out_ref[...] = out
    _gather_drain(n, y_hbm, buf, slot, sem)


def _combine(geo, y_pad, dest, x_rows, wt, mods3, final_g, final):
    n_rows, d = x_rows.shape
    tc = _largest_tile(geo.row_tile, 128)
    nt = n_rows // tc
    dst = dest.reshape(nt, tc, TOP_K).transpose(0, 2, 1).reshape(nt, 1, TOP_K * tc)
    rows = lambda width: pl.BlockSpec((tc, width), lambda i: (i, 0))
    return pl.pallas_call(
        functools.partial(_combine_kernel, tc, final),
        grid=(nt,),
        in_specs=[pl.BlockSpec((1, 1, TOP_K * tc), lambda i: (i, 0, 0), memory_space=pltpu.SMEM),
                  pl.BlockSpec((1, 1, TOP_K * tc), lambda i: (jnp.minimum(i + 1, nt - 1), 0, 0),
                               memory_space=pltpu.SMEM),
                  pl.BlockSpec(memory_space=pl.ANY), rows(d), rows(LANES),
                  pl.BlockSpec((1, 1, d), lambda i: (geo.mod_row(i, tc, final), 0, 5)),
                  pl.BlockSpec((1, d), lambda i: (0, 0))],
        out_specs=rows(d),
        out_shape=jax.ShapeDtypeStruct((n_rows, d), F32),
        scratch_shapes=[pltpu.VMEM((2, TOP_K * tc * ROW_TILE, LANES), F32), pltpu.SemaphoreType.DMA((2,))],
        compiler_params=_cparams(("arbitrary",)),
        name="moe_combine",
    )(dst, dst, y_pad, x_rows, wt, mods3, final_g.astype(F32).reshape(1, d))


def _moe(geo, x_rows, g, mods3, router_w, router_b, layer, w1_all, b1_all, w2_all, b2_all, final_g, final):
    f_tiles, top_idx, top_w = _router(geo, final, x_rows, g, mods3, router_w, router_b)
    src_tok, block_expert, n_used, dest = _route_plan(top_idx[:, :TOP_K], router_w.shape[1])
    y_tiles = _experts(f_tiles, src_tok, block_expert, n_used, layer, w1_all, b1_all, w2_all, b2_all)
    return _combine(geo, y_tiles, dest, x_rows, top_w, mods3, final_g, final)


def kernel(x, c, ctx, c_ctx, w_mod, b_mod, norm1_g, norm2_g, w_in, gdn_conv_w, gdn_a_log, gdn_dt_bias, gdn_norm_g, hgrn_lb, hgrn_norm_g, lru_conv_w, lru_conv_b, lru_wa, lru_ba, lru_wx, lru_bx, lru_lambda, ret_norm_g, w_branch, w_out, router_w, router_b, moe_w1, moe_b1, moe_w2, moe_b2, final_norm_g):
    b, n_lat, d = x.shape
    n_ctx = ctx.shape[1]
    depth = w_mod.shape[0]
    assert d == D_MODEL and w_branch.shape[2] == MIX_W
    geo = _Geo(b, n_lat, n_ctx)
    lb_soft = jax.nn.softmax(hgrn_lb.astype(F32), axis=0)
    lower_bounds = jnp.clip(jnp.cumsum(lb_soft, axis=0) - lb_soft[0], 0.0, LB_MAX)
    n_cond = -(-(b + 1) // SUBLANES) * SUBLANES
    cc = jnp.concatenate([c, c_ctx[None, :], jnp.zeros((n_cond - b - 1, d), c.dtype)], axis=0)
    mods = _modulation(cc.astype(F32), w_mod, b_mod)
    rows = jnp.concatenate([x, ctx], axis=1).reshape(geo.m, d)
    flat = lambda o: o.reshape(2, geo.m, MIX_W)
    for l in range(depth):
        last = l == depth - 1
        mods3 = mods[l].reshape(n_cond, 1, N_MOD * d)
        p = _norm_proj(geo, rows, norm1_g[l], mods3, _reorder_w_in(w_in[l]))
        gq, gk, gv, gb = _gdn_prep(geo, p, gdn_conv_w[l], gdn_a_log[l], gdn_dt_bias[l])
        o_gdn = _gdn_scan(geo, gq, gk, gv, gb)
        o_hgrn = flat(_hgrn_scan(geo, p, lower_bounds[l]))
        lru_args = (geo, p, lru_conv_w[l], lru_conv_b[l], lru_wa[l], lru_ba[l], lru_wx[l], lru_bx[l],
                    lru_lambda[l])
        o_lru_f = _lru_scan(*lru_args, rev=False)
        o_lru_b = _lru_scan(*lru_args, rev=True)
        o_ret = flat(_ret_scan(geo, p))
        rows = _merge(geo, last, rows, p, o_gdn, o_hgrn, o_lru_f, o_lru_b, o_ret, mods3, gdn_norm_g[l],
                      hgrn_norm_g[l], ret_norm_g[l], w_branch[l], w_out[l])
        rows = _moe(geo, rows, norm2_g[l], mods3, router_w[l], router_b[l], l, moe_w1, moe_b1, moe_w2, moe_b2,
                    final_norm_g, last)
    return rows.reshape(b, n_lat, d)
```

```python
import functools
import math

import numpy as np
import jax
import jax.numpy as jnp
from jax import lax
from jax.experimental import pallas as pl
from jax.experimental.pallas import tpu as pltpu

F32 = jnp.float32
BF16 = jnp.bfloat16
HIGHEST = lax.Precision.HIGHEST

GRID_W = 64
N_BRANCH = 4
HEAD_DIM = 128
LRU_BLOCKS = 8
LRU_C = 8.0
CONV_W = 4
CHUNK = 64
ROPE_BASE = 10000.0
TOP_K = 4
SWIGLU_ALPHA = 1.702
SWIGLU_LIMIT = 7.0
N_MOD = 6
EPS = 1e-6
LB_MIN = 1e-30
LB_MAX = 1.0 - 1e-4
SQRT_MIN = 1e-12

LANES = 128
SUBLANES = 8
VMEM_LIMIT_BYTES = 56 * 1024 * 1024
ROUTE_BLOCK = 256
CAST_STEPS = 8
NEG_BIG = -1e30


def _cparams(semantics):
    return pltpu.CompilerParams(dimension_semantics=semantics, vmem_limit_bytes=VMEM_LIMIT_BYTES)


def _largest_tile(n, cap, quantum=SUBLANES):
    best = None
    for t in range(quantum, min(n, cap) + 1, quantum):
        if n % t == 0:
            best = t
    assert best is not None, (n, cap)
    return best


def _dot(a, b, precision=None):
    return jnp.dot(a, b, preferred_element_type=F32, precision=precision)


def _dot_nt(a, b):
    return lax.dot_general(a, b, (((1,), (1,)), ((), ())), preferred_element_type=F32)


def _dot_tn(a, b):
    return lax.dot_general(a, b, (((0,), (0,)), ((), ())), preferred_element_type=F32)


def _bdot(a, b):
    return _dot(a.astype(BF16), b.astype(BF16))


def _bdot_nt(a, b):
    return _dot_nt(a.astype(BF16), b.astype(BF16))


def _bdot_tn(a, b):
    return _dot_tn(a.astype(BF16), b.astype(BF16))


def _dot_exact01(a01x3, x):
    hi = x.astype(BF16)
    r1 = x - hi.astype(F32)
    mid = r1.astype(BF16)
    lo = (r1 - mid.astype(F32)).astype(BF16)
    return _dot(a01x3, jnp.concatenate([hi, mid, lo], axis=0))


def _sigmoid(x):
    return 1.0 / (1.0 + jnp.exp(-x))


def _silu(x):
    return x * _sigmoid(x)


def _softplus(x):
    return jnp.maximum(x, 0.0) + jnp.log1p(jnp.exp(-jnp.abs(x)))


def _log_sigmoid(x):
    return -_softplus(-x)


def _mod_kernel(c_ref, w_ref, b_ref, o_ref):
    o_ref[0] = _dot(_silu(c_ref[...]), w_ref[0], precision=HIGHEST) + b_ref[0]


def _modulation(cc, w_mod, b_mod):
    n_layers, d, n = w_mod.shape
    rows = cc.shape[0]
    tn = _largest_tile(n, 1024, LANES)
    return pl.pallas_call(
        _mod_kernel,
        grid=(n_layers, n // tn),
        in_specs=[pl.BlockSpec((rows, d), lambda l, j: (0, 0)),
                  pl.BlockSpec((1, d, tn), lambda l, j: (l, 0, j)),
                  pl.BlockSpec((1, 1, tn), lambda l, j: (l, 0, j))],
        out_specs=pl.BlockSpec((1, rows, tn), lambda l, j: (l, 0, j)),
        out_shape=jax.ShapeDtypeStruct((n_layers, rows, n), F32),
        compiler_params=_cparams(("parallel", "parallel")),
        name="modulation",
    )(cc, w_mod, b_mod.reshape(n_layers, 1, n))


D_MODEL = 1024
MIX_W = 512
N_HEADS = MIX_W // HEAD_DIM
COL_MERGE = 0
COL_GDN_Z = 4096
COL_GDN_QKV = 4608
COL_HG_Q = 6144
COL_HG_I = 6656
COL_HG_G = 7168
COL_LRU_X = 7680
COL_LRU_Y = 8192
COL_RET_Q = 8704
COL_RET_K = 9216
COL_RET_V = 9728
COL_RET_G = 10240
N_PROJ = 10752
CCOL_HG_F = 0
CCOL_AB = 1024
N_CRIT = 1280


class _Geo:
    def __init__(self, b, n_lat, n_ctx):
        self.b, self.n_lat, self.n_ctx = b, n_lat, n_ctx
        self.n_tot = n_lat + n_ctx
        self.m_lat = b * n_lat
        self.m = b * self.n_tot
        assert n_lat % CHUNK == 0 and n_ctx % CHUNK == 0 and n_lat % GRID_W == 0
        self.row_tile = math.gcd(n_lat, n_ctx)

    def mod_row(self, i, tm, compact=False):
        r0 = i * tm
        if compact:
            return r0 // self.n_lat
        return jnp.where(r0 % self.n_tot < self.n_lat, r0 // self.n_tot, self.b)

    def full_tile(self, i, tm, compact=False):
        if not compact:
            return i
        per_lat, per_tot = self.n_lat // tm, self.n_tot // tm
        return (i // per_lat) * per_tot + i % per_lat

    def seq_edges(self, i, tr):
        pos = (i * tr) % self.n_tot
        first = jnp.logical_or(pos == 0, pos == self.n_lat)
        last = jnp.logical_or(pos + tr == self.n_lat, pos + tr == self.n_tot)
        return first, last


def _reorder_w_in(w_in):
    w = w_in.astype(BF16)
    d = w.shape[0]
    pad = lambda n: jnp.zeros((d, n), BF16)
    ab = [jnp.concatenate([w[:, 2048 + 4 * dr:2052 + 4 * dr], w[:, 2056 + 4 * dr:2060 + 4 * dr],
                           pad(LANES - 8)], axis=1) for dr in range(2)]
    main = jnp.concatenate([w[:, 7696:11792], w[:, 1536:2048], w[:, 0:1536], w[:, 2064:3088],
                            w[:, 4112:7696]], axis=1)
    crit = jnp.concatenate([w[:, 3088:4112], ab[0], ab[1]], axis=1)
    assert main.shape[1] == N_PROJ and crit.shape[1] == N_CRIT
    return main, crit


def _rms_mod(x, g, sc, sh):
    y = x * lax.rsqrt(jnp.mean(x * x, axis=-1, keepdims=True) + EPS) * g
    return y * (1.0 + sc) + sh


def _norm_kernel(x_ref, g_ref, sc_ref, sh_ref, h_ref):
    h_ref[...] = _rms_mod(x_ref[...], g_ref[...], sc_ref[0], sh_ref[0]).astype(BF16)


def _proj_kernel(h_ref, w_ref, o_ref):
    o_ref[...] = _dot(h_ref[...], w_ref[...]).astype(o_ref.dtype)


def _proj(h, w_p, out_dtype):
    m, d = h.shape
    n = w_p.shape[1]
    tm = _largest_tile(m, 1024)
    tn = _largest_tile(n, 2048, LANES)
    return pl.pallas_call(
        _proj_kernel,
        grid=(n // tn, m // tm),
        in_specs=[pl.BlockSpec((tm, d), lambda j, i: (i, 0)),
                  pl.BlockSpec((d, tn), lambda j, i: (0, j))],
        out_specs=pl.BlockSpec((tm, tn), lambda j, i: (i, j)),
        out_shape=jax.ShapeDtypeStruct((m, n), out_dtype),
        compiler_params=_cparams(("parallel", "parallel")),
        name="in_proj",
    )(h, w_p)


def _norm_proj(geo, x_rows, g, mods3, w_main, w_crit):
    m, d = x_rows.shape
    tr = _largest_tile(geo.row_tile, 512)
    h = pl.pallas_call(
        _norm_kernel,
        grid=(m // tr,),
        in_specs=[pl.BlockSpec((tr, d), lambda i: (i, 0)),
                  pl.BlockSpec((1, d), lambda i: (0, 0)),
                  pl.BlockSpec((1, 1, d), lambda i: (geo.mod_row(i, tr), 0, 1)),
                  pl.BlockSpec((1, 1, d), lambda i: (geo.mod_row(i, tr), 0, 0))],
        out_specs=pl.BlockSpec((tr, d), lambda i: (i, 0)),
        out_shape=jax.ShapeDtypeStruct((m, d), BF16),
        compiler_params=_cparams(("parallel",)),
        name="norm1",
    )(x_rows, g.reshape(1, d), mods3, mods3)
    return _proj(h, w_main, BF16), _proj(h, w_crit, F32)


HALO = 16


def _halo_specs(geo, tr, width, col_block, row_block=lambda i: i):
    nbh = geo.m // HALO
    rh = tr // HALO
    return [pl.BlockSpec((HALO, width), lambda *g: (jnp.maximum(row_block(*g) * rh - 1, 0), col_block)),
            pl.BlockSpec((tr, width), lambda *g: (row_block(*g), col_block)),
            pl.BlockSpec((HALO, width), lambda *g: (jnp.minimum((row_block(*g) + 1) * rh, nbh - 1), col_block))]


def _token_conv(prev_ref, x_ref, next_ref, w_ref, first, last):
    x = x_ref[...].astype(F32)
    tr = x.shape[0]
    prev = jnp.where(first, 0.0, prev_ref[...].astype(F32))
    nxt = jnp.where(last, 0.0, next_ref[...].astype(F32))
    ext = jnp.concatenate([prev, x, nxt], axis=0)
    w = w_ref[...]
    y = None
    for j in range(CONV_W):
        off = HALO - CONV_W // 2 + j
        term = ext[off:off + tr] * w[j:j + 1]
        y = term if y is None else y + term
    return y


def _gdn_prep_kernel(geo, tr, prev_ref, x_ref, next_ref, ab0_ref, ab1_ref, cw_ref, alog_ref, dtb_ref,
                     q_ref, k_ref, v_ref, gb_ref):
    first, last = geo.seq_edges(pl.program_id(0), tr)
    s = _silu(_token_conv(prev_ref, x_ref, next_ref, cw_ref, first, last))
    for h in range(N_HEADS):
        sl = slice(h * HEAD_DIM, (h + 1) * HEAD_DIM)
        qh = s[:, sl]
        kh = s[:, MIX_W + h * HEAD_DIM:MIX_W + (h + 1) * HEAD_DIM]
        q_ref[:, sl] = qh * (lax.rsqrt(jnp.sum(qh * qh, axis=-1, keepdims=True) + EPS) * HEAD_DIM ** -0.5)
        k_ref[:, sl] = kh * lax.rsqrt(jnp.sum(kh * kh, axis=-1, keepdims=True) + EPS)
    v_ref[...] = s[:, 2 * MIX_W:]
    lane = lax.broadcasted_iota(jnp.int32, (tr, LANES), 1)
    for dr, ab_ref in enumerate((ab0_ref, ab1_ref)):
        a = ab_ref[...]
        g = -jnp.exp(alog_ref[dr]) * _softplus(a + dtb_ref[dr])
        gb_ref[dr] = jnp.where(lane < N_HEADS, g, _sigmoid(a))


def _gdn_prep(geo, p, pc, conv_w, a_log, dt_bias):
    tr = _largest_tile(geo.row_tile, 256)
    m = geo.m
    lane_pad = lambda v: jnp.pad(v.astype(F32), ((0, 0), (0, LANES - N_HEADS))).reshape(2, 1, LANES)
    ab_blk = CCOL_AB // LANES
    rows = lambda width: pl.BlockSpec((tr, width), lambda i: (i, 0))
    full3 = pl.BlockSpec((2, 1, LANES), lambda i: (0, 0, 0))
    return pl.pallas_call(
        functools.partial(_gdn_prep_kernel, geo, tr),
        grid=(m // tr,),
        in_specs=_halo_specs(geo, tr, 3 * MIX_W, COL_GDN_QKV // (3 * MIX_W)) + [
            pl.BlockSpec((tr, LANES), lambda i: (i, ab_blk)),
            pl.BlockSpec((tr, LANES), lambda i: (i, ab_blk + 1)),
            pl.BlockSpec((CONV_W, 3 * MIX_W), lambda i: (0, 0)), full3, full3],
        out_specs=[rows(MIX_W), rows(MIX_W), rows(MIX_W),
                   pl.BlockSpec((2, tr, LANES), lambda i: (0, i, 0))],
        out_shape=[jax.ShapeDtypeStruct((m, MIX_W), F32)] * 3 + [jax.ShapeDtypeStruct((2, m, LANES), F32)],
        compiler_params=_cparams(("parallel",)),
        name="gdn_prep",
    )(p, p, p, pc, pc, conv_w.astype(F32), lane_pad(a_log), lane_pad(dt_bias))


def _chunk_index(geo, d, j):
    nc_ctx, nc_lat = geo.n_ctx // CHUNK, geo.n_lat // CHUNK
    jc = jnp.where(d == 0, j, nc_ctx - 1 - j)
    jl = jnp.where(d == 0, j - nc_ctx, nc_lat - 1 - (j - nc_ctx))
    return jnp.where(j < nc_ctx, nc_lat + jc, jl)


def _scan_grid(geo):
    return (2, geo.n_tot // CHUNK)


def _chunk_spec(geo, width, col_block=0):
    return pl.BlockSpec((geo.b, CHUNK, width), lambda d, j: (0, _chunk_index(geo, d, j), col_block))


def _dir_chunk_spec(geo, width):
    return pl.BlockSpec((1, geo.b, CHUNK, width), lambda d, j: (d, 0, _chunk_index(geo, d, j), 0))


def _per_dir_spec(shape):
    return pl.BlockSpec((1,) + shape, lambda d, j: (d,) + (0,) * len(shape))


N_LEVELS = int(math.log2(CHUNK))


def _chunk_consts():
    t = np.arange(CHUNK)
    tt, uu = t[:, None], t[None, :]
    tri = uu <= tt
    pair = []
    for lv in range(N_LEVELS):
        hi = ((t >> lv) & 1) == 1
        pair.append(((uu >> (lv + 1)) == (tt >> (lv + 1))) & hi[:, None] & (~hi)[None, :])
    pair = np.stack(pair)
    both = lambda a: np.stack([a, a[..., ::-1, ::-1]]).astype(np.float32)
    return jnp.asarray(both(tri)), jnp.asarray(both(pair))


def _gdn_scan_kernel(n_batch, qf_ref, kf_ref, vf_ref, qb_ref, kb_ref, vb_ref, gbf_ref, gbb_ref, tri_ref,
                     pair_ref, of_ref, ob_ref, s_ref):
    @pl.when(pl.program_id(0) == 0)
    def _():
        s_ref[...] = jnp.zeros_like(s_ref)

    row = lax.broadcasted_iota(jnp.int32, (CHUNK, CHUNK), 0)
    col = lax.broadcasted_iota(jnp.int32, (CHUNK, CHUNK), 1)
    eye = (row == col).astype(F32)
    chains = []
    for d, (q_ref, k_ref, v_ref, gb_ref, o_ref) in enumerate(((qf_ref, kf_ref, vf_ref, gbf_ref, of_ref),
                                                              (qb_ref, kb_ref, vb_ref, gbb_ref, ob_ref))):
        tri = tri_ref[d]
        incl = tri > 0.5
        for b in range(n_batch):
            gb = gb_ref[0, b]
            gcum = _dot(tri, gb, precision=HIGHEST)
            gcum_t = gcum.T
            gtot = jnp.sum(gb, axis=0, keepdims=True)
            q, k, v = q_ref[b], k_ref[b], v_ref[b]
            for h in range(N_HEADS):
                sl = slice(h * HEAD_DIM, (h + 1) * HEAD_DIM)
                chains.append(dict(d=d, b=b, h=h, sl=sl, o_ref=o_ref, incl=incl, bc=gcum[:, h:h + 1],
                                   br=gcum_t[h:h + 1, :], beta=gb[:, N_HEADS + h:N_HEADS + h + 1],
                                   bl=gtot[:, h:h + 1], q=q[:, sl], k=k[:, sl], v=v[:, sl], s=s_ref[d, b, h]))
    for c in chains:
        c["decay"] = jnp.where(c["incl"], jnp.exp(jnp.where(c["incl"], c["bc"] - c["br"], 0.0)), 0.0)
        c["kbeta"] = c["k"] * c["beta"]
        c["k16"] = c["k"].astype(BF16)
        c["mm"] = _dot_nt(c["kbeta"].astype(BF16), c["k16"]) * c["decay"]
        c["dinv"] = eye - pair_ref[c["d"], 0] * c["mm"]
    for lv in range(1, N_LEVELS):
        for c in chains:
            c["ld"] = _bdot(pair_ref[c["d"], lv] * c["mm"], c["dinv"])
        for c in chains:
            c["dinv"] = c["dinv"] - _bdot(c["dinv"], c["ld"])
    for c in chains:
        c["eb"] = jnp.exp(c["bc"])
        rhs = jnp.concatenate([c["v"] * c["beta"], c["kbeta"] * c["eb"]], axis=-1)
        c["sol"] = rhs + _bdot(c["dinv"] - eye, rhs)
        c["attn"] = _dot_nt(c["q"].astype(BF16), c["k16"]) * c["decay"]
    for c in chains:
        c["v_new"] = c["sol"][:, :HEAD_DIM] - _bdot(c["sol"][:, HEAD_DIM:], c["s"])
    for c in chains:
        c["o_ref"][c["b"], :, c["sl"]] = (_bdot(c["q"] * c["eb"], c["s"])
                                          + _bdot(c["attn"], c["v_new"])).astype(c["o_ref"].dtype)
        s_ref[c["d"], c["b"], c["h"]] = (c["s"] * jnp.exp(c["bl"])
                                         + _bdot_tn(c["k"] * jnp.exp(c["bl"] - c["bc"]), c["v_new"]))


def _state_scratch(geo):
    return pltpu.VMEM((geo.b, N_HEADS, HEAD_DIM, HEAD_DIM), F32)


def _gdn_scan(geo, q, k, v, gb):
    tri, pair = _chunk_consts()
    view = lambda a: a.reshape(geo.b, geo.n_tot, a.shape[-1])
    chunk = lambda d, width: pl.BlockSpec((geo.b, CHUNK, width), lambda j: (0, _chunk_index(geo, d, j), 0))
    gate = lambda d: pl.BlockSpec((1, geo.b, CHUNK, LANES), lambda j: (d, 0, _chunk_index(geo, d, j), 0))
    full = lambda shape: pl.BlockSpec(shape, lambda j: (0,) * len(shape))
    q3, k3, v3 = view(q), view(k), view(v)
    gb4 = gb.reshape(2, geo.b, geo.n_tot, LANES)
    o_f, o_b = pl.pallas_call(
        functools.partial(_gdn_scan_kernel, geo.b),
        grid=(geo.n_tot // CHUNK,),
        in_specs=[chunk(0, MIX_W)] * 3 + [chunk(1, MIX_W)] * 3 + [gate(0), gate(1),
                  full((2, CHUNK, CHUNK)), full((2, N_LEVELS, CHUNK, CHUNK))],
        out_specs=[chunk(0, MIX_W), chunk(1, MIX_W)],
        out_shape=[jax.ShapeDtypeStruct((geo.b, geo.n_tot, MIX_W), BF16)] * 2,
        scratch_shapes=[pltpu.VMEM((2, geo.b, N_HEADS, HEAD_DIM, HEAD_DIM), F32)],
        compiler_params=_cparams(("arbitrary",)),
        name="gdn_scan",
    )(q3, k3, v3, q3, k3, v3, gb4, gb4, tri, pair)
    return o_f.reshape(geo.m, MIX_W), o_b.reshape(geo.m, MIX_W)


def _hgrn_consts():
    t = np.arange(CHUNK)
    tt, uu = t[:, None], t[None, :]
    a_all, rowhi, pair = [(uu <= tt)], [], [(uu == tt)]
    for lv in range(N_LEVELS):
        hi = ((t >> lv) & 1) == 1
        same_half = (uu >> lv) == (tt >> lv)
        a_all.append(same_half & np.where(hi[:, None], uu <= tt, uu > tt))
        rowhi.append(np.broadcast_to(hi[:, None], (CHUNK, HEAD_DIM)))
        pair.append(((uu >> (lv + 1)) == (tt >> (lv + 1))) & hi[:, None] & (~hi)[None, :])
    mirror = lambda a: a[::-1, ::-1] if a.shape[1] == CHUNK else a[::-1]
    stack = lambda xs: np.stack([np.stack(xs), np.stack([mirror(x) for x in xs])]).astype(np.float32)
    a = stack(a_all)
    return (jnp.asarray(a.reshape(2, (N_LEVELS + 1) * CHUNK, CHUNK)), jnp.asarray(stack(rowhi)),
            jnp.asarray(stack(pair)))


def _hgrn_scan_kernel(n_batch, q_ref, i_ref, z_ref, lb_ref, a_ref, rowhi_ref, pair_ref, o_ref, st_ref):
    @pl.when(pl.program_id(1) == 0)
    def _():
        st_ref[...] = jnp.zeros_like(st_ref)

    lb = lb_ref[0]
    log_lb = jnp.log(jnp.maximum(lb, LB_MIN))
    log_1mlb = jnp.log1p(-lb)
    a16 = a_ref[0].astype(BF16)
    a16 = jnp.concatenate([a16, a16, a16], axis=1)

    def prep(b):
        z = z_ref[b]
        other = log_1mlb + _log_sigmoid(z)
        log_f = jnp.maximum(log_lb, other) + jnp.log1p(jnp.exp(-jnp.abs(log_lb - other)))
        return dict(b=b, key=(1.0 - lb) * _sigmoid(-z), q=_silu(q_ref[b].astype(F32)) * HEAD_DIM ** -0.5,
                    v=i_ref[b],
                    e_all=_dot_exact01(a16, log_f),
                    ltot=jnp.sum(log_f, axis=0, keepdims=True))

    def levels(pb, h):
        sl = slice(h * HEAD_DIM, (h + 1) * HEAD_DIM)
        q, k, e = pb["q"][:, sl], pb["key"][:, sl], pb["e_all"][:, sl]
        scores = pair_ref[0, 0] * _bdot_nt(q, k)
        for lv in range(N_LEVELS):
            z16 = (jnp.where(rowhi_ref[0, lv] > 0.5, q, k) * jnp.exp(e[(lv + 1) * CHUNK:(lv + 2) * CHUNK])).astype(BF16)
            scores = scores + pair_ref[0, lv + 1] * _dot_nt(z16, z16)
        return dict(b=pb["b"], h=h, sl=sl, q=q, k=k, v=pb["v"][:, sl], bh=e[:CHUNK], bl=pb["ltot"][:, sl],
                    scores=scores)

    def tail(c):
        st = st_ref[c["b"], c["h"]]
        o_ref[0, c["b"], :, c["sl"]] = (_bdot(c["scores"], c["v"])
                                        + _bdot_nt(c["q"] * jnp.exp(c["bh"]), st)).astype(o_ref.dtype)
        st_ref[c["b"], c["h"]] = st * jnp.exp(c["bl"]) + _bdot_tn(c["v"], c["k"] * jnp.exp(c["bl"] - c["bh"]))

    pending = None
    pb = prep(0)
    for b in range(n_batch):
        pb_next = prep(b + 1) if b + 1 < n_batch else None
        for h in range(N_HEADS):
            cur = levels(pb, h)
            if pending is not None:
                tail(pending)
            pending = cur
        pb = pb_next
    tail(pending)


def _hgrn_scan(geo, p, pc, lb):
    a_all, rowhi, pair = _hgrn_consts()
    p3 = p.reshape(geo.b, geo.n_tot, N_PROJ)
    pc3 = pc.reshape(geo.b, geo.n_tot, N_CRIT)
    return pl.pallas_call(
        functools.partial(_hgrn_scan_kernel, geo.b),
        grid=_scan_grid(geo),
        in_specs=[_chunk_spec(geo, MIX_W, COL_HG_Q // MIX_W), _chunk_spec(geo, MIX_W, COL_HG_I // MIX_W),
                  pl.BlockSpec((geo.b, CHUNK, MIX_W),
                               lambda d, j: (0, _chunk_index(geo, d, j), CCOL_HG_F // MIX_W + d)),
                  _per_dir_spec((1, MIX_W)), _per_dir_spec(((N_LEVELS + 1) * CHUNK, CHUNK)),
                  _per_dir_spec((N_LEVELS, CHUNK, HEAD_DIM)), _per_dir_spec((N_LEVELS + 1, CHUNK, CHUNK))],
        out_specs=_dir_chunk_spec(geo, MIX_W),
        out_shape=jax.ShapeDtypeStruct((2, geo.b, geo.n_tot, MIX_W), BF16),
        scratch_shapes=[_state_scratch(geo)],
        compiler_params=_cparams(("parallel", "arbitrary")),
        name="hgrn_scan",
    )(p3, p3, pc3, lb.reshape(2, 1, MIX_W), a_all, rowhi, pair)


def _ret_consts(n_lat):
    t = np.arange(CHUNK, dtype=np.float64)
    log_gamma = np.log1p(-np.exp2(-5.0 - np.arange(N_HEADS, dtype=np.float64)))
    dmat, qs, ks = [], [], []
    for rev in (False, True):
        te = t[::-1] if rev else t
        diff = te[:, None] - te[None, :]
        dmat.append(np.stack([np.where(diff >= 0, np.exp(lg * np.maximum(diff, 0)), 0.0) for lg in log_gamma]))
        qs.append(np.repeat(np.exp(log_gamma[None, :] * (te[:, None] + 1.0)), HEAD_DIM, axis=1))
        ks.append(np.repeat(np.exp(log_gamma[None, :] * (CHUNK - 1.0 - te[:, None])), HEAD_DIM, axis=1))
    g_chunk = np.repeat(np.exp(log_gamma * CHUNK), HEAD_DIM)[None, :]
    row = np.repeat(np.arange(n_lat // GRID_W, dtype=np.float32), GRID_W)
    colp = np.tile(np.arange(GRID_W, dtype=np.float32), n_lat // GRID_W)
    n_freq = HEAD_DIM // 4
    inv_freq = (np.float32(ROPE_BASE) ** (-np.arange(n_freq, dtype=np.float32) / n_freq)).astype(np.float32)
    ang = np.concatenate([row[:, None] * inv_freq, colp[:, None] * inv_freq], axis=-1).astype(np.float32)
    cos, sin = np.cos(ang), np.sin(ang)
    f32 = lambda a: jnp.asarray(np.asarray(a, np.float32))
    return (f32(np.stack(dmat)), f32(np.stack(qs)), f32(np.stack(ks)), f32(g_chunk),
            f32(np.concatenate([cos, cos], axis=-1)), f32(np.concatenate([-sin, sin], axis=-1)))


def _ret_scan_kernel(n_batch, nc_ctx, q_ref, k_ref, v_ref, cos_ref, sin_ref, dm_ref, qs_ref, ks_ref, gc_ref,
                     o_ref, s_ref):
    j = pl.program_id(1)

    @pl.when(j == 0)
    def _():
        s_ref[...] = jnp.zeros_like(s_ref)

    is_lat = j >= nc_ctx
    cos = jnp.where(is_lat, cos_ref[...], 1.0)
    sin = jnp.where(is_lat, sin_ref[...], 0.0)
    qs, ks, gc = qs_ref[0], ks_ref[0], gc_ref[...]
    chains = []
    for b in range(n_batch):
        q, k, v = q_ref[b].astype(F32), k_ref[b].astype(F32), v_ref[b]
        for h in range(N_HEADS):
            sl = slice(h * HEAD_DIM, (h + 1) * HEAD_DIM)
            qh, kh = q[:, sl], k[:, sl]
            rq = qh * cos + pltpu.roll(qh, HEAD_DIM // 2, 1) * sin
            rk = (kh * cos + pltpu.roll(kh, HEAD_DIM // 2, 1) * sin) * HEAD_DIM ** -0.5
            chains.append(dict(b=b, h=h, sl=sl, rq=rq, rk=rk, v=v[:, sl], s=s_ref[b, h]))
    for c in chains:
        c["scores"] = _bdot_nt(c["rq"], c["rk"]) * dm_ref[0, c["h"]]
    for c in chains:
        sl = c["sl"]
        o_ref[0, c["b"], :, sl] = (_bdot(c["scores"], c["v"])
                                   + _bdot(c["rq"] * qs[:, sl], c["s"])).astype(o_ref.dtype)
        s_ref[c["b"], c["h"]] = c["s"] * gc[:, sl] + _bdot_tn(c["rk"] * ks[:, sl], c["v"])


def _ret_scan(geo, p):
    dmat, qs, ks, gc, cos2, sin2 = _ret_consts(geo.n_lat)
    nc_ctx = geo.n_ctx // CHUNK
    p3 = p.reshape(geo.b, geo.n_tot, N_PROJ)

    def rope_blk(d, j):
        return (jnp.where(j < nc_ctx, 0, _chunk_index(geo, d, j)), 0)

    return pl.pallas_call(
        functools.partial(_ret_scan_kernel, geo.b, nc_ctx),
        grid=_scan_grid(geo),
        in_specs=[_chunk_spec(geo, MIX_W, COL_RET_Q // MIX_W), _chunk_spec(geo, MIX_W, COL_RET_K // MIX_W),
                  _chunk_spec(geo, MIX_W, COL_RET_V // MIX_W),
                  pl.BlockSpec((CHUNK, HEAD_DIM), rope_blk), pl.BlockSpec((CHUNK, HEAD_DIM), rope_blk),
                  _per_dir_spec((N_HEADS, CHUNK, CHUNK)), _per_dir_spec((CHUNK, MIX_W)),
                  _per_dir_spec((CHUNK, MIX_W)), pl.BlockSpec((1, MIX_W), lambda d, j: (0, 0))],
        out_specs=_dir_chunk_spec(geo, MIX_W),
        out_shape=jax.ShapeDtypeStruct((2, geo.b, geo.n_tot, MIX_W), BF16),
        scratch_shapes=[_state_scratch(geo)],
        compiler_params=_cparams(("parallel", "arbitrary")),
        name="ret_scan",
    )(p3, p3, p3, cos2, sin2, dmat, qs, ks, gc)


def _lru_kernel(geo, tr, rev, prev_ref, x_ref, next_ref, cw_ref, cb_ref, wg_ref, bg_ref, lam_ref,
                o_ref, h_ref):
    b, j = pl.program_id(0), pl.program_id(1)

    @pl.when(j == 0)
    def _():
        h_ref[...] = jnp.zeros_like(h_ref)

    first, last = geo.seq_edges(_lru_block(geo, tr, rev, b, j), tr)
    xb = _token_conv(prev_ref, x_ref, next_ref, cw_ref, first, last) + cb_ref[...]
    gates = _bdot(xb, wg_ref[...]) + bg_ref[...]
    gate_r = _sigmoid(gates[:, :MIX_W])
    gate_i = _sigmoid(gates[:, MIX_W:])
    log_a = LRU_C * gate_r * _log_sigmoid(lam_ref[...])
    a = jnp.exp(log_a)
    u = xb * gate_i * jnp.sqrt(jnp.maximum(-jnp.tanh(log_a) * (a * a + 1.0), SQRT_MIN))
    ng = tr // SUBLANES
    a3 = a.reshape(ng, SUBLANES, MIX_W)
    u3 = u.reshape(ng, SUBLANES, MIX_W)
    sub = lax.broadcasted_iota(jnp.int32, (ng, SUBLANES, MIX_W), 1)
    step = 1
    while step < SUBLANES:
        shift = SUBLANES - step if rev else step
        valid = (sub < SUBLANES - step) if rev else (sub >= step)
        u3 = jnp.where(valid, a3 * pltpu.roll(u3, shift, 1) + u3, u3)
        a3 = jnp.where(valid, a3 * pltpu.roll(a3, shift, 1), a3)
        step *= 2
    edge = 0 if rev else SUBLANES - 1
    h_in = jnp.broadcast_to(h_ref[...], (SUBLANES, MIX_W))
    h_groups = [None] * ng
    for g in (range(ng - 1, -1, -1) if rev else range(ng)):
        h_groups[g] = h_in
        h_in = a3[g, edge:edge + 1] * h_in + u3[g, edge:edge + 1]
    h = a3 * jnp.stack(h_groups, axis=0) + u3
    o_ref[...] = h.reshape(tr, MIX_W).astype(o_ref.dtype)
    h_ref[...] = h_in[0:1]


def _lru_block(geo, tr, rev, b, j):
    nb_ctx, nb_lat = geo.n_ctx // tr, geo.n_lat // tr
    jc = nb_ctx - 1 - j if rev else j
    jl = nb_lat - 1 - (j - nb_ctx) if rev else j - nb_ctx
    return b * (geo.n_tot // tr) + jnp.where(j < nb_ctx, nb_lat + jc, jl)


def _lru_scan(geo, p, conv_w, conv_b, wa, ba, wx, bx, lam, rev):
    tr = _largest_tile(geo.row_tile, 256)
    d = 1 if rev else 0
    blockdiag = lambda w: jax.scipy.linalg.block_diag(*[w[n] for n in range(LRU_BLOCKS)])
    wg = jnp.concatenate([blockdiag(wa[d]), blockdiag(wx[d])], axis=1).astype(BF16)
    bg = jnp.concatenate([ba[d], bx[d]]).astype(F32).reshape(1, 2 * MIX_W)
    blk = lambda b, j: _lru_block(geo, tr, rev, b, j)
    full = lambda shape: pl.BlockSpec(shape, lambda b, j: (0,) * len(shape))
    return pl.pallas_call(
        functools.partial(_lru_kernel, geo, tr, rev),
        grid=(geo.b, (geo.n_ctx + geo.n_lat) // tr),
        in_specs=_halo_specs(geo, tr, MIX_W, COL_LRU_X // MIX_W, blk) + [
                  full((CONV_W, MIX_W)), full((1, MIX_W)), full((MIX_W, 2 * MIX_W)),
                  full((1, 2 * MIX_W)), full((1, MIX_W))],
        out_specs=pl.BlockSpec((tr, MIX_W), lambda b, j: (blk(b, j), 0)),
        out_shape=jax.ShapeDtypeStruct((geo.m, MIX_W), BF16),
        scratch_shapes=[pltpu.VMEM((1, MIX_W), F32)],
        compiler_params=_cparams(("parallel", "arbitrary")),
        name="lru_scan_bwd" if rev else "lru_scan_fwd",
    )(p, p, p, conv_w.astype(F32), conv_b.astype(F32).reshape(1, MIX_W), wg, bg,
      lam[d].astype(F32).reshape(1, MIX_W))


def _gelu_tanh(x):
    return 0.5 * x * (1.0 + jnp.tanh(math.sqrt(2.0 / math.pi) * (x + 0.044715 * (x * x * x))))


def _gated_head_norm(o, g, z):
    parts = []
    for h in range(N_HEADS):
        oh = o[:, h * HEAD_DIM:(h + 1) * HEAD_DIM]
        parts.append(oh * lax.rsqrt(jnp.mean(oh * oh, axis=-1, keepdims=True) + EPS) * g)
    return jnp.concatenate(parts, axis=-1) * _silu(z)


def _merge_kernel(ogf_ref, ogb_ref, oh_ref, olf_ref, olb_ref, or_ref, gz_ref, hg_ref, ly_ref, rg_ref, mg_ref,
                  x_ref, gt_ref, gng_ref, hng_ref, rng_ref, wb_ref, wo_ref, out_ref):
    f32 = lambda v: v.astype(F32)
    branches = (
        _gated_head_norm(f32(ogf_ref[...]) + f32(ogb_ref[...]), gng_ref[...], f32(gz_ref[...])),
        _gated_head_norm(f32(oh_ref[0]) + f32(oh_ref[1]), hng_ref[...], f32(hg_ref[...])),
        (f32(olf_ref[...]) + f32(olb_ref[...])) * _gelu_tanh(f32(ly_ref[...])),
        _gated_head_norm(f32(or_ref[0]) + f32(or_ref[1]), rng_ref[...], f32(rg_ref[...])),
    )
    mix = None
    for n, y in enumerate(branches):
        term = _sigmoid(f32(mg_ref[:, n * D_MODEL:(n + 1) * D_MODEL])) * _bdot(y, wb_ref[n])
        mix = term if mix is None else mix + term
    out_ref[...] = x_ref[...] + gt_ref[0] * _bdot(mix, wo_ref[...])


def _merge(geo, compact, x_rows, p, o_gdn, o_hgrn, o_lru_f, o_lru_b, o_ret, mods3, gdn_g, hgrn_g, ret_g,
           w_branch, w_out):
    tm = _largest_tile(geo.row_tile, 256)
    n_rows = geo.m_lat if compact else geo.m
    src = lambda i: geo.full_tile(i, tm, compact)
    both = pl.BlockSpec((2, tm, MIX_W), lambda i: (0, src(i), 0))
    rows = lambda width: pl.BlockSpec((tm, width), lambda i: (src(i), 0))
    col = lambda c, width: pl.BlockSpec((tm, width), lambda i: (src(i), c // width))
    full = lambda shape: pl.BlockSpec(shape, lambda i: (0,) * len(shape))
    hd = lambda g: g.astype(F32).reshape(1, HEAD_DIM)
    return pl.pallas_call(
        _merge_kernel,
        grid=(n_rows // tm,),
        in_specs=[rows(MIX_W), rows(MIX_W), both, rows(MIX_W), rows(MIX_W), both,
                  col(COL_GDN_Z, MIX_W), col(COL_HG_G, MIX_W), col(COL_LRU_Y, MIX_W), col(COL_RET_G, MIX_W),
                  col(COL_MERGE, N_BRANCH * D_MODEL), rows(D_MODEL),
                  pl.BlockSpec((1, 1, D_MODEL), lambda i: (geo.mod_row(i, tm, compact), 0, 2)),
                  full((1, HEAD_DIM)), full((1, HEAD_DIM)), full((1, HEAD_DIM)),
                  full((N_BRANCH, MIX_W, D_MODEL)), full((D_MODEL, D_MODEL))],
        out_specs=pl.BlockSpec((tm, D_MODEL), lambda i: (i, 0)),
        out_shape=jax.ShapeDtypeStruct((n_rows, D_MODEL), F32),
        compiler_params=_cparams(("parallel",)),
        name="merge",
    )(o_gdn[0], o_gdn[1], o_hgrn, o_lru_f, o_lru_b, o_ret, p, p, p, p, p, x_rows, mods3, hd(gdn_g), hd(hgrn_g),
      hd(ret_g), w_branch.astype(BF16), w_out.astype(BF16))


ROW_TILE = D_MODEL // LANES
assert ROW_TILE == SUBLANES


def _store_row_tiles(ref, val):
    n = val.shape[0]
    v = jnp.stack([val[:, s * LANES:(s + 1) * LANES].reshape(n // SUBLANES, SUBLANES, LANES)
                   for s in range(ROW_TILE)], axis=1)
    ref[...] = jnp.swapaxes(v, 1, 2).reshape(n * ROW_TILE, LANES)


def _load_row_tiles(ref_at, start, n):
    v = ref_at[pl.ds(start, n * ROW_TILE), :].reshape(n // SUBLANES, SUBLANES, ROW_TILE, LANES)
    v = jnp.swapaxes(v, 1, 2)
    return jnp.concatenate([v[:, s].reshape(n, LANES) for s in range(ROW_TILE)], axis=1)


def _router_kernel(n_experts, x_ref, g_ref, sc_ref, sh_ref, rw_ref, rb_ref, f_ref, idx_ref, wt_ref):
    f = _rms_mod(x_ref[...], g_ref[...], sc_ref[0], sh_ref[0])
    _store_row_tiles(f_ref, f)
    logits = _dot(f, rw_ref[...], precision=HIGHEST) + rb_ref[...]
    lane = lax.broadcasted_iota(jnp.int32, logits.shape, 1)
    logits = jnp.where(lane < n_experts, logits, NEG_BIG)
    vals, idxs = [], []
    for _ in range(TOP_K):
        mx = jnp.max(logits, axis=-1, keepdims=True)
        ix = jnp.min(jnp.where(logits == mx, lane, LANES), axis=-1, keepdims=True)
        vals.append(mx)
        idxs.append(ix)
        logits = jnp.where(lane == ix, -jnp.inf, logits)
    ex = [jnp.exp(vl - vals[0]) for vl in vals]
    tot = ex[0] + ex[1] + ex[2] + ex[3]
    idx_out = jnp.zeros(lane.shape, jnp.int32)
    wt_out = jnp.zeros(lane.shape, F32)
    for kk in range(TOP_K):
        idx_out = jnp.where(lane == kk, idxs[kk], idx_out)
        wt_out = jnp.where(lane == kk, ex[kk] / tot, wt_out)
    idx_ref[...] = idx_out
    wt_ref[...] = wt_out


def _router(geo, compact, x_rows, g, mods3, router_w, router_b):
    n_rows, d = x_rows.shape
    n_experts = router_w.shape[1]
    tm = _largest_tile(geo.row_tile, 256)
    rw = jnp.pad(router_w.astype(F32), ((0, 0), (0, LANES - n_experts)))
    rb = jnp.pad(router_b.astype(F32), (0, LANES - n_experts)).reshape(1, LANES)
    rows = lambda width: pl.BlockSpec((tm, width), lambda i: (i, 0))
    full = lambda shape: pl.BlockSpec(shape, lambda i: (0,) * len(shape))
    return pl.pallas_call(
        functools.partial(_router_kernel, n_experts),
        grid=(n_rows // tm,),
        in_specs=[rows(d), full((1, d)),
                  pl.BlockSpec((1, 1, d), lambda i: (geo.mod_row(i, tm, compact), 0, 4)),
                  pl.BlockSpec((1, 1, d), lambda i: (geo.mod_row(i, tm, compact), 0, 3)),
                  full((d, LANES)), full((1, LANES))],
        out_specs=[pl.BlockSpec((tm * ROW_TILE, LANES), lambda i: (i, 0)), rows(LANES), rows(LANES)],
        out_shape=[jax.ShapeDtypeStruct((n_rows * ROW_TILE, LANES), F32),
                   jax.ShapeDtypeStruct((n_rows, LANES), jnp.int32), jax.ShapeDtypeStruct((n_rows, LANES), F32)],
        compiler_params=_cparams(("parallel",)),
        name="router",
    )(x_rows, g.astype(F32).reshape(1, d), mods3, mods3, rw, rb)


def _route_plan(top_idx, n_experts):
    m = top_idx.shape[0]
    n_assign = m * TOP_K
    flat_e = top_idx.reshape(-1)
    order = jnp.argsort(flat_e).astype(jnp.int32)
    blk = _largest_tile(n_assign, 512)
    onehot = (flat_e[:, None] == jnp.arange(n_experts, dtype=jnp.int32)[None, :]).astype(F32)
    oh3 = onehot.reshape(n_assign // blk, blk, n_experts)
    strict_lower = jnp.asarray(np.tril(np.ones((blk, blk), np.float32), k=-1))
    block_tot = jnp.sum(oh3, axis=1)
    block_off = jnp.cumsum(block_tot, axis=0) - block_tot
    prefix = jnp.einsum("ij,bjk->bik", strict_lower, oh3) + block_off[:, None, :]
    rank_in_expert = jnp.sum(prefix * oh3, axis=-1).reshape(n_assign).astype(jnp.int32)
    counts = jnp.sum(block_tot, axis=0).astype(jnp.int32)
    padded = (counts + ROUTE_BLOCK - 1) // ROUTE_BLOCK * ROUTE_BLOCK
    pad_end = jnp.cumsum(padded)
    pad_start = pad_end - padded
    start = jnp.cumsum(counts) - counts
    n_blocks = -(-n_assign // ROUTE_BLOCK) + n_experts
    cap = n_blocks * ROUTE_BLOCK
    block_row0 = jnp.arange(n_blocks, dtype=jnp.int32) * ROUTE_BLOCK
    block_expert = jnp.minimum(jnp.sum(pad_end[None, :] <= block_row0[:, None], axis=1, dtype=jnp.int32),
                               n_experts - 1)
    n_used = (pad_end[-1:] // ROUTE_BLOCK).astype(jnp.int32)
    row = jnp.arange(cap, dtype=jnp.int32)
    row_e = jnp.repeat(block_expert, ROUTE_BLOCK)
    within = row - pad_start[row_e]
    valid = jnp.logical_and(within >= 0, within < counts[row_e])
    src_tok = jnp.where(valid, order[jnp.clip(start[row_e] + within, 0, n_assign - 1)] // TOP_K, 0)
    dest = pad_start[flat_e] + rank_in_expert
    return src_tok.reshape(n_blocks, 1, ROUTE_BLOCK), block_expert, n_used, dest


def _start_row_gather(idx_ref, n, src_hbm, buf, slot, sem):
    for r in range(n):
        src_row = pl.multiple_of(idx_ref[0, 0, r] * ROW_TILE, ROW_TILE)
        pltpu.make_async_copy(src_hbm.at[pl.ds(src_row, ROW_TILE)], buf.at[slot, pl.ds(r * ROW_TILE, ROW_TILE)],
                              sem.at[slot]).start(priority=r % 2)


def _wait_row_gather(n, src_hbm, buf, slot, sem):
    pltpu.make_async_copy(src_hbm.at[pl.ds(0, n * ROW_TILE)], buf.at[slot], sem.at[slot]).wait()


def _gather_step(cur_ref, nxt_ref, n, src_hbm, buf, sem):
    i, nb = pl.program_id(0), pl.num_programs(0)
    slot = i % 2

    @pl.when(i == 0)
    def _():
        _start_row_gather(cur_ref, n, src_hbm, buf, 0, sem)

    _wait_row_gather(n, src_hbm, buf, slot, sem)
    _start_row_gather(nxt_ref, n, src_hbm, buf, 1 - slot, sem)
    return slot


def _gather_drain(n, src_hbm, buf, slot, sem):
    @pl.when(pl.program_id(0) == pl.num_programs(0) - 1)
    def _():
        _wait_row_gather(n, src_hbm, buf, 1 - slot, sem)


def _expert_kernel(be_ref, nu_ref, src_ref, nxt_ref, f_hbm, w1_ref, b1_ref, w2_ref, b2_ref, y_ref,
                   buf, sem, w1b_ref, w2b_ref):
    i = pl.program_id(0)
    ff = w2_ref.shape[1]
    slot = _gather_step(src_ref, nxt_ref, ROUTE_BLOCK, f_hbm, buf, sem)

    changed = jnp.logical_or(i == 0, be_ref[i] != be_ref[jnp.maximum(i - 1, 0)])
    r1, r2 = w1_ref.shape[1] // CAST_STEPS, w2_ref.shape[1] // CAST_STEPS

    def cast_rows(t, carry):
        o1, o2 = pl.multiple_of(t * r1, r1), pl.multiple_of(t * r2, r2)
        w1b_ref[pl.ds(o1, r1), :] = w1_ref[0, pl.ds(o1, r1), :].astype(BF16)
        w2b_ref[pl.ds(o2, r2), :] = w2_ref[0, pl.ds(o2, r2), :].astype(BF16)
        return carry

    lax.fori_loop(0, jnp.where(changed, CAST_STEPS, 0), cast_rows, 0)

    @pl.when(i >= nu_ref[0])
    def _():
        y_ref[...] = jnp.zeros_like(y_ref)

    @pl.when(i < nu_ref[0])
    def _():
        z = _bdot(_load_row_tiles(buf.at[slot], 0, ROUTE_BLOCK), w1b_ref[...]) + b1_ref[0]
        glu = jnp.minimum(z[:, :ff], SWIGLU_LIMIT)
        lin = jnp.clip(z[:, ff:], -SWIGLU_LIMIT, SWIGLU_LIMIT)
        act = glu * _sigmoid(SWIGLU_ALPHA * glu) * (lin + 1.0)
        _store_row_tiles(y_ref, _bdot(act, w2b_ref[...]) + b2_ref[0])

    _gather_drain(ROUTE_BLOCK, f_hbm, buf, slot, sem)


def _experts(f_tiles, src_tok, block_expert, n_used, layer, w1_all, b1_all, w2_all, b2_all):
    n_blocks = src_tok.shape[0]
    n_layers, n_experts, d, ff2 = w1_all.shape
    ff = w2_all.shape[2]
    w1 = w1_all.reshape(n_layers * n_experts, d, ff2)
    w2 = w2_all.reshape(n_layers * n_experts, ff, d)
    b1 = b1_all.reshape(n_layers * n_experts, ff2)
    b2 = b2_all.reshape(n_layers * n_experts, d)
    block_expert = block_expert + layer * n_experts
    n_experts = n_layers * n_experts
    grid_spec = pltpu.PrefetchScalarGridSpec(
        num_scalar_prefetch=2,
        grid=(n_blocks,),
        in_specs=[pl.BlockSpec((1, 1, ROUTE_BLOCK), lambda i, be, nu: (i, 0, 0), memory_space=pltpu.SMEM),
                  pl.BlockSpec((1, 1, ROUTE_BLOCK), lambda i, be, nu: (jnp.minimum(i + 1, n_blocks - 1), 0, 0),
                               memory_space=pltpu.SMEM),
                  pl.BlockSpec(memory_space=pl.ANY),
                  pl.BlockSpec((1, d, ff2), lambda i, be, nu: (be[i], 0, 0)),
                  pl.BlockSpec((1, 1, ff2), lambda i, be, nu: (be[i], 0, 0)),
                  pl.BlockSpec((1, ff, d), lambda i, be, nu: (be[i], 0, 0)),
                  pl.BlockSpec((1, 1, d), lambda i, be, nu: (be[i], 0, 0))],
        out_specs=pl.BlockSpec((ROUTE_BLOCK * ROW_TILE, LANES), lambda i, be, nu: (i, 0)),
        scratch_shapes=[pltpu.VMEM((2, ROUTE_BLOCK * ROW_TILE, LANES), F32), pltpu.SemaphoreType.DMA((2,)),
                        pltpu.VMEM((d, ff2), BF16), pltpu.VMEM((ff, d), BF16)],
    )
    return pl.pallas_call(
        _expert_kernel,
        grid_spec=grid_spec,
        out_shape=jax.ShapeDtypeStruct((n_blocks * ROUTE_BLOCK * ROW_TILE, LANES), F32),
        compiler_params=_cparams(("arbitrary",)),
        name="experts",
    )(block_expert, n_used, src_tok, src_tok, f_tiles, w1, b1.astype(F32).reshape(n_experts, 1, ff2),
      w2, b2.astype(F32).reshape(n_experts, 1, d))


def _combine_kernel(tc, final, dst_ref, nxt_ref, y_hbm, x_ref, wt_ref, gt_ref, fg_ref, out_ref, buf, sem):
    n = TOP_K * tc
    slot = _gather_step(dst_ref, nxt_ref, n, y_hbm, buf, sem)
    wt = wt_ref[...]
    acc = None
    for kk in range(TOP_K):
        term = wt[:, kk:kk + 1] * _load_row_tiles(buf.at[slot], kk * tc * ROW_TILE, tc)
        acc = term if acc is None else acc + term
    out = x_ref[...] + gt_ref[0] * acc
    if final:
        out = out * lax.rsqrt(jnp.mean(out * out, axis=-1, keepdims=True) + EPS) * fg_ref[...]
    out_ref[...] = out
    _gather_drain(n, y_hbm, buf, slot, sem)


def _combine(geo, y_pad, dest, x_rows, wt, mods3, final_g, final):
    n_rows, d = x_rows.shape
    tc = _largest_tile(geo.row_tile, 128)
    nt = n_rows // tc
    dst = dest.reshape(nt, tc, TOP_K).transpose(0, 2, 1).reshape(nt, 1, TOP_K * tc)
    rows = lambda width: pl.BlockSpec((tc, width), lambda i: (i, 0))
    return pl.pallas_call(
        functools.partial(_combine_kernel, tc, final),
        grid=(nt,),
        in_specs=[pl.BlockSpec((1, 1, TOP_K * tc), lambda i: (i, 0, 0), memory_space=pltpu.SMEM),
                  pl.BlockSpec((1, 1, TOP_K * tc), lambda i: (jnp.minimum(i + 1, nt - 1), 0, 0),
                               memory_space=pltpu.SMEM),
                  pl.BlockSpec(memory_space=pl.ANY), rows(d), rows(LANES),
                  pl.BlockSpec((1, 1, d), lambda i: (geo.mod_row(i, tc, final), 0, 5)),
                  pl.BlockSpec((1, d), lambda i: (0, 0))],
        out_specs=rows(d),
        out_shape=jax.ShapeDtypeStruct((n_rows, d), F32),
        scratch_shapes=[pltpu.VMEM((2, TOP_K * tc * ROW_TILE, LANES), F32), pltpu.SemaphoreType.DMA((2,))],
        compiler_params=_cparams(("arbitrary",)),
        name="moe_combine",
    )(dst, dst, y_pad, x_rows, wt, mods3, final_g.astype(F32).reshape(1, d))


def _moe(geo, x_rows, g, mods3, router_w, router_b, layer, w1_all, b1_all, w2_all, b2_all, final_g, final):
    f_tiles, top_idx, top_w = _router(geo, final, x_rows, g, mods3, router_w, router_b)
    src_tok, block_expert, n_used, dest = _route_plan(top_idx[:, :TOP_K], router_w.shape[1])
    y_tiles = _experts(f_tiles, src_tok, block_expert, n_used, layer, w1_all, b1_all, w2_all, b2_all)
    return _combine(geo, y_tiles, dest, x_rows, top_w, mods3, final_g, final)


def kernel(x, c, ctx, c_ctx, w_mod, b_mod, norm1_g, norm2_g, w_in, gdn_conv_w, gdn_a_log, gdn_dt_bias, gdn_norm_g, hgrn_lb, hgrn_norm_g, lru_conv_w, lru_conv_b, lru_wa, lru_ba, lru_wx, lru_bx, lru_lambda, ret_norm_g, w_branch, w_out, router_w, router_b, moe_w1, moe_b1, moe_w2, moe_b2, final_norm_g):
    b, n_lat, d = x.shape
    n_ctx = ctx.shape[1]
    depth = w_mod.shape[0]
    assert d == D_MODEL and w_branch.shape[2] == MIX_W
    geo = _Geo(b, n_lat, n_ctx)
    lb_soft = jax.nn.softmax(hgrn_lb.astype(F32), axis=0)
    lower_bounds = jnp.clip(jnp.cumsum(lb_soft, axis=0) - lb_soft[0], 0.0, LB_MAX)
    n_cond = -(-(b + 1) // SUBLANES) * SUBLANES
    cc = jnp.concatenate([c, c_ctx[None, :], jnp.zeros((n_cond - b - 1, d), c.dtype)], axis=0)
    mods = _modulation(cc.astype(F32), w_mod, b_mod)
    rows = jnp.concatenate([x, ctx], axis=1).reshape(geo.m, d)
    flat = lambda o: o.reshape(2, geo.m, MIX_W)
    for l in range(depth):
        last = l == depth - 1
        mods3 = mods[l].reshape(n_cond, 1, N_MOD * d)
        p, pc = _norm_proj(geo, rows, norm1_g[l], mods3, *_reorder_w_in(w_in[l]))
        gq, gk, gv, gb = _gdn_prep(geo, p, pc, gdn_conv_w[l], gdn_a_log[l], gdn_dt_bias[l])
        o_gdn = _gdn_scan(geo, gq, gk, gv, gb)
        o_hgrn = flat(_hgrn_scan(geo, p, pc, lower_bounds[l]))
        lru_args = (geo, p, lru_conv_w[l], lru_conv_b[l], lru_wa[l], lru_ba[l], lru_wx[l], lru_bx[l],
                    lru_lambda[l])
        o_lru_f = _lru_scan(*lru_args, rev=False)
        o_lru_b = _lru_scan(*lru_args, rev=True)
        o_ret = flat(_ret_scan(geo, p))
        rows = _merge(geo, last, rows, p, o_gdn, o_hgrn, o_lru_f, o_lru_b, o_ret, mods3, gdn_norm_g[l],
                      hgrn_norm_g[l], ret_norm_g[l], w_branch[l], w_out[l])
        rows = _moe(geo, rows, norm2_g[l], mods3, router_w[l], router_b[l], l, moe_w1, moe_b1, moe_w2, moe_b2,
                    final_norm_g, last)
    return rows.reshape(b, n_lat, d)
```

```python
import functools
import math

import numpy as np
import jax
import jax.numpy as jnp
from jax import lax
from jax.experimental import pallas as pl
from jax.experimental.pallas import tpu as pltpu

F32 = jnp.float32
BF16 = jnp.bfloat16
HIGHEST = lax.Precision.HIGHEST

GRID_W = 64
N_BRANCH = 4
HEAD_DIM = 128
LRU_BLOCKS = 8
LRU_C = 8.0
CONV_W = 4
CHUNK = 64
ROPE_BASE = 10000.0
TOP_K = 4
SWIGLU_ALPHA = 1.702
SWIGLU_LIMIT = 7.0
N_MOD = 6
EPS = 1e-6
LB_MIN = 1e-30
LB_MAX = 1.0 - 1e-4
SQRT_MIN = 1e-12

LANES = 128
SUBLANES = 8
VMEM_LIMIT_BYTES = 56 * 1024 * 1024
ROUTE_BLOCK = 256
CAST_STEPS = 8
NEG_BIG = -1e30


def _cparams(semantics):
    return pltpu.CompilerParams(dimension_semantics=semantics, vmem_limit_bytes=VMEM_LIMIT_BYTES)


def _largest_tile(n, cap, quantum=SUBLANES):
    best = None
    for t in range(quantum, min(n, cap) + 1, quantum):
        if n % t == 0:
            best = t
    assert best is not None, (n, cap)
    return best


def _dot(a, b, precision=None):
    return jnp.dot(a, b, preferred_element_type=F32, precision=precision)


def _dot_nt(a, b):
    return lax.dot_general(a, b, (((1,), (1,)), ((), ())), preferred_element_type=F32)


def _dot_tn(a, b):
    return lax.dot_general(a, b, (((0,), (0,)), ((), ())), preferred_element_type=F32)


def _bdot(a, b):
    return _dot(a.astype(BF16), b.astype(BF16))


def _bdot_nt(a, b):
    return _dot_nt(a.astype(BF16), b.astype(BF16))


def _bdot_tn(a, b):
    return _dot_tn(a.astype(BF16), b.astype(BF16))


def _dot_exact01(a01x3, x):
    hi = x.astype(BF16)
    r1 = x - hi.astype(F32)
    mid = r1.astype(BF16)
    lo = (r1 - mid.astype(F32)).astype(BF16)
    return _dot(a01x3, jnp.concatenate([hi, mid, lo], axis=0))


def _sigmoid(x):
    return 1.0 / (1.0 + jnp.exp(-x))


def _silu(x):
    return x * _sigmoid(x)


def _softplus(x):
    return jnp.maximum(x, 0.0) + jnp.log1p(jnp.exp(-jnp.abs(x)))


def _log_sigmoid(x):
    return -_softplus(-x)


def _mod_kernel(c_ref, w_ref, b_ref, o_ref):
    o_ref[0] = _dot(_silu(c_ref[...]), w_ref[0], precision=HIGHEST) + b_ref[0]


def _modulation(cc, w_mod, b_mod):
    n_layers, d, n = w_mod.shape
    rows = cc.shape[0]
    tn = _largest_tile(n, 1024, LANES)
    return pl.pallas_call(
        _mod_kernel,
        grid=(n_layers, n // tn),
        in_specs=[pl.BlockSpec((rows, d), lambda l, j: (0, 0)),
                  pl.BlockSpec((1, d, tn), lambda l, j: (l, 0, j)),
                  pl.BlockSpec((1, 1, tn), lambda l, j: (l, 0, j))],
        out_specs=pl.BlockSpec((1, rows, tn), lambda l, j: (l, 0, j)),
        out_shape=jax.ShapeDtypeStruct((n_layers, rows, n), F32),
        compiler_params=_cparams(("parallel", "parallel")),
        name="modulation",
    )(cc, w_mod, b_mod.reshape(n_layers, 1, n))


D_MODEL = 1024
MIX_W = 512
N_HEADS = MIX_W // HEAD_DIM
COL_MERGE = 0
COL_GDN_Z = 4096
COL_GDN_QKV = 4608
COL_HG_Q = 6144
COL_HG_I = 6656
COL_HG_G = 7168
COL_LRU_X = 7680
COL_LRU_Y = 8192
COL_RET_Q = 8704
COL_RET_K = 9216
COL_RET_V = 9728
COL_RET_G = 10240
N_PROJ = 10752
CCOL_HG_F = 0
CCOL_AB = 1024
N_CRIT = 1280


class _Geo:
    def __init__(self, b, n_lat, n_ctx):
        self.b, self.n_lat, self.n_ctx = b, n_lat, n_ctx
        self.n_tot = n_lat + n_ctx
        self.m_lat = b * n_lat
        self.m = b * self.n_tot
        assert n_lat % CHUNK == 0 and n_ctx % CHUNK == 0 and n_lat % GRID_W == 0
        self.row_tile = math.gcd(n_lat, n_ctx)

    def mod_row(self, i, tm, compact=False):
        r0 = i * tm
        if compact:
            return r0 // self.n_lat
        return jnp.where(r0 % self.n_tot < self.n_lat, r0 // self.n_tot, self.b)

    def full_tile(self, i, tm, compact=False):
        if not compact:
            return i
        per_lat, per_tot = self.n_lat // tm, self.n_tot // tm
        return (i // per_lat) * per_tot + i % per_lat

    def seq_edges(self, i, tr):
        pos = (i * tr) % self.n_tot
        first = jnp.logical_or(pos == 0, pos == self.n_lat)
        last = jnp.logical_or(pos + tr == self.n_lat, pos + tr == self.n_tot)
        return first, last


def _reorder_w_in(w_in):
    w = w_in.astype(BF16)
    d = w.shape[0]
    pad = lambda n: jnp.zeros((d, n), BF16)
    ab = [jnp.concatenate([w[:, 2048 + 4 * dr:2052 + 4 * dr], w[:, 2056 + 4 * dr:2060 + 4 * dr],
                           pad(LANES - 8)], axis=1) for dr in range(2)]
    main = jnp.concatenate([w[:, 7696:11792], w[:, 1536:2048], w[:, 0:1536], w[:, 2064:3088],
                            w[:, 4112:7696]], axis=1)
    crit = jnp.concatenate([w[:, 3088:4112], ab[0], ab[1]], axis=1)
    assert main.shape[1] == N_PROJ and crit.shape[1] == N_CRIT
    return main, crit


def _rms_mod(x, g, sc, sh):
    y = x * lax.rsqrt(jnp.mean(x * x, axis=-1, keepdims=True) + EPS) * g
    return y * (1.0 + sc) + sh


def _norm_kernel(x_ref, g_ref, sc_ref, sh_ref, h_ref):
    h_ref[...] = _rms_mod(x_ref[...], g_ref[...], sc_ref[0], sh_ref[0]).astype(BF16)


def _proj_kernel(h_ref, w_ref, o_ref):
    o_ref[...] = _dot(h_ref[...], w_ref[...]).astype(o_ref.dtype)


def _proj(h, w_p, out_dtype):
    m, d = h.shape
    n = w_p.shape[1]
    tm = _largest_tile(m, 1024)
    tn = _largest_tile(n, 2048, LANES)
    return pl.pallas_call(
        _proj_kernel,
        grid=(n // tn, m // tm),
        in_specs=[pl.BlockSpec((tm, d), lambda j, i: (i, 0)),
                  pl.BlockSpec((d, tn), lambda j, i: (0, j))],
        out_specs=pl.BlockSpec((tm, tn), lambda j, i: (i, j)),
        out_shape=jax.ShapeDtypeStruct((m, n), out_dtype),
        compiler_params=_cparams(("parallel", "parallel")),
        name="in_proj",
    )(h, w_p)


def _norm_proj(geo, x_rows, g, mods3, w_main, w_crit):
    m, d = x_rows.shape
    tr = _largest_tile(geo.row_tile, 512)
    h = pl.pallas_call(
        _norm_kernel,
        grid=(m // tr,),
        in_specs=[pl.BlockSpec((tr, d), lambda i: (i, 0)),
                  pl.BlockSpec((1, d), lambda i: (0, 0)),
                  pl.BlockSpec((1, 1, d), lambda i: (geo.mod_row(i, tr), 0, 1)),
                  pl.BlockSpec((1, 1, d), lambda i: (geo.mod_row(i, tr), 0, 0))],
        out_specs=pl.BlockSpec((tr, d), lambda i: (i, 0)),
        out_shape=jax.ShapeDtypeStruct((m, d), BF16),
        compiler_params=_cparams(("parallel",)),
        name="norm1",
    )(x_rows, g.reshape(1, d), mods3, mods3)
    return _proj(h, w_main, BF16), _proj(h, w_crit, F32)


HALO = 16


def _halo_specs(geo, tr, width, col_block, row_block=lambda i: i):
    nbh = geo.m // HALO
    rh = tr // HALO
    return [pl.BlockSpec((HALO, width), lambda *g: (jnp.maximum(row_block(*g) * rh - 1, 0), col_block)),
            pl.BlockSpec((tr, width), lambda *g: (row_block(*g), col_block)),
            pl.BlockSpec((HALO, width), lambda *g: (jnp.minimum((row_block(*g) + 1) * rh, nbh - 1), col_block))]


def _token_conv(prev_ref, x_ref, next_ref, w_ref, first, last):
    x = x_ref[...].astype(F32)
    tr = x.shape[0]
    prev = jnp.where(first, 0.0, prev_ref[...].astype(F32))
    nxt = jnp.where(last, 0.0, next_ref[...].astype(F32))
    ext = jnp.concatenate([prev, x, nxt], axis=0)
    w = w_ref[...]
    y = None
    for j in range(CONV_W):
        off = HALO - CONV_W // 2 + j
        term = ext[off:off + tr] * w[j:j + 1]
        y = term if y is None else y + term
    return y


def _gdn_prep_kernel(geo, tr, prev_ref, x_ref, next_ref, ab0_ref, ab1_ref, cw_ref, alog_ref, dtb_ref,
                     q_ref, k_ref, v_ref, gb_ref):
    first, last = geo.seq_edges(pl.program_id(0), tr)
    s = _silu(_token_conv(prev_ref, x_ref, next_ref, cw_ref, first, last))
    for h in range(N_HEADS):
        sl = slice(h * HEAD_DIM, (h + 1) * HEAD_DIM)
        qh = s[:, sl]
        kh = s[:, MIX_W + h * HEAD_DIM:MIX_W + (h + 1) * HEAD_DIM]
        q_ref[:, sl] = (qh * (lax.rsqrt(jnp.sum(qh * qh, axis=-1, keepdims=True) + EPS)
                              * HEAD_DIM ** -0.5)).astype(q_ref.dtype)
        k_ref[:, sl] = (kh * lax.rsqrt(jnp.sum(kh * kh, axis=-1, keepdims=True) + EPS)).astype(k_ref.dtype)
    v_ref[...] = s[:, 2 * MIX_W:].astype(v_ref.dtype)
    lane = lax.broadcasted_iota(jnp.int32, (tr, LANES), 1)
    for dr, ab_ref in enumerate((ab0_ref, ab1_ref)):
        a = ab_ref[...]
        g = -jnp.exp(alog_ref[dr]) * _softplus(a + dtb_ref[dr])
        gb_ref[dr] = jnp.where(lane < N_HEADS, g, _sigmoid(a))


def _gdn_prep(geo, p, pc, conv_w, a_log, dt_bias):
    tr = _largest_tile(geo.row_tile, 256)
    m = geo.m
    lane_pad = lambda v: jnp.pad(v.astype(F32), ((0, 0), (0, LANES - N_HEADS))).reshape(2, 1, LANES)
    ab_blk = CCOL_AB // LANES
    rows = lambda width: pl.BlockSpec((tr, width), lambda i: (i, 0))
    full3 = pl.BlockSpec((2, 1, LANES), lambda i: (0, 0, 0))
    return pl.pallas_call(
        functools.partial(_gdn_prep_kernel, geo, tr),
        grid=(m // tr,),
        in_specs=_halo_specs(geo, tr, 3 * MIX_W, COL_GDN_QKV // (3 * MIX_W)) + [
            pl.BlockSpec((tr, LANES), lambda i: (i, ab_blk)),
            pl.BlockSpec((tr, LANES), lambda i: (i, ab_blk + 1)),
            pl.BlockSpec((CONV_W, 3 * MIX_W), lambda i: (0, 0)), full3, full3],
        out_specs=[rows(MIX_W), rows(MIX_W), rows(MIX_W),
                   pl.BlockSpec((2, tr, LANES), lambda i: (0, i, 0))],
        out_shape=[jax.ShapeDtypeStruct((m, MIX_W), BF16)] * 3 + [jax.ShapeDtypeStruct((2, m, LANES), F32)],
        compiler_params=_cparams(("parallel",)),
        name="gdn_prep",
    )(p, p, p, pc, pc, conv_w.astype(F32), lane_pad(a_log), lane_pad(dt_bias))


def _chunk_index(geo, d, j):
    nc_ctx, nc_lat = geo.n_ctx // CHUNK, geo.n_lat // CHUNK
    jc = jnp.where(d == 0, j, nc_ctx - 1 - j)
    jl = jnp.where(d == 0, j - nc_ctx, nc_lat - 1 - (j - nc_ctx))
    return jnp.where(j < nc_ctx, nc_lat + jc, jl)


def _scan_grid(geo):
    return (2, geo.n_tot // CHUNK)


def _chunk_spec(geo, width, col_block=0):
    return pl.BlockSpec((geo.b, CHUNK, width), lambda d, j: (0, _chunk_index(geo, d, j), col_block))


def _dir_chunk_spec(geo, width):
    return pl.BlockSpec((1, geo.b, CHUNK, width), lambda d, j: (d, 0, _chunk_index(geo, d, j), 0))


def _per_dir_spec(shape):
    return pl.BlockSpec((1,) + shape, lambda d, j: (d,) + (0,) * len(shape))


N_LEVELS = int(math.log2(CHUNK))


def _chunk_consts():
    t = np.arange(CHUNK)
    tt, uu = t[:, None], t[None, :]
    tri = uu <= tt
    pair = []
    for lv in range(N_LEVELS):
        hi = ((t >> lv) & 1) == 1
        pair.append(((uu >> (lv + 1)) == (tt >> (lv + 1))) & hi[:, None] & (~hi)[None, :])
    pair = np.stack(pair)
    both = lambda a: np.stack([a, a[..., ::-1, ::-1]]).astype(np.float32)
    return jnp.asarray(both(tri)), jnp.asarray(both(pair))


def _gdn_scan_kernel(n_batch, qf_ref, kf_ref, vf_ref, qb_ref, kb_ref, vb_ref, gbf_ref, gbb_ref, tri_ref,
                     pair_ref, of_ref, ob_ref, s_ref):
    @pl.when(pl.program_id(0) == 0)
    def _():
        s_ref[...] = jnp.zeros_like(s_ref)

    row = lax.broadcasted_iota(jnp.int32, (CHUNK, CHUNK), 0)
    col = lax.broadcasted_iota(jnp.int32, (CHUNK, CHUNK), 1)
    eye = (row == col).astype(F32)
    chains = []
    for d, (q_ref, k_ref, v_ref, gb_ref, o_ref) in enumerate(((qf_ref, kf_ref, vf_ref, gbf_ref, of_ref),
                                                              (qb_ref, kb_ref, vb_ref, gbb_ref, ob_ref))):
        tri = tri_ref[d]
        incl = tri > 0.5
        for b in range(n_batch):
            gb = gb_ref[0, b]
            gcum = _dot(tri, gb, precision=HIGHEST)
            gcum_t = gcum.T
            gtot = jnp.sum(gb, axis=0, keepdims=True)
            q, k, v = q_ref[b].astype(F32), k_ref[b].astype(F32), v_ref[b].astype(F32)
            for h in range(N_HEADS):
                sl = slice(h * HEAD_DIM, (h + 1) * HEAD_DIM)
                chains.append(dict(d=d, b=b, h=h, sl=sl, o_ref=o_ref, incl=incl, bc=gcum[:, h:h + 1],
                                   br=gcum_t[h:h + 1, :], beta=gb[:, N_HEADS + h:N_HEADS + h + 1],
                                   bl=gtot[:, h:h + 1], q=q[:, sl], k=k[:, sl], v=v[:, sl], s=s_ref[d, b, h]))
    for c in chains:
        c["decay"] = jnp.where(c["incl"], jnp.exp(jnp.where(c["incl"], c["bc"] - c["br"], 0.0)), 0.0)
        c["kbeta"] = c["k"] * c["beta"]
        c["k16"] = c["k"].astype(BF16)
        c["mm"] = _dot_nt(c["kbeta"].astype(BF16), c["k16"]) * c["decay"]
        c["dinv"] = eye - pair_ref[c["d"], 0] * c["mm"]
    for lv in range(1, N_LEVELS):
        for c in chains:
            c["ld"] = _bdot(pair_ref[c["d"], lv] * c["mm"], c["dinv"])
        for c in chains:
            c["dinv"] = c["dinv"] - _bdot(c["dinv"], c["ld"])
    for c in chains:
        c["eb"] = jnp.exp(c["bc"])
        rhs = jnp.concatenate([c["v"] * c["beta"], c["kbeta"] * c["eb"]], axis=-1)
        c["sol"] = rhs + _bdot(c["dinv"] - eye, rhs)
        c["attn"] = _dot_nt(c["q"].astype(BF16), c["k16"]) * c["decay"]
    for c in chains:
        c["v_new"] = c["sol"][:, :HEAD_DIM] - _bdot(c["sol"][:, HEAD_DIM:], c["s"])
    for c in chains:
        c["o_ref"][c["b"], :, c["sl"]] = (_bdot(c["q"] * c["eb"], c["s"])
                                          + _bdot(c["attn"], c["v_new"])).astype(c["o_ref"].dtype)
        s_ref[c["d"], c["b"], c["h"]] = (c["s"] * jnp.exp(c["bl"])
                                         + _bdot_tn(c["k"] * jnp.exp(c["bl"] - c["bc"]), c["v_new"]))


def _state_scratch(geo):
    return pltpu.VMEM((geo.b, N_HEADS, HEAD_DIM, HEAD_DIM), F32)


def _gdn_scan(geo, q, k, v, gb):
    tri, pair = _chunk_consts()
    view = lambda a: a.reshape(geo.b, geo.n_tot, a.shape[-1])
    chunk = lambda d, width: pl.BlockSpec((geo.b, CHUNK, width), lambda j: (0, _chunk_index(geo, d, j), 0))
    gate = lambda d: pl.BlockSpec((1, geo.b, CHUNK, LANES), lambda j: (d, 0, _chunk_index(geo, d, j), 0))
    full = lambda shape: pl.BlockSpec(shape, lambda j: (0,) * len(shape))
    q3, k3, v3 = view(q), view(k), view(v)
    gb4 = gb.reshape(2, geo.b, geo.n_tot, LANES)
    o_f, o_b = pl.pallas_call(
        functools.partial(_gdn_scan_kernel, geo.b),
        grid=(geo.n_tot // CHUNK,),
        in_specs=[chunk(0, MIX_W)] * 3 + [chunk(1, MIX_W)] * 3 + [gate(0), gate(1),
                  full((2, CHUNK, CHUNK)), full((2, N_LEVELS, CHUNK, CHUNK))],
        out_specs=[chunk(0, MIX_W), chunk(1, MIX_W)],
        out_shape=[jax.ShapeDtypeStruct((geo.b, geo.n_tot, MIX_W), BF16)] * 2,
        scratch_shapes=[pltpu.VMEM((2, geo.b, N_HEADS, HEAD_DIM, HEAD_DIM), F32)],
        compiler_params=_cparams(("arbitrary",)),
        name="gdn_scan",
    )(q3, k3, v3, q3, k3, v3, gb4, gb4, tri, pair)
    return o_f.reshape(geo.m, MIX_W), o_b.reshape(geo.m, MIX_W)


def _hgrn_consts():
    t = np.arange(CHUNK)
    tt, uu = t[:, None], t[None, :]
    a_all, rowhi, pair = [(uu <= tt)], [], [(uu == tt)]
    for lv in range(N_LEVELS):
        hi = ((t >> lv) & 1) == 1
        same_half = (uu >> lv) == (tt >> lv)
        a_all.append(same_half & np.where(hi[:, None], uu <= tt, uu > tt))
        rowhi.append(np.broadcast_to(hi[:, None], (CHUNK, HEAD_DIM)))
        pair.append(((uu >> (lv + 1)) == (tt >> (lv + 1))) & hi[:, None] & (~hi)[None, :])
    mirror = lambda a: a[::-1, ::-1] if a.shape[1] == CHUNK else a[::-1]
    stack = lambda xs: np.stack([np.stack(xs), np.stack([mirror(x) for x in xs])]).astype(np.float32)
    a = stack(a_all)
    return (jnp.asarray(a.reshape(2, (N_LEVELS + 1) * CHUNK, CHUNK)), jnp.asarray(stack(rowhi)),
            jnp.asarray(stack(pair)))


def _hgrn_scan_kernel(n_batch, q_ref, i_ref, z_ref, lb_ref, a_ref, rowhi_ref, pair_ref, o_ref, st_ref):
    @pl.when(pl.program_id(1) == 0)
    def _():
        st_ref[...] = jnp.zeros_like(st_ref)

    lb = lb_ref[0]
    log_lb = jnp.log(jnp.maximum(lb, LB_MIN))
    log_1mlb = jnp.log1p(-lb)
    a16 = a_ref[0].astype(BF16)
    a16 = jnp.concatenate([a16, a16, a16], axis=1)

    def prep(b):
        z = z_ref[b]
        other = log_1mlb + _log_sigmoid(z)
        log_f = jnp.maximum(log_lb, other) + jnp.log1p(jnp.exp(-jnp.abs(log_lb - other)))
        return dict(b=b, key=(1.0 - lb) * _sigmoid(-z), q=_silu(q_ref[b].astype(F32)) * HEAD_DIM ** -0.5,
                    v=i_ref[b],
                    e_all=_dot_exact01(a16, log_f),
                    ltot=jnp.sum(log_f, axis=0, keepdims=True))

    def levels(pb, h):
        sl = slice(h * HEAD_DIM, (h + 1) * HEAD_DIM)
        q, k, e = pb["q"][:, sl], pb["key"][:, sl], pb["e_all"][:, sl]
        scores = pair_ref[0, 0] * _bdot_nt(q, k)
        for lv in range(N_LEVELS):
            z16 = (jnp.where(rowhi_ref[0, lv] > 0.5, q, k) * jnp.exp(e[(lv + 1) * CHUNK:(lv + 2) * CHUNK])).astype(BF16)
            scores = scores + pair_ref[0, lv + 1] * _dot_nt(z16, z16)
        return dict(b=pb["b"], h=h, sl=sl, q=q, k=k, v=pb["v"][:, sl], bh=e[:CHUNK], bl=pb["ltot"][:, sl],
                    scores=scores)

    def tail(c):
        st = st_ref[c["b"], c["h"]]
        o_ref[0, c["b"], :, c["sl"]] = (_bdot(c["scores"], c["v"])
                                        + _bdot_nt(c["q"] * jnp.exp(c["bh"]), st)).astype(o_ref.dtype)
        st_ref[c["b"], c["h"]] = st * jnp.exp(c["bl"]) + _bdot_tn(c["v"], c["k"] * jnp.exp(c["bl"] - c["bh"]))

    pending = None
    pb = prep(0)
    for b in range(n_batch):
        pb_next = prep(b + 1) if b + 1 < n_batch else None
        for h in range(N_HEADS):
            cur = levels(pb, h)
            if pending is not None:
                tail(pending)
            pending = cur
        pb = pb_next
    tail(pending)


def _hgrn_scan(geo, p, pc, lb):
    a_all, rowhi, pair = _hgrn_consts()
    p3 = p.reshape(geo.b, geo.n_tot, N_PROJ)
    pc3 = pc.reshape(geo.b, geo.n_tot, N_CRIT)
    return pl.pallas_call(
        functools.partial(_hgrn_scan_kernel, geo.b),
        grid=_scan_grid(geo),
        in_specs=[_chunk_spec(geo, MIX_W, COL_HG_Q // MIX_W), _chunk_spec(geo, MIX_W, COL_HG_I // MIX_W),
                  pl.BlockSpec((geo.b, CHUNK, MIX_W),
                               lambda d, j: (0, _chunk_index(geo, d, j), CCOL_HG_F // MIX_W + d)),
                  _per_dir_spec((1, MIX_W)), _per_dir_spec(((N_LEVELS + 1) * CHUNK, CHUNK)),
                  _per_dir_spec((N_LEVELS, CHUNK, HEAD_DIM)), _per_dir_spec((N_LEVELS + 1, CHUNK, CHUNK))],
        out_specs=_dir_chunk_spec(geo, MIX_W),
        out_shape=jax.ShapeDtypeStruct((2, geo.b, geo.n_tot, MIX_W), BF16),
        scratch_shapes=[_state_scratch(geo)],
        compiler_params=_cparams(("parallel", "arbitrary")),
        name="hgrn_scan",
    )(p3, p3, pc3, lb.reshape(2, 1, MIX_W), a_all, rowhi, pair)


def _ret_consts(n_lat):
    t = np.arange(CHUNK, dtype=np.float64)
    log_gamma = np.log1p(-np.exp2(-5.0 - np.arange(N_HEADS, dtype=np.float64)))
    dmat, qs, ks = [], [], []
    for rev in (False, True):
        te = t[::-1] if rev else t
        diff = te[:, None] - te[None, :]
        dmat.append(np.stack([np.where(diff >= 0, np.exp(lg * np.maximum(diff, 0)), 0.0) for lg in log_gamma]))
        qs.append(np.repeat(np.exp(log_gamma[None, :] * (te[:, None] + 1.0)), HEAD_DIM, axis=1))
        ks.append(np.repeat(np.exp(log_gamma[None, :] * (CHUNK - 1.0 - te[:, None])), HEAD_DIM, axis=1))
    g_chunk = np.repeat(np.exp(log_gamma * CHUNK), HEAD_DIM)[None, :]
    row = np.repeat(np.arange(n_lat // GRID_W, dtype=np.float32), GRID_W)
    colp = np.tile(np.arange(GRID_W, dtype=np.float32), n_lat // GRID_W)
    n_freq = HEAD_DIM // 4
    inv_freq = (np.float32(ROPE_BASE) ** (-np.arange(n_freq, dtype=np.float32) / n_freq)).astype(np.float32)
    ang = np.concatenate([row[:, None] * inv_freq, colp[:, None] * inv_freq], axis=-1).astype(np.float32)
    cos, sin = np.cos(ang), np.sin(ang)
    f32 = lambda a: jnp.asarray(np.asarray(a, np.float32))
    return (f32(np.stack(dmat)), f32(np.stack(qs)), f32(np.stack(ks)), f32(g_chunk),
            f32(np.concatenate([cos, cos], axis=-1)), f32(np.concatenate([-sin, sin], axis=-1)))


def _ret_scan_kernel(n_batch, nc_ctx, qf_ref, kf_ref, vf_ref, qb_ref, kb_ref, vb_ref, cosf_ref, sinf_ref,
                     cosb_ref, sinb_ref, dm_ref, qs_ref, ks_ref, gc_ref, of_ref, ob_ref, s_ref):
    j = pl.program_id(0)

    @pl.when(j == 0)
    def _():
        s_ref[...] = jnp.zeros_like(s_ref)

    is_lat = j >= nc_ctx
    gc = gc_ref[...]
    chains = []
    for d, (q_ref, k_ref, v_ref, cos_ref, sin_ref, o_ref) in enumerate((
            (qf_ref, kf_ref, vf_ref, cosf_ref, sinf_ref, of_ref),
            (qb_ref, kb_ref, vb_ref, cosb_ref, sinb_ref, ob_ref))):
        cos = jnp.where(is_lat, cos_ref[...], 1.0)
        sin = jnp.where(is_lat, sin_ref[...], 0.0)
        for b in range(n_batch):
            q, k, v = q_ref[b].astype(F32), k_ref[b].astype(F32), v_ref[b]
            for h in range(N_HEADS):
                sl = slice(h * HEAD_DIM, (h + 1) * HEAD_DIM)
                qh, kh = q[:, sl], k[:, sl]
                rq = qh * cos + pltpu.roll(qh, HEAD_DIM // 2, 1) * sin
                rk = (kh * cos + pltpu.roll(kh, HEAD_DIM // 2, 1) * sin) * HEAD_DIM ** -0.5
                chains.append(dict(d=d, b=b, h=h, sl=sl, o_ref=o_ref, rq=rq, rk=rk, v=v[:, sl], s=s_ref[d, b, h]))
    for c in chains:
        c["scores"] = _bdot_nt(c["rq"], c["rk"]) * dm_ref[c["d"], c["h"]]
    for c in chains:
        d, sl = c["d"], c["sl"]
        c["o_ref"][c["b"], :, sl] = (_bdot(c["scores"], c["v"])
                                     + _bdot(c["rq"] * qs_ref[d, :, sl], c["s"])).astype(c["o_ref"].dtype)
        s_ref[d, c["b"], c["h"]] = c["s"] * gc[:, sl] + _bdot_tn(c["rk"] * ks_ref[d, :, sl], c["v"])


def _ret_scan(geo, p):
    dmat, qs, ks, gc, cos2, sin2 = _ret_consts(geo.n_lat)
    nc_ctx = geo.n_ctx // CHUNK
    p3 = p.reshape(geo.b, geo.n_tot, N_PROJ)
    chunk = lambda d, c: pl.BlockSpec((geo.b, CHUNK, MIX_W), lambda j: (0, _chunk_index(geo, d, j), c // MIX_W))
    rope = lambda d: pl.BlockSpec((CHUNK, HEAD_DIM), lambda j: (jnp.where(j < nc_ctx, 0, _chunk_index(geo, d, j)), 0))
    full = lambda shape: pl.BlockSpec(shape, lambda j: (0,) * len(shape))
    o_f, o_b = pl.pallas_call(
        functools.partial(_ret_scan_kernel, geo.b, nc_ctx),
        grid=(geo.n_tot // CHUNK,),
        in_specs=[chunk(0, COL_RET_Q), chunk(0, COL_RET_K), chunk(0, COL_RET_V),
                  chunk(1, COL_RET_Q), chunk(1, COL_RET_K), chunk(1, COL_RET_V),
                  rope(0), rope(0), rope(1), rope(1),
                  full((2, N_HEADS, CHUNK, CHUNK)), full((2, CHUNK, MIX_W)), full((2, CHUNK, MIX_W)),
                  full((1, MIX_W))],
        out_specs=[pl.BlockSpec((geo.b, CHUNK, MIX_W), lambda j: (0, _chunk_index(geo, 0, j), 0)),
                   pl.BlockSpec((geo.b, CHUNK, MIX_W), lambda j: (0, _chunk_index(geo, 1, j), 0))],
        out_shape=[jax.ShapeDtypeStruct((geo.b, geo.n_tot, MIX_W), BF16)] * 2,
        scratch_shapes=[pltpu.VMEM((2, geo.b, N_HEADS, HEAD_DIM, HEAD_DIM), F32)],
        compiler_params=_cparams(("arbitrary",)),
        name="ret_scan",
    )(p3, p3, p3, p3, p3, p3, cos2, sin2, cos2, sin2, dmat, qs, ks, gc)
    return o_f.reshape(geo.m, MIX_W), o_b.reshape(geo.m, MIX_W)


def _lru_kernel(geo, tr, rev, prev_ref, x_ref, next_ref, cw_ref, cb_ref, wg_ref, bg_ref, lam_ref,
                o_ref, h_ref):
    b, j = pl.program_id(0), pl.program_id(1)

    @pl.when(j == 0)
    def _():
        h_ref[...] = jnp.zeros_like(h_ref)

    first, last = geo.seq_edges(_lru_block(geo, tr, rev, b, j), tr)
    xb = _token_conv(prev_ref, x_ref, next_ref, cw_ref, first, last) + cb_ref[...]
    gates = _bdot(xb, wg_ref[...]) + bg_ref[...]
    gate_r = _sigmoid(gates[:, :MIX_W])
    gate_i = _sigmoid(gates[:, MIX_W:])
    log_a = LRU_C * gate_r * _log_sigmoid(lam_ref[...])
    a = jnp.exp(log_a)
    u = xb * gate_i * jnp.sqrt(jnp.maximum(-jnp.tanh(log_a) * (a * a + 1.0), SQRT_MIN))
    ng = tr // SUBLANES
    a3 = a.reshape(ng, SUBLANES, MIX_W)
    u3 = u.reshape(ng, SUBLANES, MIX_W)
    sub = lax.broadcasted_iota(jnp.int32, (ng, SUBLANES, MIX_W), 1)
    step = 1
    while step < SUBLANES:
        shift = SUBLANES - step if rev else step
        valid = (sub < SUBLANES - step) if rev else (sub >= step)
        u3 = jnp.where(valid, a3 * pltpu.roll(u3, shift, 1) + u3, u3)
        a3 = jnp.where(valid, a3 * pltpu.roll(a3, shift, 1), a3)
        step *= 2
    edge = 0 if rev else SUBLANES - 1
    h_in = jnp.broadcast_to(h_ref[...], (SUBLANES, MIX_W))
    h_groups = [None] * ng
    for g in (range(ng - 1, -1, -1) if rev else range(ng)):
        h_groups[g] = h_in
        h_in = a3[g, edge:edge + 1] * h_in + u3[g, edge:edge + 1]
    h = a3 * jnp.stack(h_groups, axis=0) + u3
    o_ref[...] = h.reshape(tr, MIX_W).astype(o_ref.dtype)
    h_ref[...] = h_in[0:1]


def _lru_block(geo, tr, rev, b, j):
    nb_ctx, nb_lat = geo.n_ctx // tr, geo.n_lat // tr
    jc = nb_ctx - 1 - j if rev else j
    jl = nb_lat - 1 - (j - nb_ctx) if rev else j - nb_ctx
    return b * (geo.n_tot // tr) + jnp.where(j < nb_ctx, nb_lat + jc, jl)


def _lru_scan(geo, p, conv_w, conv_b, wa, ba, wx, bx, lam, rev):
    tr = _largest_tile(geo.row_tile, 256)
    d = 1 if rev else 0
    blockdiag = lambda w: jax.scipy.linalg.block_diag(*[w[n] for n in range(LRU_BLOCKS)])
    wg = jnp.concatenate([blockdiag(wa[d]), blockdiag(wx[d])], axis=1).astype(BF16)
    bg = jnp.concatenate([ba[d], bx[d]]).astype(F32).reshape(1, 2 * MIX_W)
    blk = lambda b, j: _lru_block(geo, tr, rev, b, j)
    full = lambda shape: pl.BlockSpec(shape, lambda b, j: (0,) * len(shape))
    return pl.pallas_call(
        functools.partial(_lru_kernel, geo, tr, rev),
        grid=(geo.b, (geo.n_ctx + geo.n_lat) // tr),
        in_specs=_halo_specs(geo, tr, MIX_W, COL_LRU_X // MIX_W, blk) + [
                  full((CONV_W, MIX_W)), full((1, MIX_W)), full((MIX_W, 2 * MIX_W)),
                  full((1, 2 * MIX_W)), full((1, MIX_W))],
        out_specs=pl.BlockSpec((tr, MIX_W), lambda b, j: (blk(b, j), 0)),
        out_shape=jax.ShapeDtypeStruct((geo.m, MIX_W), BF16),
        scratch_shapes=[pltpu.VMEM((1, MIX_W), F32)],
        compiler_params=_cparams(("parallel", "arbitrary")),
        name="lru_scan_bwd" if rev else "lru_scan_fwd",
    )(p, p, p, conv_w.astype(F32), conv_b.astype(F32).reshape(1, MIX_W), wg, bg,
      lam[d].astype(F32).reshape(1, MIX_W))


def _gelu_tanh(x):
    return 0.5 * x * (1.0 + jnp.tanh(math.sqrt(2.0 / math.pi) * (x + 0.044715 * (x * x * x))))


def _gated_head_norm(o, g, z):
    parts = []
    for h in range(N_HEADS):
        oh = o[:, h * HEAD_DIM:(h + 1) * HEAD_DIM]
        parts.append(oh * lax.rsqrt(jnp.mean(oh * oh, axis=-1, keepdims=True) + EPS) * g)
    return jnp.concatenate(parts, axis=-1) * _silu(z)


def _merge_kernel(ogf_ref, ogb_ref, oh_ref, olf_ref, olb_ref, orf_ref, orb_ref, gz_ref, hg_ref, ly_ref, rg_ref,
                  mg_ref, x_ref, gt_ref, gng_ref, hng_ref, rng_ref, wb_ref, wo_ref, out_ref):
    f32 = lambda v: v.astype(F32)
    branches = (
        _gated_head_norm(f32(ogf_ref[...]) + f32(ogb_ref[...]), gng_ref[...], f32(gz_ref[...])),
        _gated_head_norm(f32(oh_ref[0]) + f32(oh_ref[1]), hng_ref[...], f32(hg_ref[...])),
        (f32(olf_ref[...]) + f32(olb_ref[...])) * _gelu_tanh(f32(ly_ref[...])),
        _gated_head_norm(f32(orf_ref[...]) + f32(orb_ref[...]), rng_ref[...], f32(rg_ref[...])),
    )
    mix = None
    for n, y in enumerate(branches):
        term = _sigmoid(f32(mg_ref[:, n * D_MODEL:(n + 1) * D_MODEL])) * _bdot(y, wb_ref[n])
        mix = term if mix is None else mix + term
    out_ref[...] = x_ref[...] + gt_ref[0] * _bdot(mix, wo_ref[...])


def _merge(geo, compact, x_rows, p, o_gdn, o_hgrn, o_lru_f, o_lru_b, o_ret, mods3, gdn_g, hgrn_g, ret_g,
           w_branch, w_out):
    tm = _largest_tile(geo.row_tile, 256)
    n_rows = geo.m_lat if compact else geo.m
    src = lambda i: geo.full_tile(i, tm, compact)
    both = pl.BlockSpec((2, tm, MIX_W), lambda i: (0, src(i), 0))
    rows = lambda width: pl.BlockSpec((tm, width), lambda i: (src(i), 0))
    col = lambda c, width: pl.BlockSpec((tm, width), lambda i: (src(i), c // width))
    full = lambda shape: pl.BlockSpec(shape, lambda i: (0,) * len(shape))
    hd = lambda g: g.astype(F32).reshape(1, HEAD_DIM)
    return pl.pallas_call(
        _merge_kernel,
        grid=(n_rows // tm,),
        in_specs=[rows(MIX_W), rows(MIX_W), both, rows(MIX_W), rows(MIX_W), rows(MIX_W), rows(MIX_W),
                  col(COL_GDN_Z, MIX_W), col(COL_HG_G, MIX_W), col(COL_LRU_Y, MIX_W), col(COL_RET_G, MIX_W),
                  col(COL_MERGE, N_BRANCH * D_MODEL), rows(D_MODEL),
                  pl.BlockSpec((1, 1, D_MODEL), lambda i: (geo.mod_row(i, tm, compact), 0, 2)),
                  full((1, HEAD_DIM)), full((1, HEAD_DIM)), full((1, HEAD_DIM)),
                  full((N_BRANCH, MIX_W, D_MODEL)), full((D_MODEL, D_MODEL))],
        out_specs=pl.BlockSpec((tm, D_MODEL), lambda i: (i, 0)),
        out_shape=jax.ShapeDtypeStruct((n_rows, D_MODEL), F32),
        compiler_params=_cparams(("parallel",)),
        name="merge",
    )(o_gdn[0], o_gdn[1], o_hgrn, o_lru_f, o_lru_b, o_ret[0], o_ret[1], p, p, p, p, p, x_rows, mods3, hd(gdn_g), hd(hgrn_g),
      hd(ret_g), w_branch.astype(BF16), w_out.astype(BF16))


ROW_TILE = D_MODEL // LANES
assert ROW_TILE == SUBLANES


def _store_row_tiles(ref, val):
    n = val.shape[0]
    v = jnp.stack([val[:, s * LANES:(s + 1) * LANES].reshape(n // SUBLANES, SUBLANES, LANES)
                   for s in range(ROW_TILE)], axis=1)
    ref[...] = jnp.swapaxes(v, 1, 2).reshape(n * ROW_TILE, LANES)


def _load_row_tiles(ref_at, start, n):
    v = ref_at[pl.ds(start, n * ROW_TILE), :].reshape(n // SUBLANES, SUBLANES, ROW_TILE, LANES)
    v = jnp.swapaxes(v, 1, 2)
    return jnp.concatenate([v[:, s].reshape(n, LANES) for s in range(ROW_TILE)], axis=1)


def _router_kernel(n_experts, x_ref, g_ref, sc_ref, sh_ref, rw_ref, rb_ref, f_ref, idx_ref, wt_ref):
    f = _rms_mod(x_ref[...], g_ref[...], sc_ref[0], sh_ref[0])
    _store_row_tiles(f_ref, f)
    logits = _dot(f, rw_ref[...], precision=HIGHEST) + rb_ref[...]
    lane = lax.broadcasted_iota(jnp.int32, logits.shape, 1)
    logits = jnp.where(lane < n_experts, logits, NEG_BIG)
    vals, idxs = [], []
    for _ in range(TOP_K):
        mx = jnp.max(logits, axis=-1, keepdims=True)
        ix = jnp.min(jnp.where(logits == mx, lane, LANES), axis=-1, keepdims=True)
        vals.append(mx)
        idxs.append(ix)
        logits = jnp.where(lane == ix, -jnp.inf, logits)
    ex = [jnp.exp(vl - vals[0]) for vl in vals]
    tot = ex[0] + ex[1] + ex[2] + ex[3]
    idx_out = jnp.zeros(lane.shape, jnp.int32)
    wt_out = jnp.zeros(lane.shape, F32)
    for kk in range(TOP_K):
        idx_out = jnp.where(lane == kk, idxs[kk], idx_out)
        wt_out = jnp.where(lane == kk, ex[kk] / tot, wt_out)
    idx_ref[...] = idx_out
    wt_ref[...] = wt_out


def _router(geo, compact, x_rows, g, mods3, router_w, router_b):
    n_rows, d = x_rows.shape
    n_experts = router_w.shape[1]
    tm = _largest_tile(geo.row_tile, 256)
    rw = jnp.pad(router_w.astype(F32), ((0, 0), (0, LANES - n_experts)))
    rb = jnp.pad(router_b.astype(F32), (0, LANES - n_experts)).reshape(1, LANES)
    rows = lambda width: pl.BlockSpec((tm, width), lambda i: (i, 0))
    full = lambda shape: pl.BlockSpec(shape, lambda i: (0,) * len(shape))
    return pl.pallas_call(
        functools.partial(_router_kernel, n_experts),
        grid=(n_rows // tm,),
        in_specs=[rows(d), full((1, d)),
                  pl.BlockSpec((1, 1, d), lambda i: (geo.mod_row(i, tm, compact), 0, 4)),
                  pl.BlockSpec((1, 1, d), lambda i: (geo.mod_row(i, tm, compact), 0, 3)),
                  full((d, LANES)), full((1, LANES))],
        out_specs=[pl.BlockSpec((tm * ROW_TILE, LANES), lambda i: (i, 0)), rows(LANES), rows(LANES)],
        out_shape=[jax.ShapeDtypeStruct((n_rows * ROW_TILE, LANES), F32),
                   jax.ShapeDtypeStruct((n_rows, LANES), jnp.int32), jax.ShapeDtypeStruct((n_rows, LANES), F32)],
        compiler_params=_cparams(("parallel",)),
        name="router",
    )(x_rows, g.astype(F32).reshape(1, d), mods3, mods3, rw, rb)


def _route_plan(top_idx, n_experts):
    m = top_idx.shape[0]
    n_assign = m * TOP_K
    flat_e = top_idx.reshape(-1)
    order = jnp.argsort(flat_e).astype(jnp.int32)
    blk = _largest_tile(n_assign, 512)
    onehot = (flat_e[:, None] == jnp.arange(n_experts, dtype=jnp.int32)[None, :]).astype(F32)
    oh3 = onehot.reshape(n_assign // blk, blk, n_experts)
    strict_lower = jnp.asarray(np.tril(np.ones((blk, blk), np.float32), k=-1))
    block_tot = jnp.sum(oh3, axis=1)
    block_off = jnp.cumsum(block_tot, axis=0) - block_tot
    prefix = jnp.einsum("ij,bjk->bik", strict_lower, oh3) + block_off[:, None, :]
    rank_in_expert = jnp.sum(prefix * oh3, axis=-1).reshape(n_assign).astype(jnp.int32)
    counts = jnp.sum(block_tot, axis=0).astype(jnp.int32)
    padded = (counts + ROUTE_BLOCK - 1) // ROUTE_BLOCK * ROUTE_BLOCK
    pad_end = jnp.cumsum(padded)
    pad_start = pad_end - padded
    start = jnp.cumsum(counts) - counts
    n_blocks = -(-n_assign // ROUTE_BLOCK) + n_experts
    cap = n_blocks * ROUTE_BLOCK
    block_row0 = jnp.arange(n_blocks, dtype=jnp.int32) * ROUTE_BLOCK
    block_expert = jnp.minimum(jnp.sum(pad_end[None, :] <= block_row0[:, None], axis=1, dtype=jnp.int32),
                               n_experts - 1)
    n_used = (pad_end[-1:] // ROUTE_BLOCK).astype(jnp.int32)
    row = jnp.arange(cap, dtype=jnp.int32)
    row_e = jnp.repeat(block_expert, ROUTE_BLOCK)
    within = row - pad_start[row_e]
    valid = jnp.logical_and(within >= 0, within < counts[row_e])
    src_tok = jnp.where(valid, order[jnp.clip(start[row_e] + within, 0, n_assign - 1)] // TOP_K, 0)
    dest = pad_start[flat_e] + rank_in_expert
    return src_tok.reshape(n_blocks, 1, ROUTE_BLOCK), block_expert, n_used, dest


def _start_row_gather(idx_ref, n, src_hbm, buf, slot, sem):
    for r in range(n):
        src_row = pl.multiple_of(idx_ref[0, 0, r] * ROW_TILE, ROW_TILE)
        pltpu.make_async_copy(src_hbm.at[pl.ds(src_row, ROW_TILE)], buf.at[slot, pl.ds(r * ROW_TILE, ROW_TILE)],
                              sem.at[slot]).start(priority=r % 2)


def _wait_row_gather(n, src_hbm, buf, slot, sem):
    pltpu.make_async_copy(src_hbm.at[pl.ds(0, n * ROW_TILE)], buf.at[slot], sem.at[slot]).wait()


def _gather_step(cur_ref, nxt_ref, n, src_hbm, buf, sem):
    i, nb = pl.program_id(0), pl.num_programs(0)
    slot = i % 2

    @pl.when(i == 0)
    def _():
        _start_row_gather(cur_ref, n, src_hbm, buf, 0, sem)

    _wait_row_gather(n, src_hbm, buf, slot, sem)
    _start_row_gather(nxt_ref, n, src_hbm, buf, 1 - slot, sem)
    return slot


def _gather_drain(n, src_hbm, buf, slot, sem):
    @pl.when(pl.program_id(0) == pl.num_programs(0) - 1)
    def _():
        _wait_row_gather(n, src_hbm, buf, 1 - slot, sem)


def _expert_kernel(be_ref, nu_ref, src_ref, nxt_ref, f_hbm, w1_ref, b1_ref, w2_ref, b2_ref, y_ref,
                   buf, sem, w1b_ref, w2b_ref):
    i = pl.program_id(0)
    ff = w2_ref.shape[1]
    slot = _gather_step(src_ref, nxt_ref, ROUTE_BLOCK, f_hbm, buf, sem)

    changed = jnp.logical_or(i == 0, be_ref[i] != be_ref[jnp.maximum(i - 1, 0)])
    r1, r2 = w1_ref.shape[1] // CAST_STEPS, w2_ref.shape[1] // CAST_STEPS

    def cast_rows(t, carry):
        o1, o2 = pl.multiple_of(t * r1, r1), pl.multiple_of(t * r2, r2)
        w1b_ref[pl.ds(o1, r1), :] = w1_ref[0, pl.ds(o1, r1), :].astype(BF16)
        w2b_ref[pl.ds(o2, r2), :] = w2_ref[0, pl.ds(o2, r2), :].astype(BF16)
        return carry

    lax.fori_loop(0, jnp.where(changed, CAST_STEPS, 0), cast_rows, 0)

    @pl.when(i >= nu_ref[0])
    def _():
        y_ref[...] = jnp.zeros_like(y_ref)

    @pl.when(i < nu_ref[0])
    def _():
        z = _bdot(_load_row_tiles(buf.at[slot], 0, ROUTE_BLOCK), w1b_ref[...]) + b1_ref[0]
        glu = jnp.minimum(z[:, :ff], SWIGLU_LIMIT)
        lin = jnp.clip(z[:, ff:], -SWIGLU_LIMIT, SWIGLU_LIMIT)
        act = glu * _sigmoid(SWIGLU_ALPHA * glu) * (lin + 1.0)
        _store_row_tiles(y_ref, _bdot(act, w2b_ref[...]) + b2_ref[0])

    _gather_drain(ROUTE_BLOCK, f_hbm, buf, slot, sem)


def _experts(f_tiles, src_tok, block_expert, n_used, layer, w1_all, b1_all, w2_all, b2_all):
    n_blocks = src_tok.shape[0]
    n_layers, n_experts, d, ff2 = w1_all.shape
    ff = w2_all.shape[2]
    w1 = w1_all.reshape(n_layers * n_experts, d, ff2)
    w2 = w2_all.reshape(n_layers * n_experts, ff, d)
    b1 = b1_all.reshape(n_layers * n_experts, ff2)
    b2 = b2_all.reshape(n_layers * n_experts, d)
    block_expert = block_expert + layer * n_experts
    n_experts = n_layers * n_experts
    grid_spec = pltpu.PrefetchScalarGridSpec(
        num_scalar_prefetch=2,
        grid=(n_blocks,),
        in_specs=[pl.BlockSpec((1, 1, ROUTE_BLOCK), lambda i, be, nu: (i, 0, 0), memory_space=pltpu.SMEM),
                  pl.BlockSpec((1, 1, ROUTE_BLOCK), lambda i, be, nu: (jnp.minimum(i + 1, n_blocks - 1), 0, 0),
                               memory_space=pltpu.SMEM),
                  pl.BlockSpec(memory_space=pl.ANY),
                  pl.BlockSpec((1, d, ff2), lambda i, be, nu: (be[i], 0, 0)),
                  pl.BlockSpec((1, 1, ff2), lambda i, be, nu: (be[i], 0, 0)),
                  pl.BlockSpec((1, ff, d), lambda i, be, nu: (be[i], 0, 0)),
                  pl.BlockSpec((1, 1, d), lambda i, be, nu: (be[i], 0, 0))],
        out_specs=pl.BlockSpec((ROUTE_BLOCK * ROW_TILE, LANES), lambda i, be, nu: (i, 0)),
        scratch_shapes=[pltpu.VMEM((2, ROUTE_BLOCK * ROW_TILE, LANES), F32), pltpu.SemaphoreType.DMA((2,)),
                        pltpu.VMEM((d, ff2), BF16), pltpu.VMEM((ff, d), BF16)],
    )
    return pl.pallas_call(
        _expert_kernel,
        grid_spec=grid_spec,
        out_shape=jax.ShapeDtypeStruct((n_blocks * ROUTE_BLOCK * ROW_TILE, LANES), F32),
        compiler_params=_cparams(("arbitrary",)),
        name="experts",
    )(block_expert, n_used, src_tok, src_tok, f_tiles, w1, b1.astype(F32).reshape(n_experts, 1, ff2),
      w2, b2.astype(F32).reshape(n_experts, 1, d))


def _combine_kernel(tc, final, dst_ref, nxt_ref, y_hbm, x_ref, wt_ref, gt_ref, fg_ref, out_ref, buf, sem):
    n = TOP_K * tc
    slot = _gather_step(dst_ref, nxt_ref, n, y_hbm, buf, sem)
    wt = wt_ref[...]
    acc = None
    for kk in range(TOP_K):
        term = wt[:, kk:kk + 1] * _load_row_tiles(buf.at[slot], kk * tc * ROW_TILE, tc)
        acc = term if acc is None else acc + term
    out = x_ref[...] + gt_ref[0] * acc
    if final:
        out = out * lax.rsqrt(jnp.mean(out * out, axis=-1, keepdims=True) + EPS) * fg_ref[...]
    out_ref[...] = out
    _gather_drain(n, y_hbm, buf, slot, sem)


def _combine(geo, y_pad, dest, x_rows, wt, mods3, final_g, final):
    n_rows, d = x_rows.shape
    tc = _largest_tile(geo.row_tile, 128)
    nt = n_rows // tc
    dst = dest.reshape(nt, tc, TOP_K).transpose(0, 2, 1).reshape(nt, 1, TOP_K * tc)
    rows = lambda width: pl.BlockSpec((tc, width), lambda i: (i, 0))
    return pl.pallas_call(
        functools.partial(_combine_kernel, tc, final),
        grid=(nt,),
        in_specs=[pl.BlockSpec((1, 1, TOP_K * tc), lambda i: (i, 0, 0), memory_space=pltpu.SMEM),
                  pl.BlockSpec((1, 1, TOP_K * tc), lambda i: (jnp.minimum(i + 1, nt - 1), 0, 0),
                               memory_space=pltpu.SMEM),
                  pl.BlockSpec(memory_space=pl.ANY), rows(d), rows(LANES),
                  pl.BlockSpec((1, 1, d), lambda i: (geo.mod_row(i, tc, final), 0, 5)),
                  pl.BlockSpec((1, d), lambda i: (0, 0))],
        out_specs=rows(d),
        out_shape=jax.ShapeDtypeStruct((n_rows, d), F32),
        scratch_shapes=[pltpu.VMEM((2, TOP_K * tc * ROW_TILE, LANES), F32), pltpu.SemaphoreType.DMA((2,))],
        compiler_params=_cparams(("arbitrary",)),
        name="moe_combine",
    )(dst, dst, y_pad, x_rows, wt, mods3, final_g.astype(F32).reshape(1, d))


def _moe(geo, x_rows, g, mods3, router_w, router_b, layer, w1_all, b1_all, w2_all, b2_all, final_g, final):
    f_tiles, top_idx, top_w = _router(geo, final, x_rows, g, mods3, router_w, router_b)
    src_tok, block_expert, n_used, dest = _route_plan(top_idx[:, :TOP_K], router_w.shape[1])
    y_tiles = _experts(f_tiles, src_tok, block_expert, n_used, layer, w1_all, b1_all, w2_all, b2_all)
    return _combine(geo, y_tiles, dest, x_rows, top_w, mods3, final_g, final)


def kernel(x, c, ctx, c_ctx, w_mod, b_mod, norm1_g, norm2_g, w_in, gdn_conv_w, gdn_a_log, gdn_dt_bias, gdn_norm_g, hgrn_lb, hgrn_norm_g, lru_conv_w, lru_conv_b, lru_wa, lru_ba, lru_wx, lru_bx, lru_lambda, ret_norm_g, w_branch, w_out, router_w, router_b, moe_w1, moe_b1, moe_w2, moe_b2, final_norm_g):
    b, n_lat, d = x.shape
    n_ctx = ctx.shape[1]
    depth = w_mod.shape[0]
    assert d == D_MODEL and w_branch.shape[2] == MIX_W
    geo = _Geo(b, n_lat, n_ctx)
    lb_soft = jax.nn.softmax(hgrn_lb.astype(F32), axis=0)
    lower_bounds = jnp.clip(jnp.cumsum(lb_soft, axis=0) - lb_soft[0], 0.0, LB_MAX)
    n_cond = -(-(b + 1) // SUBLANES) * SUBLANES
    cc = jnp.concatenate([c, c_ctx[None, :], jnp.zeros((n_cond - b - 1, d), c.dtype)], axis=0)
    mods = _modulation(cc.astype(F32), w_mod, b_mod)
    rows = jnp.concatenate([x, ctx], axis=1).reshape(geo.m, d)
    flat = lambda o: o.reshape(2, geo.m, MIX_W)
    for l in range(depth):
        last = l == depth - 1
        mods3 = mods[l].reshape(n_cond, 1, N_MOD * d)
        p, pc = _norm_proj(geo, rows, norm1_g[l], mods3, *_reorder_w_in(w_in[l]))
        gq, gk, gv, gb = _gdn_prep(geo, p, pc, gdn_conv_w[l], gdn_a_log[l], gdn_dt_bias[l])
        o_gdn = _gdn_scan(geo, gq, gk, gv, gb)
        o_hgrn = flat(_hgrn_scan(geo, p, pc, lower_bounds[l]))
        lru_args = (geo, p, lru_conv_w[l], lru_conv_b[l], lru_wa[l], lru_ba[l], lru_wx[l], lru_bx[l],
                    lru_lambda[l])
        o_lru_f = _lru_scan(*lru_args, rev=False)
        o_lru_b = _lru_scan(*lru_args, rev=True)
        o_ret = _ret_scan(geo, p)
        rows = _merge(geo, last, rows, p, o_gdn, o_hgrn, o_lru_f, o_lru_b, o_ret, mods3, gdn_norm_g[l],
                      hgrn_norm_g[l], ret_norm_g[l], w_branch[l], w_out[l])
        rows = _moe(geo, rows, norm2_g[l], mods3, router_w[l], router_b[l], l, moe_w1, moe_b1, moe_w2, moe_b2,
                    final_norm_g, last)
    return rows.reshape(b, n_lat, d)
```

```python
import functools
import math

import numpy as np
import jax
import jax.numpy as jnp
from jax import lax
from jax.experimental import pallas as pl
from jax.experimental.pallas import tpu as pltpu

F32 = jnp.float32
BF16 = jnp.bfloat16
HIGHEST = lax.Precision.HIGHEST

GRID_W = 64
N_BRANCH = 4
HEAD_DIM = 128
LRU_BLOCKS = 8
LRU_C = 8.0
CONV_W = 4
CHUNK = 64
ROPE_BASE = 10000.0
TOP_K = 4
SWIGLU_ALPHA = 1.702
SWIGLU_LIMIT = 7.0
N_MOD = 6
EPS = 1e-6
LB_MIN = 1e-30
LB_MAX = 1.0 - 1e-4
SQRT_MIN = 1e-12

LANES = 128
SUBLANES = 8
VMEM_LIMIT_BYTES = 56 * 1024 * 1024
ROUTE_BLOCK = 256
CAST_STEPS = 8
NEG_BIG = -1e30


def _cparams(semantics):
    return pltpu.CompilerParams(dimension_semantics=semantics, vmem_limit_bytes=VMEM_LIMIT_BYTES)


def _largest_tile(n, cap, quantum=SUBLANES):
    best = None
    for t in range(quantum, min(n, cap) + 1, quantum):
        if n % t == 0:
            best = t
    assert best is not None, (n, cap)
    return best


def _dot(a, b, precision=None):
    return jnp.dot(a, b, preferred_element_type=F32, precision=precision)


def _dot_nt(a, b):
    return lax.dot_general(a, b, (((1,), (1,)), ((), ())), preferred_element_type=F32)


def _dot_tn(a, b):
    return lax.dot_general(a, b, (((0,), (0,)), ((), ())), preferred_element_type=F32)


def _bdot(a, b):
    return _dot(a.astype(BF16), b.astype(BF16))


def _bdot_nt(a, b):
    return _dot_nt(a.astype(BF16), b.astype(BF16))


def _bdot_tn(a, b):
    return _dot_tn(a.astype(BF16), b.astype(BF16))


def _dot_exact01(a01x3, x):
    hi = x.astype(BF16)
    r1 = x - hi.astype(F32)
    mid = r1.astype(BF16)
    lo = (r1 - mid.astype(F32)).astype(BF16)
    return _dot(a01x3, jnp.concatenate([hi, mid, lo], axis=0))


def _sigmoid(x):
    return 1.0 / (1.0 + jnp.exp(-x))


def _silu(x):
    return x * _sigmoid(x)


def _softplus(x):
    return jnp.maximum(x, 0.0) + jnp.log1p(jnp.exp(-jnp.abs(x)))


def _log_sigmoid(x):
    return -_softplus(-x)


def _mod_kernel(c_ref, w_ref, b_ref, o_ref):
    o_ref[0] = _dot(_silu(c_ref[...]), w_ref[0], precision=HIGHEST) + b_ref[0]


def _modulation(cc, w_mod, b_mod):
    n_layers, d, n = w_mod.shape
    rows = cc.shape[0]
    tn = _largest_tile(n, 1024, LANES)
    return pl.pallas_call(
        _mod_kernel,
        grid=(n_layers, n // tn),
        in_specs=[pl.BlockSpec((rows, d), lambda l, j: (0, 0)),
                  pl.BlockSpec((1, d, tn), lambda l, j: (l, 0, j)),
                  pl.BlockSpec((1, 1, tn), lambda l, j: (l, 0, j))],
        out_specs=pl.BlockSpec((1, rows, tn), lambda l, j: (l, 0, j)),
        out_shape=jax.ShapeDtypeStruct((n_layers, rows, n), F32),
        compiler_params=_cparams(("parallel", "parallel")),
        name="modulation",
    )(cc, w_mod, b_mod.reshape(n_layers, 1, n))


D_MODEL = 1024
MIX_W = 512
N_HEADS = MIX_W // HEAD_DIM
COL_MERGE = 0
COL_GDN_Z = 4096
COL_GDN_QKV = 4608
COL_HG_Q = 6144
COL_HG_I = 6656
COL_HG_G = 7168
COL_LRU_X = 7680
COL_LRU_Y = 8192
COL_RET_Q = 8704
COL_RET_K = 9216
COL_RET_V = 9728
COL_RET_G = 10240
N_PROJ = 10752
CCOL_HG_F = 0
CCOL_AB = 1024
N_CRIT = 1280


class _Geo:
    def __init__(self, b, n_lat, n_ctx):
        self.b, self.n_lat, self.n_ctx = b, n_lat, n_ctx
        self.n_tot = n_lat + n_ctx
        self.m_lat = b * n_lat
        self.m = b * self.n_tot
        assert n_lat % CHUNK == 0 and n_ctx % CHUNK == 0 and n_lat % GRID_W == 0
        self.row_tile = math.gcd(n_lat, n_ctx)

    def mod_row(self, i, tm, compact=False):
        r0 = i * tm
        if compact:
            return r0 // self.n_lat
        return jnp.where(r0 % self.n_tot < self.n_lat, r0 // self.n_tot, self.b)

    def full_tile(self, i, tm, compact=False):
        if not compact:
            return i
        per_lat, per_tot = self.n_lat // tm, self.n_tot // tm
        return (i // per_lat) * per_tot + i % per_lat

    def seq_edges(self, i, tr):
        pos = (i * tr) % self.n_tot
        first = jnp.logical_or(pos == 0, pos == self.n_lat)
        last = jnp.logical_or(pos + tr == self.n_lat, pos + tr == self.n_tot)
        return first, last


def _reorder_w_in(w_in):
    w = w_in.astype(BF16)
    d = w.shape[0]
    pad = lambda n: jnp.zeros((d, n), BF16)
    ab = [jnp.concatenate([w[:, 2048 + 4 * dr:2052 + 4 * dr], w[:, 2056 + 4 * dr:2060 + 4 * dr],
                           pad(LANES - 8)], axis=1) for dr in range(2)]
    main = jnp.concatenate([w[:, 7696:11792], w[:, 1536:2048], w[:, 0:1536], w[:, 2064:3088],
                            w[:, 4112:7696]], axis=1)
    crit = jnp.concatenate([w[:, 3088:4112], ab[0], ab[1]], axis=1)
    assert main.shape[1] == N_PROJ and crit.shape[1] == N_CRIT
    return main, crit


def _rms_mod(x, g, sc, sh):
    y = x * lax.rsqrt(jnp.mean(x * x, axis=-1, keepdims=True) + EPS) * g
    return y * (1.0 + sc) + sh


def _norm_kernel(x_ref, g_ref, sc_ref, sh_ref, h_ref):
    h_ref[...] = _rms_mod(x_ref[...], g_ref[...], sc_ref[0], sh_ref[0]).astype(BF16)


def _proj_kernel(h_ref, w_ref, o_ref):
    o_ref[...] = _dot(h_ref[...], w_ref[...]).astype(o_ref.dtype)


def _proj(h, w_p, out_dtype):
    m, d = h.shape
    n = w_p.shape[1]
    tm = _largest_tile(m, 1024)
    tn = _largest_tile(n, 2048, LANES)
    return pl.pallas_call(
        _proj_kernel,
        grid=(n // tn, m // tm),
        in_specs=[pl.BlockSpec((tm, d), lambda j, i: (i, 0)),
                  pl.BlockSpec((d, tn), lambda j, i: (0, j))],
        out_specs=pl.BlockSpec((tm, tn), lambda j, i: (i, j)),
        out_shape=jax.ShapeDtypeStruct((m, n), out_dtype),
        compiler_params=_cparams(("parallel", "parallel")),
        name="in_proj",
    )(h, w_p)


def _norm_proj(geo, x_rows, g, mods3, w_main, w_crit):
    m, d = x_rows.shape
    tr = _largest_tile(geo.row_tile, 512)
    h = pl.pallas_call(
        _norm_kernel,
        grid=(m // tr,),
        in_specs=[pl.BlockSpec((tr, d), lambda i: (i, 0)),
                  pl.BlockSpec((1, d), lambda i: (0, 0)),
                  pl.BlockSpec((1, 1, d), lambda i: (geo.mod_row(i, tr), 0, 1)),
                  pl.BlockSpec((1, 1, d), lambda i: (geo.mod_row(i, tr), 0, 0))],
        out_specs=pl.BlockSpec((tr, d), lambda i: (i, 0)),
        out_shape=jax.ShapeDtypeStruct((m, d), BF16),
        compiler_params=_cparams(("parallel",)),
        name="norm1",
    )(x_rows, g.reshape(1, d), mods3, mods3)
    return _proj(h, w_main, BF16), _proj(h, w_crit, F32)


HALO = 16


def _halo_specs(geo, tr, width, col_block, row_block=lambda i: i):
    nbh = geo.m // HALO
    rh = tr // HALO
    return [pl.BlockSpec((HALO, width), lambda *g: (jnp.maximum(row_block(*g) * rh - 1, 0), col_block)),
            pl.BlockSpec((tr, width), lambda *g: (row_block(*g), col_block)),
            pl.BlockSpec((HALO, width), lambda *g: (jnp.minimum((row_block(*g) + 1) * rh, nbh - 1), col_block))]


def _token_conv(prev_ref, x_ref, next_ref, w_ref, first, last):
    x = x_ref[...].astype(F32)
    tr = x.shape[0]
    prev = jnp.where(first, 0.0, prev_ref[...].astype(F32))
    nxt = jnp.where(last, 0.0, next_ref[...].astype(F32))
    ext = jnp.concatenate([prev, x, nxt], axis=0)
    w = w_ref[...]
    y = None
    for j in range(CONV_W):
        off = HALO - CONV_W // 2 + j
        term = ext[off:off + tr] * w[j:j + 1]
        y = term if y is None else y + term
    return y


def _gdn_prep_kernel(geo, tr, prev_ref, x_ref, next_ref, ab0_ref, ab1_ref, cw_ref, alog_ref, dtb_ref,
                     q_ref, k_ref, v_ref, gb_ref):
    first, last = geo.seq_edges(pl.program_id(0), tr)
    s = _silu(_token_conv(prev_ref, x_ref, next_ref, cw_ref, first, last))
    for h in range(N_HEADS):
        sl = slice(h * HEAD_DIM, (h + 1) * HEAD_DIM)
        qh = s[:, sl]
        kh = s[:, MIX_W + h * HEAD_DIM:MIX_W + (h + 1) * HEAD_DIM]
        q_ref[:, sl] = (qh * (lax.rsqrt(jnp.sum(qh * qh, axis=-1, keepdims=True) + EPS)
                              * HEAD_DIM ** -0.5)).astype(q_ref.dtype)
        k_ref[:, sl] = (kh * lax.rsqrt(jnp.sum(kh * kh, axis=-1, keepdims=True) + EPS)).astype(k_ref.dtype)
    v_ref[...] = s[:, 2 * MIX_W:].astype(v_ref.dtype)
    lane = lax.broadcasted_iota(jnp.int32, (tr, LANES), 1)
    for dr, ab_ref in enumerate((ab0_ref, ab1_ref)):
        a = ab_ref[...]
        g = -jnp.exp(alog_ref[dr]) * _softplus(a + dtb_ref[dr])
        gb_ref[dr] = jnp.where(lane < N_HEADS, g, _sigmoid(a))


def _gdn_prep(geo, p, pc, conv_w, a_log, dt_bias):
    tr = _largest_tile(geo.row_tile, 256)
    m = geo.m
    lane_pad = lambda v: jnp.pad(v.astype(F32), ((0, 0), (0, LANES - N_HEADS))).reshape(2, 1, LANES)
    ab_blk = CCOL_AB // LANES
    rows = lambda width: pl.BlockSpec((tr, width), lambda i: (i, 0))
    full3 = pl.BlockSpec((2, 1, LANES), lambda i: (0, 0, 0))
    return pl.pallas_call(
        functools.partial(_gdn_prep_kernel, geo, tr),
        grid=(m // tr,),
        in_specs=_halo_specs(geo, tr, 3 * MIX_W, COL_GDN_QKV // (3 * MIX_W)) + [
            pl.BlockSpec((tr, LANES), lambda i: (i, ab_blk)),
            pl.BlockSpec((tr, LANES), lambda i: (i, ab_blk + 1)),
            pl.BlockSpec((CONV_W, 3 * MIX_W), lambda i: (0, 0)), full3, full3],
        out_specs=[rows(MIX_W), rows(MIX_W), rows(MIX_W),
                   pl.BlockSpec((2, tr, LANES), lambda i: (0, i, 0))],
        out_shape=[jax.ShapeDtypeStruct((m, MIX_W), BF16)] * 3 + [jax.ShapeDtypeStruct((2, m, LANES), F32)],
        compiler_params=_cparams(("parallel",)),
        name="gdn_prep",
    )(p, p, p, pc, pc, conv_w.astype(F32), lane_pad(a_log), lane_pad(dt_bias))


def _chunk_index(geo, d, j):
    nc_ctx, nc_lat = geo.n_ctx // CHUNK, geo.n_lat // CHUNK
    jc = jnp.where(d == 0, j, nc_ctx - 1 - j)
    jl = jnp.where(d == 0, j - nc_ctx, nc_lat - 1 - (j - nc_ctx))
    return jnp.where(j < nc_ctx, nc_lat + jc, jl)


def _scan_grid(geo):
    return (2, geo.n_tot // CHUNK)


def _chunk_spec(geo, width, col_block=0):
    return pl.BlockSpec((geo.b, CHUNK, width), lambda d, j: (0, _chunk_index(geo, d, j), col_block))


def _dir_chunk_spec(geo, width):
    return pl.BlockSpec((1, geo.b, CHUNK, width), lambda d, j: (d, 0, _chunk_index(geo, d, j), 0))


def _per_dir_spec(shape):
    return pl.BlockSpec((1,) + shape, lambda d, j: (d,) + (0,) * len(shape))


N_LEVELS = int(math.log2(CHUNK))


def _chunk_consts():
    t = np.arange(CHUNK)
    tt, uu = t[:, None], t[None, :]
    tri = uu <= tt
    pair = []
    for lv in range(N_LEVELS):
        hi = ((t >> lv) & 1) == 1
        pair.append(((uu >> (lv + 1)) == (tt >> (lv + 1))) & hi[:, None] & (~hi)[None, :])
    pair = np.stack(pair)
    both = lambda a: np.stack([a, a[..., ::-1, ::-1]]).astype(np.float32)
    return jnp.asarray(both(tri)), jnp.asarray(both(pair))


def _gdn_scan_kernel(n_batch, qf_ref, kf_ref, vf_ref, qb_ref, kb_ref, vb_ref, gbf_ref, gbb_ref, tri_ref,
                     pair_ref, of_ref, ob_ref, s_ref):
    @pl.when(pl.program_id(0) == 0)
    def _():
        s_ref[...] = jnp.zeros_like(s_ref)

    row = lax.broadcasted_iota(jnp.int32, (CHUNK, CHUNK), 0)
    col = lax.broadcasted_iota(jnp.int32, (CHUNK, CHUNK), 1)
    eye = (row == col).astype(F32)
    chains = []
    for d, (q_ref, k_ref, v_ref, gb_ref, o_ref) in enumerate(((qf_ref, kf_ref, vf_ref, gbf_ref, of_ref),
                                                              (qb_ref, kb_ref, vb_ref, gbb_ref, ob_ref))):
        tri = tri_ref[d]
        incl = tri > 0.5
        for b in range(n_batch):
            gb = gb_ref[0, b]
            gcum = _dot(tri, gb, precision=HIGHEST)
            gcum_t = gcum.T
            gtot = jnp.sum(gb, axis=0, keepdims=True)
            q, k, v = q_ref[b].astype(F32), k_ref[b].astype(F32), v_ref[b].astype(F32)
            for h in range(N_HEADS):
                sl = slice(h * HEAD_DIM, (h + 1) * HEAD_DIM)
                chains.append(dict(d=d, b=b, h=h, sl=sl, o_ref=o_ref, incl=incl, bc=gcum[:, h:h + 1],
                                   br=gcum_t[h:h + 1, :], beta=gb[:, N_HEADS + h:N_HEADS + h + 1],
                                   bl=gtot[:, h:h + 1], q=q[:, sl], k=k[:, sl], v=v[:, sl], s=s_ref[d, b, h]))
    for c in chains:
        c["decay"] = jnp.where(c["incl"], jnp.exp(jnp.where(c["incl"], c["bc"] - c["br"], 0.0)), 0.0)
        c["kbeta"] = c["k"] * c["beta"]
        c["k16"] = c["k"].astype(BF16)
        c["mm"] = _dot_nt(c["kbeta"].astype(BF16), c["k16"]) * c["decay"]
        c["dinv"] = eye - pair_ref[c["d"], 0] * c["mm"]
    for lv in range(1, N_LEVELS):
        for c in chains:
            c["ld"] = _bdot(pair_ref[c["d"], lv] * c["mm"], c["dinv"])
        for c in chains:
            c["dinv"] = c["dinv"] - _bdot(c["dinv"], c["ld"])
    for c in chains:
        c["eb"] = jnp.exp(c["bc"])
        rhs = jnp.concatenate([c["v"] * c["beta"], c["kbeta"] * c["eb"]], axis=-1)
        c["sol"] = rhs + _bdot(c["dinv"] - eye, rhs)
        c["attn"] = _dot_nt(c["q"].astype(BF16), c["k16"]) * c["decay"]
    for c in chains:
        c["v_new"] = c["sol"][:, :HEAD_DIM] - _bdot(c["sol"][:, HEAD_DIM:], c["s"])
    for c in chains:
        c["o_ref"][c["b"], :, c["sl"]] = (_bdot(c["q"] * c["eb"], c["s"])
                                          + _bdot(c["attn"], c["v_new"])).astype(c["o_ref"].dtype)
        s_ref[c["d"], c["b"], c["h"]] = (c["s"] * jnp.exp(c["bl"])
                                         + _bdot_tn(c["k"] * jnp.exp(c["bl"] - c["bc"]), c["v_new"]))


def _state_scratch(geo):
    return pltpu.VMEM((geo.b, N_HEADS, HEAD_DIM, HEAD_DIM), F32)


def _gdn_scan(geo, q, k, v, gb):
    tri, pair = _chunk_consts()
    view = lambda a: a.reshape(geo.b, geo.n_tot, a.shape[-1])
    chunk = lambda d, width: pl.BlockSpec((geo.b, CHUNK, width), lambda j: (0, _chunk_index(geo, d, j), 0))
    gate = lambda d: pl.BlockSpec((1, geo.b, CHUNK, LANES), lambda j: (d, 0, _chunk_index(geo, d, j), 0))
    full = lambda shape: pl.BlockSpec(shape, lambda j: (0,) * len(shape))
    q3, k3, v3 = view(q), view(k), view(v)
    gb4 = gb.reshape(2, geo.b, geo.n_tot, LANES)
    o_f, o_b = pl.pallas_call(
        functools.partial(_gdn_scan_kernel, geo.b),
        grid=(geo.n_tot // CHUNK,),
        in_specs=[chunk(0, MIX_W)] * 3 + [chunk(1, MIX_W)] * 3 + [gate(0), gate(1),
                  full((2, CHUNK, CHUNK)), full((2, N_LEVELS, CHUNK, CHUNK))],
        out_specs=[chunk(0, MIX_W), chunk(1, MIX_W)],
        out_shape=[jax.ShapeDtypeStruct((geo.b, geo.n_tot, MIX_W), BF16)] * 2,
        scratch_shapes=[pltpu.VMEM((2, geo.b, N_HEADS, HEAD_DIM, HEAD_DIM), F32)],
        compiler_params=_cparams(("arbitrary",)),
        name="gdn_scan",
    )(q3, k3, v3, q3, k3, v3, gb4, gb4, tri, pair)
    return o_f.reshape(geo.m, MIX_W), o_b.reshape(geo.m, MIX_W)


def _hgrn_consts():
    t = np.arange(CHUNK)
    tt, uu = t[:, None], t[None, :]
    a_all, rowhi, pair = [(uu <= tt)], [], [(uu == tt)]
    for lv in range(N_LEVELS):
        hi = ((t >> lv) & 1) == 1
        same_half = (uu >> lv) == (tt >> lv)
        a_all.append(same_half & np.where(hi[:, None], uu <= tt, uu > tt))
        rowhi.append(np.broadcast_to(hi[:, None], (CHUNK, HEAD_DIM)))
        pair.append(((uu >> (lv + 1)) == (tt >> (lv + 1))) & hi[:, None] & (~hi)[None, :])
    mirror = lambda a: a[::-1, ::-1] if a.shape[1] == CHUNK else a[::-1]
    stack = lambda xs: np.stack([np.stack(xs), np.stack([mirror(x) for x in xs])]).astype(np.float32)
    a = stack(a_all)
    return (jnp.asarray(a.reshape(2, (N_LEVELS + 1) * CHUNK, CHUNK)), jnp.asarray(stack(rowhi)),
            jnp.asarray(stack(pair)))


def _hgrn_scan_kernel(n_batch, q_ref, i_ref, z_ref, lb_ref, a_ref, rowhi_ref, pair_ref, o_ref, st_ref):
    @pl.when(pl.program_id(1) == 0)
    def _():
        st_ref[...] = jnp.zeros_like(st_ref)

    lb = lb_ref[0]
    log_lb = jnp.log(jnp.maximum(lb, LB_MIN))
    log_1mlb = jnp.log1p(-lb)
    a16 = a_ref[0].astype(BF16)
    a16 = jnp.concatenate([a16, a16, a16], axis=1)

    def prep(b):
        z = z_ref[b]
        other = log_1mlb + _log_sigmoid(z)
        log_f = jnp.maximum(log_lb, other) + jnp.log1p(jnp.exp(-jnp.abs(log_lb - other)))
        return dict(b=b, key=(1.0 - lb) * _sigmoid(-z), q=_silu(q_ref[b].astype(F32)) * HEAD_DIM ** -0.5,
                    v=i_ref[b],
                    e_all=_dot_exact01(a16, log_f),
                    ltot=jnp.sum(log_f, axis=0, keepdims=True))

    def levels(pb, h):
        sl = slice(h * HEAD_DIM, (h + 1) * HEAD_DIM)
        q, k, e = pb["q"][:, sl], pb["key"][:, sl], pb["e_all"][:, sl]
        scores = pair_ref[0, 0] * _bdot_nt(q, k)
        for lv in range(N_LEVELS):
            z16 = (jnp.where(rowhi_ref[0, lv] > 0.5, q, k) * jnp.exp(e[(lv + 1) * CHUNK:(lv + 2) * CHUNK])).astype(BF16)
            scores = scores + pair_ref[0, lv + 1] * _dot_nt(z16, z16)
        return dict(b=pb["b"], h=h, sl=sl, q=q, k=k, v=pb["v"][:, sl], bh=e[:CHUNK], bl=pb["ltot"][:, sl],
                    scores=scores)

    def tail(c):
        st = st_ref[c["b"], c["h"]]
        o_ref[0, c["b"], :, c["sl"]] = (_bdot(c["scores"], c["v"])
                                        + _bdot_nt(c["q"] * jnp.exp(c["bh"]), st)).astype(o_ref.dtype)
        st_ref[c["b"], c["h"]] = st * jnp.exp(c["bl"]) + _bdot_tn(c["v"], c["k"] * jnp.exp(c["bl"] - c["bh"]))

    pending = None
    pb = prep(0)
    for b in range(n_batch):
        pb_next = prep(b + 1) if b + 1 < n_batch else None
        for h in range(N_HEADS):
            cur = levels(pb, h)
            if pending is not None:
                tail(pending)
            pending = cur
        pb = pb_next
    tail(pending)


def _hgrn_scan(geo, p, pc, lb):
    a_all, rowhi, pair = _hgrn_consts()
    p3 = p.reshape(geo.b, geo.n_tot, N_PROJ)
    pc3 = pc.reshape(geo.b, geo.n_tot, N_CRIT)
    return pl.pallas_call(
        functools.partial(_hgrn_scan_kernel, geo.b),
        grid=_scan_grid(geo),
        in_specs=[_chunk_spec(geo, MIX_W, COL_HG_Q // MIX_W), _chunk_spec(geo, MIX_W, COL_HG_I // MIX_W),
                  pl.BlockSpec((geo.b, CHUNK, MIX_W),
                               lambda d, j: (0, _chunk_index(geo, d, j), CCOL_HG_F // MIX_W + d)),
                  _per_dir_spec((1, MIX_W)), _per_dir_spec(((N_LEVELS + 1) * CHUNK, CHUNK)),
                  _per_dir_spec((N_LEVELS, CHUNK, HEAD_DIM)), _per_dir_spec((N_LEVELS + 1, CHUNK, CHUNK))],
        out_specs=_dir_chunk_spec(geo, MIX_W),
        out_shape=jax.ShapeDtypeStruct((2, geo.b, geo.n_tot, MIX_W), BF16),
        scratch_shapes=[_state_scratch(geo)],
        compiler_params=_cparams(("parallel", "arbitrary")),
        name="hgrn_scan",
    )(p3, p3, pc3, lb.reshape(2, 1, MIX_W), a_all, rowhi, pair)


def _ret_consts(n_lat):
    t = np.arange(CHUNK, dtype=np.float64)
    log_gamma = np.log1p(-np.exp2(-5.0 - np.arange(N_HEADS, dtype=np.float64)))
    dmat, qs, ks = [], [], []
    for rev in (False, True):
        te = t[::-1] if rev else t
        diff = te[:, None] - te[None, :]
        dmat.append(np.stack([np.where(diff >= 0, np.exp(lg * np.maximum(diff, 0)), 0.0) for lg in log_gamma]))
        qs.append(np.repeat(np.exp(log_gamma[None, :] * (te[:, None] + 1.0)), HEAD_DIM, axis=1))
        ks.append(np.repeat(np.exp(log_gamma[None, :] * (CHUNK - 1.0 - te[:, None])), HEAD_DIM, axis=1))
    g_chunk = np.repeat(np.exp(log_gamma * CHUNK), HEAD_DIM)[None, :]
    row = np.repeat(np.arange(n_lat // GRID_W, dtype=np.float32), GRID_W)
    colp = np.tile(np.arange(GRID_W, dtype=np.float32), n_lat // GRID_W)
    n_freq = HEAD_DIM // 4
    inv_freq = (np.float32(ROPE_BASE) ** (-np.arange(n_freq, dtype=np.float32) / n_freq)).astype(np.float32)
    ang = np.concatenate([row[:, None] * inv_freq, colp[:, None] * inv_freq], axis=-1).astype(np.float32)
    cos, sin = np.cos(ang), np.sin(ang)
    f32 = lambda a: jnp.asarray(np.asarray(a, np.float32))
    return (f32(np.stack(dmat)), f32(np.stack(qs)), f32(np.stack(ks)), f32(g_chunk),
            f32(np.concatenate([cos, cos], axis=-1)), f32(np.concatenate([-sin, sin], axis=-1)))


def _ret_scan_kernel(n_batch, nc_ctx, qf_ref, kf_ref, vf_ref, qb_ref, kb_ref, vb_ref, cosf_ref, sinf_ref,
                     cosb_ref, sinb_ref, dm_ref, qs_ref, ks_ref, gc_ref, of_ref, ob_ref, s_ref):
    j = pl.program_id(0)

    @pl.when(j == 0)
    def _():
        s_ref[...] = jnp.zeros_like(s_ref)

    is_lat = j >= nc_ctx
    gc = gc_ref[...]
    chains = []
    for d, (q_ref, k_ref, v_ref, cos_ref, sin_ref, o_ref) in enumerate((
            (qf_ref, kf_ref, vf_ref, cosf_ref, sinf_ref, of_ref),
            (qb_ref, kb_ref, vb_ref, cosb_ref, sinb_ref, ob_ref))):
        cos = jnp.where(is_lat, cos_ref[...], 1.0)
        sin = jnp.where(is_lat, sin_ref[...], 0.0)
        for b in range(n_batch):
            q, k, v = q_ref[b].astype(F32), k_ref[b].astype(F32), v_ref[b]
            for h in range(N_HEADS):
                sl = slice(h * HEAD_DIM, (h + 1) * HEAD_DIM)
                qh, kh = q[:, sl], k[:, sl]
                rq = qh * cos + pltpu.roll(qh, HEAD_DIM // 2, 1) * sin
                rk = (kh * cos + pltpu.roll(kh, HEAD_DIM // 2, 1) * sin) * HEAD_DIM ** -0.5
                chains.append(dict(d=d, b=b, h=h, sl=sl, o_ref=o_ref, rq=rq, rk=rk, v=v[:, sl], s=s_ref[d, b, h]))
    for c in chains:
        c["scores"] = _bdot_nt(c["rq"], c["rk"]) * dm_ref[c["d"], c["h"]]
    for c in chains:
        d, sl = c["d"], c["sl"]
        c["o_ref"][c["b"], :, sl] = (_bdot(c["scores"], c["v"])
                                     + _bdot(c["rq"] * qs_ref[d, :, sl], c["s"])).astype(c["o_ref"].dtype)
        s_ref[d, c["b"], c["h"]] = c["s"] * gc[:, sl] + _bdot_tn(c["rk"] * ks_ref[d, :, sl], c["v"])


def _ret_scan(geo, p):
    dmat, qs, ks, gc, cos2, sin2 = _ret_consts(geo.n_lat)
    nc_ctx = geo.n_ctx // CHUNK
    p3 = p.reshape(geo.b, geo.n_tot, N_PROJ)
    chunk = lambda d, c: pl.BlockSpec((geo.b, CHUNK, MIX_W), lambda j: (0, _chunk_index(geo, d, j), c // MIX_W))
    rope = lambda d: pl.BlockSpec((CHUNK, HEAD_DIM), lambda j: (jnp.where(j < nc_ctx, 0, _chunk_index(geo, d, j)), 0))
    full = lambda shape: pl.BlockSpec(shape, lambda j: (0,) * len(shape))
    o_f, o_b = pl.pallas_call(
        functools.partial(_ret_scan_kernel, geo.b, nc_ctx),
        grid=(geo.n_tot // CHUNK,),
        in_specs=[chunk(0, COL_RET_Q), chunk(0, COL_RET_K), chunk(0, COL_RET_V),
                  chunk(1, COL_RET_Q), chunk(1, COL_RET_K), chunk(1, COL_RET_V),
                  rope(0), rope(0), rope(1), rope(1),
                  full((2, N_HEADS, CHUNK, CHUNK)), full((2, CHUNK, MIX_W)), full((2, CHUNK, MIX_W)),
                  full((1, MIX_W))],
        out_specs=[pl.BlockSpec((geo.b, CHUNK, MIX_W), lambda j: (0, _chunk_index(geo, 0, j), 0)),
                   pl.BlockSpec((geo.b, CHUNK, MIX_W), lambda j: (0, _chunk_index(geo, 1, j), 0))],
        out_shape=[jax.ShapeDtypeStruct((geo.b, geo.n_tot, MIX_W), BF16)] * 2,
        scratch_shapes=[pltpu.VMEM((2, geo.b, N_HEADS, HEAD_DIM, HEAD_DIM), F32)],
        compiler_params=_cparams(("arbitrary",)),
        name="ret_scan",
    )(p3, p3, p3, p3, p3, p3, cos2, sin2, cos2, sin2, dmat, qs, ks, gc)
    return o_f.reshape(geo.m, MIX_W), o_b.reshape(geo.m, MIX_W)


def _lru_kernel(geo, tr, rev, prev_ref, x_ref, next_ref, cw_ref, cb_ref, wg_ref, bg_ref, lam_ref,
                o_ref, h_ref):
    b, j = pl.program_id(0), pl.program_id(1)

    @pl.when(j == 0)
    def _():
        h_ref[...] = jnp.zeros_like(h_ref)

    first, last = geo.seq_edges(_lru_block(geo, tr, rev, b, j), tr)
    xb = _token_conv(prev_ref, x_ref, next_ref, cw_ref, first, last) + cb_ref[...]
    gates = _bdot(xb, wg_ref[...]) + bg_ref[...]
    gate_r = _sigmoid(gates[:, :MIX_W])
    gate_i = _sigmoid(gates[:, MIX_W:])
    log_a = LRU_C * gate_r * _log_sigmoid(lam_ref[...])
    a = jnp.exp(log_a)
    u = xb * gate_i * jnp.sqrt(jnp.maximum(-jnp.tanh(log_a) * (a * a + 1.0), SQRT_MIN))
    ng = tr // SUBLANES
    a3 = a.reshape(ng, SUBLANES, MIX_W)
    u3 = u.reshape(ng, SUBLANES, MIX_W)
    sub = lax.broadcasted_iota(jnp.int32, (ng, SUBLANES, MIX_W), 1)
    step = 1
    while step < SUBLANES:
        shift = SUBLANES - step if rev else step
        valid = (sub < SUBLANES - step) if rev else (sub >= step)
        u3 = jnp.where(valid, a3 * pltpu.roll(u3, shift, 1) + u3, u3)
        a3 = jnp.where(valid, a3 * pltpu.roll(a3, shift, 1), a3)
        step *= 2
    edge = 0 if rev else SUBLANES - 1
    h_in = jnp.broadcast_to(h_ref[...], (SUBLANES, MIX_W))
    h_groups = [None] * ng
    for g in (range(ng - 1, -1, -1) if rev else range(ng)):
        h_groups[g] = h_in
        h_in = a3[g, edge:edge + 1] * h_in + u3[g, edge:edge + 1]
    h = a3 * jnp.stack(h_groups, axis=0) + u3
    o_ref[...] = h.reshape(tr, MIX_W).astype(o_ref.dtype)
    h_ref[...] = h_in[0:1]


def _lru_block(geo, tr, rev, b, j):
    nb_ctx, nb_lat = geo.n_ctx // tr, geo.n_lat // tr
    jc = nb_ctx - 1 - j if rev else j
    jl = nb_lat - 1 - (j - nb_ctx) if rev else j - nb_ctx
    return b * (geo.n_tot // tr) + jnp.where(j < nb_ctx, nb_lat + jc, jl)


def _lru_scan(geo, p, conv_w, conv_b, wa, ba, wx, bx, lam, rev):
    tr = _largest_tile(geo.row_tile, 256)
    d = 1 if rev else 0
    blockdiag = lambda w: jax.scipy.linalg.block_diag(*[w[n] for n in range(LRU_BLOCKS)])
    wg = jnp.concatenate([blockdiag(wa[d]), blockdiag(wx[d])], axis=1).astype(BF16)
    bg = jnp.concatenate([ba[d], bx[d]]).astype(F32).reshape(1, 2 * MIX_W)
    blk = lambda b, j: _lru_block(geo, tr, rev, b, j)
    full = lambda shape: pl.BlockSpec(shape, lambda b, j: (0,) * len(shape))
    return pl.pallas_call(
        functools.partial(_lru_kernel, geo, tr, rev),
        grid=(geo.b, (geo.n_ctx + geo.n_lat) // tr),
        in_specs=_halo_specs(geo, tr, MIX_W, COL_LRU_X // MIX_W, blk) + [
                  full((CONV_W, MIX_W)), full((1, MIX_W)), full((MIX_W, 2 * MIX_W)),
                  full((1, 2 * MIX_W)), full((1, MIX_W))],
        out_specs=pl.BlockSpec((tr, MIX_W), lambda b, j: (blk(b, j), 0)),
        out_shape=jax.ShapeDtypeStruct((geo.m, MIX_W), BF16),
        scratch_shapes=[pltpu.VMEM((1, MIX_W), F32)],
        compiler_params=_cparams(("parallel", "arbitrary")),
        name="lru_scan_bwd" if rev else "lru_scan_fwd",
    )(p, p, p, conv_w.astype(F32), conv_b.astype(F32).reshape(1, MIX_W), wg, bg,
      lam[d].astype(F32).reshape(1, MIX_W))


def _gelu_tanh(x):
    return 0.5 * x * (1.0 + jnp.tanh(math.sqrt(2.0 / math.pi) * (x + 0.044715 * (x * x * x))))


def _gated_head_norm(o, g, z):
    parts = []
    for h in range(N_HEADS):
        oh = o[:, h * HEAD_DIM:(h + 1) * HEAD_DIM]
        parts.append(oh * lax.rsqrt(jnp.mean(oh * oh, axis=-1, keepdims=True) + EPS) * g)
    return jnp.concatenate(parts, axis=-1) * _silu(z)


def _merge_kernel(ogf_ref, ogb_ref, oh_ref, olf_ref, olb_ref, orf_ref, orb_ref, gz_ref, hg_ref, ly_ref, rg_ref,
                  mg_ref, x_ref, gt_ref, gng_ref, hng_ref, rng_ref, wb_ref, wo_ref, out_ref):
    f32 = lambda v: v.astype(F32)
    branches = (
        _gated_head_norm(f32(ogf_ref[...]) + f32(ogb_ref[...]), gng_ref[...], f32(gz_ref[...])),
        _gated_head_norm(f32(oh_ref[0]) + f32(oh_ref[1]), hng_ref[...], f32(hg_ref[...])),
        (f32(olf_ref[...]) + f32(olb_ref[...])) * _gelu_tanh(f32(ly_ref[...])),
        _gated_head_norm(f32(orf_ref[...]) + f32(orb_ref[...]), rng_ref[...], f32(rg_ref[...])),
    )
    mix = None
    for n, y in enumerate(branches):
        term = _sigmoid(f32(mg_ref[:, n * D_MODEL:(n + 1) * D_MODEL])) * _bdot(y, wb_ref[n])
        mix = term if mix is None else mix + term
    out_ref[...] = x_ref[...] + gt_ref[0] * _bdot(mix, wo_ref[...])


def _merge(geo, compact, x_rows, p, o_gdn, o_hgrn, o_lru_f, o_lru_b, o_ret, mods3, gdn_g, hgrn_g, ret_g,
           w_branch, w_out):
    tm = _largest_tile(geo.row_tile, 256)
    n_rows = geo.m_lat if compact else geo.m
    src = lambda i: geo.full_tile(i, tm, compact)
    both = pl.BlockSpec((2, tm, MIX_W), lambda i: (0, src(i), 0))
    rows = lambda width: pl.BlockSpec((tm, width), lambda i: (src(i), 0))
    col = lambda c, width: pl.BlockSpec((tm, width), lambda i: (src(i), c // width))
    full = lambda shape: pl.BlockSpec(shape, lambda i: (0,) * len(shape))
    hd = lambda g: g.astype(F32).reshape(1, HEAD_DIM)
    return pl.pallas_call(
        _merge_kernel,
        grid=(n_rows // tm,),
        in_specs=[rows(MIX_W), rows(MIX_W), both, rows(MIX_W), rows(MIX_W), rows(MIX_W), rows(MIX_W),
                  col(COL_GDN_Z, MIX_W), col(COL_HG_G, MIX_W), col(COL_LRU_Y, MIX_W), col(COL_RET_G, MIX_W),
                  col(COL_MERGE, N_BRANCH * D_MODEL), rows(D_MODEL),
                  pl.BlockSpec((1, 1, D_MODEL), lambda i: (geo.mod_row(i, tm, compact), 0, 2)),
                  full((1, HEAD_DIM)), full((1, HEAD_DIM)), full((1, HEAD_DIM)),
                  full((N_BRANCH, MIX_W, D_MODEL)), full((D_MODEL, D_MODEL))],
        out_specs=pl.BlockSpec((tm, D_MODEL), lambda i: (i, 0)),
        out_shape=jax.ShapeDtypeStruct((n_rows, D_MODEL), F32),
        compiler_params=_cparams(("parallel",)),
        name="merge",
    )(o_gdn[0], o_gdn[1], o_hgrn, o_lru_f, o_lru_b, o_ret[0], o_ret[1], p, p, p, p, p, x_rows, mods3, hd(gdn_g), hd(hgrn_g),
      hd(ret_g), w_branch.astype(BF16), w_out.astype(BF16))


ROW_TILE = D_MODEL // LANES
assert ROW_TILE == SUBLANES


def _store_row_tiles(ref, val):
    n = val.shape[0]
    v = jnp.stack([val[:, s * LANES:(s + 1) * LANES].reshape(n // SUBLANES, SUBLANES, LANES)
                   for s in range(ROW_TILE)], axis=1)
    ref[...] = jnp.swapaxes(v, 1, 2).reshape(n * ROW_TILE, LANES)


def _load_row_tiles(ref_at, start, n):
    v = ref_at[pl.ds(start, n * ROW_TILE), :].reshape(n // SUBLANES, SUBLANES, ROW_TILE, LANES)
    v = jnp.swapaxes(v, 1, 2)
    return jnp.concatenate([v[:, s].reshape(n, LANES) for s in range(ROW_TILE)], axis=1)


def _router_kernel(n_experts, x_ref, g_ref, sc_ref, sh_ref, rw_ref, rb_ref, f_ref, idx_ref, wt_ref):
    f = _rms_mod(x_ref[...], g_ref[...], sc_ref[0], sh_ref[0])
    _store_row_tiles(f_ref, f)
    logits = _dot(f, rw_ref[...], precision=HIGHEST) + rb_ref[...]
    lane = lax.broadcasted_iota(jnp.int32, logits.shape, 1)
    logits = jnp.where(lane < n_experts, logits, NEG_BIG)
    vals, idxs = [], []
    for _ in range(TOP_K):
        mx = jnp.max(logits, axis=-1, keepdims=True)
        ix = jnp.min(jnp.where(logits == mx, lane, LANES), axis=-1, keepdims=True)
        vals.append(mx)
        idxs.append(ix)
        logits = jnp.where(lane == ix, -jnp.inf, logits)
    ex = [jnp.exp(vl - vals[0]) for vl in vals]
    tot = ex[0] + ex[1] + ex[2] + ex[3]
    idx_out = jnp.zeros(lane.shape, jnp.int32)
    wt_out = jnp.zeros(lane.shape, F32)
    for kk in range(TOP_K):
        idx_out = jnp.where(lane == kk, idxs[kk], idx_out)
        wt_out = jnp.where(lane == kk, ex[kk] / tot, wt_out)
    idx_ref[...] = idx_out
    wt_ref[...] = wt_out


def _router(geo, compact, x_rows, g, mods3, router_w, router_b):
    n_rows, d = x_rows.shape
    n_experts = router_w.shape[1]
    tm = _largest_tile(geo.row_tile, 256)
    rw = jnp.pad(router_w.astype(F32), ((0, 0), (0, LANES - n_experts)))
    rb = jnp.pad(router_b.astype(F32), (0, LANES - n_experts)).reshape(1, LANES)
    rows = lambda width: pl.BlockSpec((tm, width), lambda i: (i, 0))
    full = lambda shape: pl.BlockSpec(shape, lambda i: (0,) * len(shape))
    return pl.pallas_call(
        functools.partial(_router_kernel, n_experts),
        grid=(n_rows // tm,),
        in_specs=[rows(d), full((1, d)),
                  pl.BlockSpec((1, 1, d), lambda i: (geo.mod_row(i, tm, compact), 0, 4)),
                  pl.BlockSpec((1, 1, d), lambda i: (geo.mod_row(i, tm, compact), 0, 3)),
                  full((d, LANES)), full((1, LANES))],
        out_specs=[pl.BlockSpec((tm * ROW_TILE, LANES), lambda i: (i, 0)), rows(LANES), rows(LANES)],
        out_shape=[jax.ShapeDtypeStruct((n_rows * ROW_TILE, LANES), F32),
                   jax.ShapeDtypeStruct((n_rows, LANES), jnp.int32), jax.ShapeDtypeStruct((n_rows, LANES), F32)],
        compiler_params=_cparams(("parallel",)),
        name="router",
    )(x_rows, g.astype(F32).reshape(1, d), mods3, mods3, rw, rb)


def _route_plan(top_idx, n_experts):
    m = top_idx.shape[0]
    n_assign = m * TOP_K
    flat_e = top_idx.reshape(-1)
    order = jnp.argsort(flat_e).astype(jnp.int32)
    blk = _largest_tile(n_assign, 512)
    onehot = (flat_e[:, None] == jnp.arange(n_experts, dtype=jnp.int32)[None, :]).astype(F32)
    oh3 = onehot.reshape(n_assign // blk, blk, n_experts)
    strict_lower = jnp.asarray(np.tril(np.ones((blk, blk), np.float32), k=-1))
    block_tot = jnp.sum(oh3, axis=1)
    block_off = jnp.cumsum(block_tot, axis=0) - block_tot
    prefix = jnp.einsum("ij,bjk->bik", strict_lower, oh3) + block_off[:, None, :]
    rank_in_expert = jnp.sum(prefix * oh3, axis=-1).reshape(n_assign).astype(jnp.int32)
    counts = jnp.sum(block_tot, axis=0).astype(jnp.int32)
    padded = (counts + ROUTE_BLOCK - 1) // ROUTE_BLOCK * ROUTE_BLOCK
    pad_end = jnp.cumsum(padded)
    pad_start = pad_end - padded
    start = jnp.cumsum(counts) - counts
    n_blocks = -(-n_assign // ROUTE_BLOCK) + n_experts
    cap = n_blocks * ROUTE_BLOCK
    block_row0 = jnp.arange(n_blocks, dtype=jnp.int32) * ROUTE_BLOCK
    block_expert = jnp.minimum(jnp.sum(pad_end[None, :] <= block_row0[:, None], axis=1, dtype=jnp.int32),
                               n_experts - 1)
    n_used = (pad_end[-1:] // ROUTE_BLOCK).astype(jnp.int32)
    row = jnp.arange(cap, dtype=jnp.int32)
    row_e = jnp.repeat(block_expert, ROUTE_BLOCK)
    within = row - pad_start[row_e]
    valid = jnp.logical_and(within >= 0, within < counts[row_e])
    src_tok = jnp.where(valid, order[jnp.clip(start[row_e] + within, 0, n_assign - 1)] // TOP_K, 0)
    dest = pad_start[flat_e] + rank_in_expert
    return src_tok.reshape(n_blocks, 1, ROUTE_BLOCK), block_expert, n_used, dest


def _start_row_gather(idx_ref, n, src_hbm, buf, slot, sem):
    for r in range(n):
        src_row = pl.multiple_of(idx_ref[0, 0, r] * ROW_TILE, ROW_TILE)
        pltpu.make_async_copy(src_hbm.at[pl.ds(src_row, ROW_TILE)], buf.at[slot, pl.ds(r * ROW_TILE, ROW_TILE)],
                              sem.at[slot]).start(priority=r % 2)


def _wait_row_gather(n, src_hbm, buf, slot, sem):
    pltpu.make_async_copy(src_hbm.at[pl.ds(0, n * ROW_TILE)], buf.at[slot], sem.at[slot]).wait()


def _gather_step(cur_ref, nxt_ref, n, src_hbm, buf, sem):
    i, nb = pl.program_id(0), pl.num_programs(0)
    slot = i % 2

    @pl.when(i == 0)
    def _():
        _start_row_gather(cur_ref, n, src_hbm, buf, 0, sem)

    _wait_row_gather(n, src_hbm, buf, slot, sem)
    _start_row_gather(nxt_ref, n, src_hbm, buf, 1 - slot, sem)
    return slot


def _gather_drain(n, src_hbm, buf, slot, sem):
    @pl.when(pl.program_id(0) == pl.num_programs(0) - 1)
    def _():
        _wait_row_gather(n, src_hbm, buf, 1 - slot, sem)


def _expert_kernel(be_ref, nu_ref, src_ref, nxt_ref, f_hbm, w1_ref, b1_ref, w2_ref, b2_ref, y_ref,
                   buf, sem, w1b_ref, w2b_ref):
    i = pl.program_id(0)
    ff = w2_ref.shape[1]
    slot = _gather_step(src_ref, nxt_ref, ROUTE_BLOCK, f_hbm, buf, sem)

    changed = jnp.logical_or(i == 0, be_ref[i] != be_ref[jnp.maximum(i - 1, 0)])
    r1, r2 = w1_ref.shape[1] // CAST_STEPS, w2_ref.shape[1] // CAST_STEPS

    def cast_rows(t, carry):
        o1, o2 = pl.multiple_of(t * r1, r1), pl.multiple_of(t * r2, r2)
        w1b_ref[pl.ds(o1, r1), :] = w1_ref[0, pl.ds(o1, r1), :].astype(BF16)
        w2b_ref[pl.ds(o2, r2), :] = w2_ref[0, pl.ds(o2, r2), :].astype(BF16)
        return carry

    lax.fori_loop(0, jnp.where(changed, CAST_STEPS, 0), cast_rows, 0)

    @pl.when(i >= nu_ref[0])
    def _():
        y_ref[...] = jnp.zeros_like(y_ref)

    @pl.when(i < nu_ref[0])
    def _():
        z = _bdot(_load_row_tiles(buf.at[slot], 0, ROUTE_BLOCK), w1b_ref[...]) + b1_ref[0]
        glu = jnp.minimum(z[:, :ff], SWIGLU_LIMIT)
        lin = jnp.clip(z[:, ff:], -SWIGLU_LIMIT, SWIGLU_LIMIT)
        act = glu * _sigmoid(SWIGLU_ALPHA * glu) * (lin + 1.0)
        _store_row_tiles(y_ref, _bdot(act, w2b_ref[...]) + b2_ref[0])

    _gather_drain(ROUTE_BLOCK, f_hbm, buf, slot, sem)


def _experts(f_tiles, src_tok, block_expert, n_used, layer, w1_all, b1_all, w2_all, b2_all):
    n_blocks = src_tok.shape[0]
    n_layers, n_experts, d, ff2 = w1_all.shape
    ff = w2_all.shape[2]
    w1 = w1_all.reshape(n_layers * n_experts, d, ff2)
    w2 = w2_all.reshape(n_layers * n_experts, ff, d)
    b1 = b1_all.reshape(n_layers * n_experts, ff2)
    b2 = b2_all.reshape(n_layers * n_experts, d)
    block_expert = block_expert + layer * n_experts
    n_experts = n_layers * n_experts
    grid_spec = pltpu.PrefetchScalarGridSpec(
        num_scalar_prefetch=2,
        grid=(n_blocks,),
        in_specs=[pl.BlockSpec((1, 1, ROUTE_BLOCK), lambda i, be, nu: (i, 0, 0), memory_space=pltpu.SMEM),
                  pl.BlockSpec((1, 1, ROUTE_BLOCK), lambda i, be, nu: (jnp.minimum(i + 1, n_blocks - 1), 0, 0),
                               memory_space=pltpu.SMEM),
                  pl.BlockSpec(memory_space=pl.ANY),
                  pl.BlockSpec((1, d, ff2), lambda i, be, nu: (be[i], 0, 0)),
                  pl.BlockSpec((1, 1, ff2), lambda i, be, nu: (be[i], 0, 0)),
                  pl.BlockSpec((1, ff, d), lambda i, be, nu: (be[i], 0, 0)),
                  pl.BlockSpec((1, 1, d), lambda i, be, nu: (be[i], 0, 0))],
        out_specs=pl.BlockSpec((ROUTE_BLOCK * ROW_TILE, LANES), lambda i, be, nu: (i, 0)),
        scratch_shapes=[pltpu.VMEM((2, ROUTE_BLOCK * ROW_TILE, LANES), F32), pltpu.SemaphoreType.DMA((2,)),
                        pltpu.VMEM((d, ff2), BF16), pltpu.VMEM((ff, d), BF16)],
    )
    return pl.pallas_call(
        _expert_kernel,
        grid_spec=grid_spec,
        out_shape=jax.ShapeDtypeStruct((n_blocks * ROUTE_BLOCK * ROW_TILE, LANES), F32),
        compiler_params=_cparams(("arbitrary",)),
        name="experts",
    )(block_expert, n_used, src_tok, src_tok, f_tiles, w1, b1.astype(F32).reshape(n_experts, 1, ff2),
      w2, b2.astype(F32).reshape(n_experts, 1, d))


def _combine_kernel(tc, final, dst_ref, nxt_ref, y_hbm, x_ref, wt_ref, gt_ref, fg_ref, out_ref, buf, sem):
    n = TOP_K * tc
    slot = _gather_step(dst_ref, nxt_ref, n, y_hbm, buf, sem)
    wt = wt_ref[...]
    acc = None
    for kk in range(TOP_K):
        term = wt[:, kk:kk + 1] * _load_row_tiles(buf.at[slot], kk * tc * ROW_TILE, tc)
        acc = term if acc is None else acc + term
    out = x_ref[...] + gt_ref[0] * acc
    if final:
        out = out * lax.rsqrt(jnp.mean(out * out, axis=-1, keepdims=True) + EPS) * fg_ref[...]
    out_ref[...] = out
    _gather_drain(n, y_hbm, buf, slot, sem)


def _combine(geo, y_pad, dest, x_rows, wt, mods3, final_g, final):
    n_rows, d = x_rows.shape
    tc = _largest_tile(geo.row_tile, 256)
    nt = n_rows // tc
    dst = dest.reshape(nt, tc, TOP_K).transpose(0, 2, 1).reshape(nt, 1, TOP_K * tc)
    rows = lambda width: pl.BlockSpec((tc, width), lambda i: (i, 0))
    return pl.pallas_call(
        functools.partial(_combine_kernel, tc, final),
        grid=(nt,),
        in_specs=[pl.BlockSpec((1, 1, TOP_K * tc), lambda i: (i, 0, 0), memory_space=pltpu.SMEM),
                  pl.BlockSpec((1, 1, TOP_K * tc), lambda i: (jnp.minimum(i + 1, nt - 1), 0, 0),
                               memory_space=pltpu.SMEM),
                  pl.BlockSpec(memory_space=pl.ANY), rows(d), rows(LANES),
                  pl.BlockSpec((1, 1, d), lambda i: (geo.mod_row(i, tc, final), 0, 5)),
                  pl.BlockSpec((1, d), lambda i: (0, 0))],
        out_specs=rows(d),
        out_shape=jax.ShapeDtypeStruct((n_rows, d), F32),
        scratch_shapes=[pltpu.VMEM((2, TOP_K * tc * ROW_TILE, LANES), F32), pltpu.SemaphoreType.DMA((2,))],
        compiler_params=_cparams(("arbitrary",)),
        name="moe_combine",
    )(dst, dst, y_pad, x_rows, wt, mods3, final_g.astype(F32).reshape(1, d))


def _moe(geo, x_rows, g, mods3, router_w, router_b, layer, w1_all, b1_all, w2_all, b2_all, final_g, final):
    f_tiles, top_idx, top_w = _router(geo, final, x_rows, g, mods3, router_w, router_b)
    src_tok, block_expert, n_used, dest = _route_plan(top_idx[:, :TOP_K], router_w.shape[1])
    y_tiles = _experts(f_tiles, src_tok, block_expert, n_used, layer, w1_all, b1_all, w2_all, b2_all)
    return _combine(geo, y_tiles, dest, x_rows, top_w, mods3, final_g, final)


def kernel(x, c, ctx, c_ctx, w_mod, b_mod, norm1_g, norm2_g, w_in, gdn_conv_w, gdn_a_log, gdn_dt_bias, gdn_norm_g, hgrn_lb, hgrn_norm_g, lru_conv_w, lru_conv_b, lru_wa, lru_ba, lru_wx, lru_bx, lru_lambda, ret_norm_g, w_branch, w_out, router_w, router_b, moe_w1, moe_b1, moe_w2, moe_b2, final_norm_g):
    b, n_lat, d = x.shape
    n_ctx = ctx.shape[1]
    depth = w_mod.shape[0]
    assert d == D_MODEL and w_branch.shape[2] == MIX_W
    geo = _Geo(b, n_lat, n_ctx)
    lb_soft = jax.nn.softmax(hgrn_lb.astype(F32), axis=0)
    lower_bounds = jnp.clip(jnp.cumsum(lb_soft, axis=0) - lb_soft[0], 0.0, LB_MAX)
    n_cond = -(-(b + 1) // SUBLANES) * SUBLANES
    cc = jnp.concatenate([c, c_ctx[None, :], jnp.zeros((n_cond - b - 1, d), c.dtype)], axis=0)
    mods = _modulation(cc.astype(F32), w_mod, b_mod)
    rows = jnp.concatenate([x, ctx], axis=1).reshape(geo.m, d)
    flat = lambda o: o.reshape(2, geo.m, MIX_W)
    for l in range(depth):
        last = l == depth - 1
        mods3 = mods[l].reshape(n_cond, 1, N_MOD * d)
        p, pc = _norm_proj(geo, rows, norm1_g[l], mods3, *_reorder_w_in(w_in[l]))
        gq, gk, gv, gb = _gdn_prep(geo, p, pc, gdn_conv_w[l], gdn_a_log[l], gdn_dt_bias[l])
        o_gdn = _gdn_scan(geo, gq, gk, gv, gb)
        o_hgrn = flat(_hgrn_scan(geo, p, pc, lower_bounds[l]))
        lru_args = (geo, p, lru_conv_w[l], lru_conv_b[l], lru_wa[l], lru_ba[l], lru_wx[l], lru_bx[l],
                    lru_lambda[l])
        o_lru_f = _lru_scan(*lru_args, rev=False)
        o_lru_b = _lru_scan(*lru_args, rev=True)
        o_ret = _ret_scan(geo, p)
        rows = _merge(geo, last, rows, p, o_gdn, o_hgrn, o_lru_f, o_lru_b, o_ret, mods3, gdn_norm_g[l],
                      hgrn_norm_g[l], ret_norm_g[l], w_branch[l], w_out[l])
        rows = _moe(geo, rows, norm2_g[l], mods3, router_w[l], router_b[l], l, moe_w1, moe_b1, moe_w2, moe_b2,
                    final_norm_g, last)
    return rows.reshape(b, n_lat, d)
```
